```python
import math
import jax
import jax.numpy as jnp
from jax import lax
import numpy as np

D_MODEL = 1024
BATCH = 2
SEQ = 8192
DEPTH = 1
DEC_BATCH = 128
DEC_SEQ = 1
PAST_LEN = 16384
PAGE_SIZE = 128

ATT_HEADS = 16
ATT_KV_HEADS = 4
ATT_HEAD_DIM = 64
ATT_GROUP = ATT_HEADS // ATT_KV_HEADS
ATT_WIDTH = ATT_HEADS * ATT_HEAD_DIM
KV_WIDTH = ATT_KV_HEADS * ATT_HEAD_DIM
WINDOW = 128
ATT_BLOCK = WINDOW
N_BUCKETS = 32
MAX_DISTANCE = 128

SSD_HEADS = 16
SSD_HEAD_DIM = 64
SSD_WIDTH = SSD_HEADS * SSD_HEAD_DIM
SSD_GROUPS = 2
SSD_HPG = SSD_HEADS // SSD_GROUPS
D_STATE = 128
CONV_WIDTH = 4
CONV_DIM = SSD_WIDTH + 2 * SSD_GROUPS * D_STATE
SSD_CHUNK = 128

MIX_WIDTH = ATT_WIDTH + SSD_WIDTH
IN_SPLITS = (ATT_WIDTH, KV_WIDTH, KV_WIDTH, SSD_WIDTH, CONV_DIM, SSD_HEADS)
IN_COLS = sum(IN_SPLITS)

N_MEM = 256
MEM_HEADS = 4
MEM_HEAD_DIM = D_MODEL // MEM_HEADS

PEER_HEADS = 8
N_KEYS = 128
N_EXPERTS = N_KEYS * N_KEYS
PEER_TOPK = 16
PEER_QDIM = 256
PEER_HALF = PEER_QDIM // 2
PEER_BLOCK = 128

EPS = 1e-6

kernel_name = 'hymba_swa_ssd_peer_memory_step'


def rmsnorm(x, w):
    xf = x.astype(jnp.float32)
    y = xf * lax.rsqrt(jnp.mean(xf * xf, axis=-1, keepdims=True) + EPS)
    return (y * w.astype(jnp.float32)).astype(x.dtype)


def split_in(proj):
    offs = np.cumsum(IN_SPLITS)[:-1].tolist()
    return jnp.split(proj, offs, axis=-1)


def t5_bucket(dist):
    n = jnp.maximum(dist, 0)
    max_exact = N_BUCKETS // 2
    nf = jnp.maximum(n, 1).astype(jnp.float32)
    large = max_exact + (jnp.log(nf / max_exact) / math.log(MAX_DISTANCE / max_exact)
                         * (N_BUCKETS - max_exact)).astype(jnp.int32)
    large = jnp.minimum(large, N_BUCKETS - 1)
    return jnp.where(n < max_exact, n, large)


def rel_bias_logits(dist, table):
    lq, lk = dist.shape
    b = table[t5_bucket(dist)].astype(jnp.float32)
    return jnp.transpose(b, (2, 0, 1)).reshape(ATT_KV_HEADS, ATT_GROUP, lq, lk)


def sink_attention(q, k, v, bias, mask, sinks):
    s = jnp.einsum('...qkgd,...skd->...kgqs', q, k).astype(jnp.float32) * (ATT_HEAD_DIM ** -0.5) + bias
    s = jnp.where(mask, s, -jnp.inf)
    sink = jnp.broadcast_to(sinks.astype(jnp.float32).reshape(ATT_KV_HEADS, ATT_GROUP, 1, 1),
                            s.shape[:-1] + (1,))
    p = jax.nn.softmax(jnp.concatenate([s, sink], axis=-1), axis=-1)[..., :-1]
    return jnp.einsum('...kgqs,...skd->...qkgd', p.astype(v.dtype), v)


def causal_dwconv(xpad, w, b):
    y = lax.conv_general_dilated(xpad, w[:, None, :].astype(xpad.dtype), window_strides=(1,),
                                 padding='VALID', dimension_numbers=('NWC', 'WIO', 'NWC'),
                                 feature_group_count=CONV_DIM)
    return jax.nn.silu(y + b)


def ssd_inputs(xbc, dt_raw, dt_bias):
    b, t, _ = xbc.shape
    xs, bs, cs = jnp.split(xbc, [SSD_WIDTH, SSD_WIDTH + SSD_GROUPS * D_STATE], axis=-1)
    dt = jax.nn.softplus((dt_raw + dt_bias).astype(jnp.float32))
    return (xs.reshape(b, t, SSD_HEADS, SSD_HEAD_DIM), dt,
            bs.reshape(b, t, SSD_GROUPS, D_STATE), cs.reshape(b, t, SSD_GROUPS, D_STATE))


def ssd_chunked(x, dt, A, Bm, Cm):
    b, s, _, _ = x.shape
    c, L = s // SSD_CHUNK, SSD_CHUNK
    xdt = (x.astype(jnp.float32) * dt[..., None]).reshape(b, c, L, SSD_GROUPS, SSD_HPG, SSD_HEAD_DIM)
    a = (dt * A).reshape(b, c, L, SSD_GROUPS, SSD_HPG).transpose(0, 1, 3, 4, 2)
    cs = jnp.cumsum(a, axis=-1)
    Bc = Bm.astype(jnp.float32).reshape(b, c, L, SSD_GROUPS, D_STATE)
    Cc = Cm.astype(jnp.float32).reshape(b, c, L, SSD_GROUPS, D_STATE)
    causal = jnp.tril(jnp.ones((L, L), dtype=bool))
    ldec = jnp.exp(jnp.where(causal, cs[..., :, None] - cs[..., None, :], -jnp.inf))
    cb = jnp.einsum('bclgn,bcsgn->bcgls', Cc, Bc)
    y_diag = jnp.einsum('bcgjls,bcsgjp->bclgjp', cb[:, :, :, None] * ldec, xdt)
    decay_to_end = jnp.exp(cs[..., -1:] - cs)
    chunk_states = jnp.einsum('bclgn,bcgjl,bclgjp->bcgjpn', Bc, decay_to_end, xdt)
    chunk_decay = jnp.exp(cs[..., -1])

    def step(h, inp):
        st, dec = inp
        return dec[..., None, None] * h + st, h

    h0 = jnp.zeros((b, SSD_GROUPS, SSD_HPG, SSD_HEAD_DIM, D_STATE), jnp.float32)
    h_final, prev = lax.scan(step, h0, (jnp.moveaxis(chunk_states, 1, 0), jnp.moveaxis(chunk_decay, 1, 0)))
    prev = jnp.moveaxis(prev, 0, 1)
    y_off = jnp.einsum('bclgn,bcgjpn,bcgjl->bclgjp', Cc, prev, jnp.exp(cs))
    y = (y_diag + y_off).reshape(b, s, SSD_HEADS, SSD_HEAD_DIM)
    return y.astype(x.dtype), h_final.reshape(b, SSD_HEADS, SSD_HEAD_DIM, D_STATE)


def ssd_recurrent(x, dt, A, Bm, Cm, h0):
    b = x.shape[0]
    Ag = A.reshape(SSD_GROUPS, SSD_HPG)

    def step(h, inp):
        xt, dtt, bt, ct = inp
        xt = xt.astype(jnp.float32).reshape(b, SSD_GROUPS, SSD_HPG, SSD_HEAD_DIM)
        dtt = dtt.reshape(b, SSD_GROUPS, SSD_HPG)
        h = jnp.exp(dtt * Ag)[..., None, None] * h + jnp.einsum('bgjp,bgn->bgjpn', xt * dtt[..., None],
                                                                  bt.astype(jnp.float32))
        y = jnp.einsum('bgjpn,bgn->bgjp', h, ct.astype(jnp.float32))
        return h, y.reshape(b, SSD_HEADS, SSD_HEAD_DIM)

    hs = h0.astype(jnp.float32).reshape(b, SSD_GROUPS, SSD_HPG, SSD_HEAD_DIM, D_STATE)
    h_final, ys = lax.scan(step, hs, (jnp.moveaxis(x, 1, 0), jnp.moveaxis(dt, 1, 0),
                                      jnp.moveaxis(Bm, 1, 0), jnp.moveaxis(Cm, 1, 0)))
    return jnp.moveaxis(ys, 0, 1).astype(x.dtype), h_final.reshape(b, SSD_HEADS, SSD_HEAD_DIM, D_STATE)


def merge_groups(o_att, y_ssd, z, lp):
    ya = rmsnorm(o_att, lp['attn_norm_w'])
    ys = rmsnorm(y_ssd * jax.nn.silu(z), lp['ssd_norm_w'])
    return jnp.concatenate([ya, ys], axis=-1) @ lp['w_out']


def memory_kv(mem, lp):
    b, m, _ = mem.shape
    mn = rmsnorm(mem, lp['mem_norm_w'])
    k = (mn @ lp['w_mk']).reshape(b, m, MEM_HEADS, MEM_HEAD_DIM)
    v = (mn @ lp['w_mv']).reshape(b, m, MEM_HEADS, MEM_HEAD_DIM)
    return k, v


def memory_attention(xn, mk, mv, lp):
    b, t, _ = xn.shape
    q = (xn @ lp['w_mq']).reshape(b, t, MEM_HEADS, MEM_HEAD_DIM)
    s = jnp.einsum('bthd,bmhd->bhtm', q, mk).astype(jnp.float32) * (MEM_HEAD_DIM ** -0.5)
    p = jax.nn.softmax(s, axis=-1).astype(mv.dtype)
    o = jnp.einsum('bhtm,bmhd->bthd', p, mv).reshape(b, t, D_MODEL)
    return o @ lp['w_mo']


def peer_ffn(xn, lp):
    b, t, _ = xn.shape
    xf = xn.reshape(b * t, D_MODEL)
    n = b * t
    q = (xf @ lp['peer_wq']).reshape(n, PEER_HEADS, 2, PEER_HALF)
    sub = jnp.einsum('nhpd,hpkd->nhpk', q, lp['peer_keys']).astype(jnp.float32)
    top_s, top_i = lax.top_k(sub, PEER_TOPK)
    cand_s = (top_s[:, :, 0, :, None] + top_s[:, :, 1, None, :]).reshape(n, PEER_HEADS, PEER_TOPK * PEER_TOPK)
    cand_i = (top_i[:, :, 0, :, None] * N_KEYS + top_i[:, :, 1, None, :]).reshape(n, PEER_HEADS, PEER_TOPK * PEER_TOPK)
    best_s, best_pos = lax.top_k(cand_s, PEER_TOPK)
    idx = jnp.take_along_axis(cand_i, best_pos, axis=-1).reshape(n, PEER_HEADS * PEER_TOPK)
    gate = jax.nn.softmax(best_s, axis=-1).reshape(n, PEER_HEADS * PEER_TOPK).astype(xn.dtype)
    n_pad = -(-n // PEER_BLOCK) * PEER_BLOCK
    pad = n_pad - n
    xp = jnp.pad(xf, ((0, pad), (0, 0))).reshape(-1, PEER_BLOCK, D_MODEL)
    ip = jnp.pad(idx, ((0, pad), (0, 0))).reshape(-1, PEER_BLOCK, PEER_HEADS * PEER_TOPK)
    gp = jnp.pad(gate, ((0, pad), (0, 0))).reshape(-1, PEER_BLOCK, PEER_HEADS * PEER_TOPK)

    def block(args):
        xb, ib, gb = args
        act = jax.nn.gelu(jnp.einsum('ted,td->te', lp['peer_u'][ib], xb), approximate=False)
        return jnp.einsum('te,ted->td', gb * act, lp['peer_v'][ib])

    out = lax.map(block, (xp, ip, gp)).reshape(n_pad, D_MODEL)[:n]
    return out.reshape(b, t, D_MODEL)


def layer_tail(x, mk, mv, lp):
    x = x + memory_attention(rmsnorm(x, lp['norm_mem_w']), mk, mv, lp)
    return x + peer_ffn(rmsnorm(x, lp['norm_ffn_w']), lp)


def prompt_layer(x, mem, lp, rel_bias):
    b, s, _ = x.shape
    h = rmsnorm(x, lp['norm_mix_w'])
    q, k, v, z, xbc, dt_raw = split_in(h @ lp['w_in'])
    nb = s // ATT_BLOCK
    qb = q.reshape(b, nb, ATT_BLOCK, ATT_KV_HEADS, ATT_GROUP, ATT_HEAD_DIM)
    kb = k.reshape(b, nb, ATT_BLOCK, ATT_KV_HEADS, ATT_HEAD_DIM)
    vb = v.reshape(b, nb, ATT_BLOCK, ATT_KV_HEADS, ATT_HEAD_DIM)
    kk = jnp.concatenate([jnp.concatenate([jnp.zeros_like(kb[:, :1]), kb[:, :-1]], axis=1), kb], axis=2)
    vv = jnp.concatenate([jnp.concatenate([jnp.zeros_like(vb[:, :1]), vb[:, :-1]], axis=1), vb], axis=2)
    kj = jnp.arange(2 * ATT_BLOCK)[None, :]
    dist = (jnp.arange(ATT_BLOCK)[:, None] + ATT_BLOCK) - kj
    in_window = (dist >= 0) & (dist <= WINDOW)
    key_exists = (jnp.arange(nb)[:, None] > 0) | (kj >= ATT_BLOCK)
    mask = in_window[None, None, None] & key_exists[:, None, None, None, :]
    o_att = sink_attention(qb, kk, vv, rel_bias_logits(dist, rel_bias), mask,
                           lp['attn_sinks']).reshape(b, s, ATT_WIDTH)
    swa_k = k.reshape(b, s, ATT_KV_HEADS, ATT_HEAD_DIM)[:, s - WINDOW:]
    swa_v = v.reshape(b, s, ATT_KV_HEADS, ATT_HEAD_DIM)[:, s - WINDOW:]
    xpad = jnp.concatenate([jnp.zeros((b, CONV_WIDTH - 1, CONV_DIM), xbc.dtype), xbc], axis=1)
    conv_state = xpad[:, -(CONV_WIDTH - 1):]
    xs, dt, bs, cs = ssd_inputs(causal_dwconv(xpad, lp['conv_w'], lp['conv_b']), dt_raw, lp['dt_bias'])
    A = -jnp.exp(lp['a_log'].astype(jnp.float32))
    y, ssm_state = ssd_chunked(xs, dt, A, bs, cs)
    y = (y + lp['d_skip'][:, None] * xs).reshape(b, s, SSD_WIDTH)
    x = x + merge_groups(o_att, y, z, lp)
    mk, mv = memory_kv(mem, lp)
    x = layer_tail(x, mk, mv, lp)
    return x, (swa_k, swa_v, ssm_state.astype(x.dtype), conv_state, mk, mv)


def sample_layer(x, cache_k, cache_v, ssm, conv, mk, mv, lp, rel_bias):
    b, t, _ = x.shape
    h = rmsnorm(x, lp['norm_mix_w'])
    q, k, v, z, xbc, dt_raw = split_in(h @ lp['w_in'])
    wc = cache_k.shape[1]
    qh = q.reshape(b, t, ATT_KV_HEADS, ATT_GROUP, ATT_HEAD_DIM)
    kk = jnp.concatenate([cache_k, k.reshape(b, t, ATT_KV_HEADS, ATT_HEAD_DIM)], axis=1)
    vv = jnp.concatenate([cache_v, v.reshape(b, t, ATT_KV_HEADS, ATT_HEAD_DIM)], axis=1)
    dist = (jnp.arange(t)[:, None] + wc) - jnp.arange(wc + t)[None, :]
    mask = (dist >= 0) & (dist <= WINDOW)
    o_att = sink_attention(qh, kk, vv, rel_bias_logits(dist, rel_bias), mask,
                           lp['attn_sinks']).reshape(b, t, ATT_WIDTH)
    new_k = kk[:, -wc:]
    new_v = vv[:, -wc:]
    xpad = jnp.concatenate([conv, xbc], axis=1)
    new_conv = xpad[:, -(CONV_WIDTH - 1):]
    xs, dt, bs, cs = ssd_inputs(causal_dwconv(xpad, lp['conv_w'], lp['conv_b']), dt_raw, lp['dt_bias'])
    A = -jnp.exp(lp['a_log'].astype(jnp.float32))
    y, new_ssm = ssd_recurrent(xs, dt, A, bs, cs, ssm)
    y = (y + lp['d_skip'][:, None] * xs).reshape(b, t, SSD_WIDTH)
    x = x + merge_groups(o_att, y, z, lp)
    x = layer_tail(x, mk, mv, lp)
    return x, (new_k, new_v, new_ssm.astype(ssm.dtype), new_conv)


def setup_inputs(seed: int = 0) -> dict:
    key = jax.random.key(seed)
    ks = iter(jax.random.split(key, 40))
    f32 = jnp.float32

    def nrm(shape, scale):
        return jax.random.normal(next(ks), shape, f32) * scale

    def gain(shape):
        return 1.0 + 0.02 * jax.random.normal(next(ks), shape, f32)

    win = min(WINDOW, PAST_LEN)
    x_prompt = nrm((BATCH, SEQ, D_MODEL), 1.0)
    x_sample = nrm((DEC_BATCH, DEC_SEQ, D_MODEL), 1.0)
    mem_prompt = nrm((BATCH, N_MEM, D_MODEL), 1.0)
    cache_swa_k = nrm((DEPTH, DEC_BATCH, win, ATT_KV_HEADS, ATT_HEAD_DIM), 1.0)
    cache_swa_v = nrm((DEPTH, DEC_BATCH, win, ATT_KV_HEADS, ATT_HEAD_DIM), 1.0)
    state_ssm = nrm((DEPTH, DEC_BATCH, SSD_HEADS, SSD_HEAD_DIM, D_STATE), 0.5)
    state_conv = nrm((DEPTH, DEC_BATCH, CONV_WIDTH - 1, CONV_DIM), 1.0)
    cache_mem_k = nrm((DEPTH, DEC_BATCH, N_MEM, MEM_HEADS, MEM_HEAD_DIM), 1.0)
    cache_mem_v = nrm((DEPTH, DEC_BATCH, N_MEM, MEM_HEADS, MEM_HEAD_DIM), 1.0)
    norm_mix_w = gain((DEPTH, D_MODEL))
    w_in = nrm((DEPTH, D_MODEL, IN_COLS), D_MODEL ** -0.5)
    attn_sinks = nrm((DEPTH, ATT_HEADS), 1.0)
    rel_bias = nrm((N_BUCKETS, ATT_HEADS), 0.5)
    attn_norm_w = gain((DEPTH, ATT_WIDTH))
    conv_w = nrm((DEPTH, CONV_WIDTH, CONV_DIM), CONV_WIDTH ** -0.5)
    conv_b = nrm((DEPTH, CONV_DIM), 0.02)
    dt0 = jnp.exp(jax.random.uniform(next(ks), (DEPTH, SSD_HEADS), f32, math.log(1e-3), math.log(1e-1)))
    dt_bias = dt0 + jnp.log(-jnp.expm1(-dt0))
    a_log = jnp.log(jax.random.uniform(next(ks), (DEPTH, SSD_HEADS), f32, 1.0, 16.0))
    d_skip = 1.0 + nrm((DEPTH, SSD_HEADS), 0.1)
    ssd_norm_w = gain((DEPTH, SSD_WIDTH))
    w_out = nrm((DEPTH, MIX_WIDTH, D_MODEL), MIX_WIDTH ** -0.5)
    norm_mem_w = gain((DEPTH, D_MODEL))
    mem_norm_w = gain((DEPTH, D_MODEL))
    w_mq = nrm((DEPTH, D_MODEL, D_MODEL), D_MODEL ** -0.5)
    w_mk = nrm((DEPTH, D_MODEL, D_MODEL), D_MODEL ** -0.5)
    w_mv = nrm((DEPTH, D_MODEL, D_MODEL), D_MODEL ** -0.5)
    w_mo = nrm((DEPTH, D_MODEL, D_MODEL), D_MODEL ** -0.5)
    norm_ffn_w = gain((DEPTH, D_MODEL))
    peer_wq = nrm((DEPTH, D_MODEL, PEER_HEADS * PEER_QDIM), D_MODEL ** -0.5)
    peer_keys = nrm((DEPTH, PEER_HEADS, 2, N_KEYS, PEER_HALF), PEER_HALF ** -0.5)
    peer_u = nrm((DEPTH, N_EXPERTS, D_MODEL), D_MODEL ** -0.5)
    peer_v = nrm((DEPTH, N_EXPERTS, D_MODEL), PEER_HEADS ** -0.5)
    final_norm_w = gain((D_MODEL,))
    return {'x_prompt': x_prompt, 'x_sample': x_sample, 'mem_prompt': mem_prompt,
            'cache_swa_k': cache_swa_k, 'cache_swa_v': cache_swa_v, 'state_ssm': state_ssm,
            'state_conv': state_conv, 'cache_mem_k': cache_mem_k, 'cache_mem_v': cache_mem_v,
            'norm_mix_w': norm_mix_w, 'w_in': w_in, 'attn_sinks': attn_sinks, 'rel_bias': rel_bias,
            'attn_norm_w': attn_norm_w, 'conv_w': conv_w, 'conv_b': conv_b, 'dt_bias': dt_bias,
            'a_log': a_log, 'd_skip': d_skip, 'ssd_norm_w': ssd_norm_w, 'w_out': w_out,
            'norm_mem_w': norm_mem_w, 'mem_norm_w': mem_norm_w, 'w_mq': w_mq, 'w_mk': w_mk,
            'w_mv': w_mv, 'w_mo': w_mo, 'norm_ffn_w': norm_ffn_w, 'peer_wq': peer_wq,
            'peer_keys': peer_keys, 'peer_u': peer_u, 'peer_v': peer_v, 'final_norm_w': final_norm_w}


def reference(x_prompt, x_sample, mem_prompt, cache_swa_k, cache_swa_v, state_ssm, state_conv,
              cache_mem_k, cache_mem_v, norm_mix_w, w_in, attn_sinks, rel_bias, attn_norm_w, conv_w,
              conv_b, dt_bias, a_log, d_skip, ssd_norm_w, w_out, norm_mem_w, mem_norm_w, w_mq, w_mk,
              w_mv, w_mo, norm_ffn_w, peer_wq, peer_keys, peer_u, peer_v, final_norm_w):
    yp, ys = x_prompt, x_sample
    p_states, s_states = [], []
    for l in range(DEPTH):
        lp = dict(norm_mix_w=norm_mix_w[l], w_in=w_in[l], attn_sinks=attn_sinks[l],
                  attn_norm_w=attn_norm_w[l], conv_w=conv_w[l], conv_b=conv_b[l], dt_bias=dt_bias[l],
                  a_log=a_log[l], d_skip=d_skip[l], ssd_norm_w=ssd_norm_w[l], w_out=w_out[l],
                  norm_mem_w=norm_mem_w[l], mem_norm_w=mem_norm_w[l], w_mq=w_mq[l], w_mk=w_mk[l],
                  w_mv=w_mv[l], w_mo=w_mo[l], norm_ffn_w=norm_ffn_w[l], peer_wq=peer_wq[l],
                  peer_keys=peer_keys[l], peer_u=peer_u[l], peer_v=peer_v[l])
        yp, ps = prompt_layer(yp, mem_prompt, lp, rel_bias)
        ys, ss = sample_layer(ys, cache_swa_k[l], cache_swa_v[l], state_ssm[l], state_conv[l],
                              cache_mem_k[l], cache_mem_v[l], lp, rel_bias)
        p_states.append(ps)
        s_states.append(ss)
    y_prompt = rmsnorm(yp, final_norm_w)
    y_sample = rmsnorm(ys, final_norm_w)

    def stack(states, i):
        return jnp.stack([st[i] for st in states])

    p_swa_k = stack(p_states, 0)
    p_swa_v = stack(p_states, 1)
    p_ssm = stack(p_states, 2)
    p_conv = stack(p_states, 3)
    p_mem_k = stack(p_states, 4)
    p_mem_v = stack(p_states, 5)
    s_swa_k = stack(s_states, 0)
    s_swa_v = stack(s_states, 1)
    s_ssm = stack(s_states, 2)
    s_conv = stack(s_states, 3)
    return (y_prompt, y_sample, p_swa_k, p_swa_v, p_ssm, p_conv, p_mem_k, p_mem_v, s_swa_k, s_swa_v, s_ssm, s_conv)
```

```python
import functools
import math

import numpy as np
import jax
import jax.numpy as jnp
from jax import lax
from jax.experimental import pallas as pl
from jax.experimental.pallas import tpu as pltpu

F32 = jnp.float32
BF16 = jnp.bfloat16
EPS = 1e-6

D_MODEL = 1024
ATT_HEADS = 16
ATT_KV_HEADS = 4
ATT_HEAD_DIM = 64
WINDOW = 128
N_BUCKETS = 32
MAX_DISTANCE = 128
SSD_HEADS = 16
SSD_HEAD_DIM = 64
SSD_WIDTH = SSD_HEADS * SSD_HEAD_DIM
SSD_GROUPS = 2
D_STATE = 128
CONV_WIDTH = 4
CONV_DIM = SSD_WIDTH + 2 * SSD_GROUPS * D_STATE
CHUNK = 128
N_MEM = 256
MEM_HEADS = 4
MEM_HEAD_DIM = D_MODEL // MEM_HEADS
PEER_HEADS = 8
N_KEYS = 128
N_EXPERTS = N_KEYS * N_KEYS
PEER_TOPK = 16
PEER_HALF = 128

LANES = 128
HEAD_PAD = 128
GROUP_W = SSD_WIDTH // SSD_GROUPS
VMEM_LIMIT = 56 * 1024 * 1024


def _params(sem):
    return pltpu.CompilerParams(dimension_semantics=sem, vmem_limit_bytes=VMEM_LIMIT)


def _dot(a, b):
    return jnp.dot(a, b, preferred_element_type=F32)


def _dot_nt(a, b):
    return lax.dot_general(a, b, (((1,), (1,)), ((), ())), preferred_element_type=F32)


def _rms(x, w):
    var = jnp.mean(x * x, axis=-1, keepdims=True)
    return x * lax.rsqrt(var + EPS) * w


def _sigmoid(x):
    return 1.0 / (1.0 + jnp.exp(-x))


def _softplus(x):
    return jnp.maximum(x, 0.0) + jnp.log1p(jnp.exp(-jnp.abs(x)))


def _split3(v):
    v1 = v.astype(BF16)
    r1 = v - v1.astype(F32)
    v2 = r1.astype(BF16)
    r2 = r1 - v2.astype(F32)
    return v1, v2, r2.astype(BF16)


def _exact_dot(v, m01):
    v1, v2, v3 = _split3(v)
    return _dot(v1, m01) + _dot(v2, m01) + _dot(v3, m01)


def _exact_dot_left(m01, v):
    v1, v2, v3 = _split3(v)
    return _dot(m01, v1) + _dot(m01, v2) + _dot(m01, v3)


def _norm_matmul_kernel(x_ref, nw_ref, w_ref, *out_refs, splits, emit_h):
    h = _rms(x_ref[...], nw_ref[...]).astype(BF16)
    refs = list(out_refs)
    if emit_h:
        refs.pop(0)[...] = h
    off = 0
    for o_ref, n in zip(refs, splits):
        o_ref[...] = _dot(h, w_ref[:, off:off + n])
        off += n


def norm_matmul(x, nw, w_bf16, splits, tm, emit_h=False):
    t, d = x.shape
    n = w_bf16.shape[1]
    assert sum(splits) == n and t % tm == 0
    out_shape = [jax.ShapeDtypeStruct((t, s), F32) for s in splits]
    out_specs = [pl.BlockSpec((tm, s), lambda i: (i, 0)) for s in splits]
    if emit_h:
        out_shape.insert(0, jax.ShapeDtypeStruct((t, d), BF16))
        out_specs.insert(0, pl.BlockSpec((tm, d), lambda i: (i, 0)))
    return pl.pallas_call(
        functools.partial(_norm_matmul_kernel, splits=tuple(splits), emit_h=emit_h),
        grid=(t // tm,),
        in_specs=[pl.BlockSpec((tm, d), lambda i: (i, 0)),
                  pl.BlockSpec((1, d), lambda i: (0, 0)),
                  pl.BlockSpec((d, n), lambda i: (0, 0))],
        out_specs=out_specs,
        out_shape=out_shape,
        compiler_params=_params(("parallel",)),
        name="norm_matmul",
    )(x, nw.reshape(1, d), w_bf16)


def _bias_kernel(table_ref, bucket_ref, out_ref):
    h = pl.program_id(0)
    bk = bucket_ref[...]
    acc = jnp.zeros(bk.shape, F32)
    for b in range(N_BUCKETS):
        acc = jnp.where(bk == b, table_ref[b, h], acc)
    out_ref[0] = acc


def rel_bias_rows(table, bucket):
    r, c = bucket.shape
    return pl.pallas_call(
        _bias_kernel,
        grid=(ATT_HEADS,),
        in_specs=[pl.BlockSpec(memory_space=pltpu.SMEM),
                  pl.BlockSpec((r, c), lambda h: (0, 0))],
        out_specs=pl.BlockSpec((1, r, c), lambda h: (h, 0, 0)),
        out_shape=jax.ShapeDtypeStruct((ATT_HEADS, r, c), F32),
        compiler_params=_params(("parallel",)),
        name="rel_bias",
    )(table, bucket)


def _t5_bucket(dist):
    n = jnp.maximum(dist, 0)
    max_exact = N_BUCKETS // 2
    nf = jnp.maximum(n, 1).astype(F32)
    large = max_exact + (jnp.log(nf / max_exact) / math.log(MAX_DISTANCE / max_exact)
                         * (N_BUCKETS - max_exact)).astype(jnp.int32)
    large = jnp.minimum(large, N_BUCKETS - 1)
    return jnp.where(n < max_exact, n, large)


def _half_pair(pair, rolled, which, lo):
    if which == 0:
        return jnp.where(lo, pair, 0.0), jnp.where(lo, 0.0, rolled)
    return jnp.where(lo, rolled, 0.0), jnp.where(lo, 0.0, pair)


def _swa_prompt_kernel(sink_ref, q_ref, kc_ref, kp_ref, vc_ref, vp_ref, bias_ref, o_ref):
    i = pl.program_id(1)
    blk = WINDOW
    row = lax.broadcasted_iota(jnp.int32, (blk, 2 * blk), 0)
    col = lax.broadcasted_iota(jnp.int32, (blk, 2 * blk), 1)
    dist = row + blk - col
    ok = (dist >= 0) & (dist <= WINDOW) & ((col >= blk) | (i > 0))
    lo = lax.broadcasted_iota(jnp.int32, (2 * blk, LANES), 1) < ATT_HEAD_DIM
    scale = ATT_HEAD_DIM ** -0.5
    for pair in range(ATT_KV_HEADS // 2):
        sl = slice(pair * LANES, (pair + 1) * LANES)
        kpair = jnp.concatenate([kp_ref[:, sl], kc_ref[:, sl]], axis=0)
        vpair = jnp.concatenate([vp_ref[:, sl], vc_ref[:, sl]], axis=0)
        kroll = pltpu.roll(kpair, ATT_HEAD_DIM, axis=1)
        vroll = pltpu.roll(vpair, ATT_HEAD_DIM, axis=1)
        for gi in range(2):
            g = 2 * pair + gi
            k_lo, k_hi = _half_pair(kpair, kroll, gi, lo)
            v_lo, v_hi = _half_pair(vpair, vroll, gi, lo)
            k_halves = (k_lo.astype(BF16), k_hi.astype(BF16))
            v_halves = (v_lo.astype(BF16), v_hi.astype(BF16))
            for qp in range(2):
                qsl = slice((2 * g + qp) * LANES, (2 * g + qp + 1) * LANES)
                qpair = q_ref[:, qsl].astype(BF16)
                acc = jnp.zeros((blk, LANES), F32)
                for par in range(2):
                    h = 4 * g + 2 * qp + par
                    s = _dot_nt(qpair, k_halves[par]) * scale + bias_ref[h]
                    s = jnp.where(ok, s, -jnp.inf)
                    sink = sink_ref[h]
                    m = jnp.maximum(jnp.max(s, axis=-1, keepdims=True), sink)
                    p = jnp.exp(s - m)
                    den = jnp.sum(p, axis=-1, keepdims=True) + jnp.exp(sink - m)
                    acc = acc + _dot((p / den).astype(BF16), v_halves[par])
                o_ref[:, qsl] = acc


def swa_prompt(q, k, v, bias, sinks, batch, seq):
    nb = seq // WINDOW
    cur = lambda b, i: (b * nb + i, 0)
    prev = lambda b, i: (b * nb + jnp.maximum(i - 1, 0), 0)
    kvw = ATT_KV_HEADS * ATT_HEAD_DIM
    return pl.pallas_call(
        _swa_prompt_kernel,
        grid=(batch, nb),
        in_specs=[pl.BlockSpec(memory_space=pltpu.SMEM),
                  pl.BlockSpec((WINDOW, D_MODEL), cur),
                  pl.BlockSpec((WINDOW, kvw), cur),
                  pl.BlockSpec((WINDOW, kvw), prev),
                  pl.BlockSpec((WINDOW, kvw), cur),
                  pl.BlockSpec((WINDOW, kvw), prev),
                  pl.BlockSpec((ATT_HEADS, WINDOW, 2 * WINDOW), lambda b, i: (0, 0, 0))],
        out_specs=pl.BlockSpec((WINDOW, D_MODEL), cur),
        out_shape=jax.ShapeDtypeStruct((batch * seq, D_MODEL), F32),
        compiler_params=_params(("parallel", "parallel")),
        name="swa_prompt",
    )(sinks, q, k, k, v, v, bias)


def _ssd_prompt_kernel(xbc_ref, dt_ref, cw_ref, cb_ref, dtb_ref, alog_ref, dskip_ref, e_ref,
                       y_ref, st_ref, state_scr, tail_scr):
    c = pl.program_id(1)
    last = pl.num_programs(1) - 1

    @pl.when(c == 0)
    def _():
        state_scr[...] = jnp.zeros(state_scr.shape, F32)
        tail_scr[...] = jnp.zeros(tail_scr.shape, F32)

    x = xbc_ref[...]
    tail = tail_scr[...]
    row8 = lax.broadcasted_iota(jnp.int32, tail.shape, 0)
    acc = x * cw_ref[CONV_WIDTH - 1:CONV_WIDTH, :]
    for s in range(1, CONV_WIDTH):
        xr = pltpu.roll(x, s, axis=0)
        tr = pltpu.roll(tail, s, axis=0)
        head = jnp.where(row8 < s, tr, xr[0:8])
        shifted = jnp.concatenate([head, xr[8:]], axis=0)
        acc = acc + shifted * cw_ref[CONV_WIDTH - 1 - s:CONV_WIDTH - s, :]
    tail_scr[...] = x[CHUNK - 8:CHUNK]
    u = acc + cb_ref[...]
    u = u * _sigmoid(u)
    xs = u[:, :SSD_WIDTH]
    bm = u[:, SSD_WIDTH:SSD_WIDTH + SSD_GROUPS * D_STATE]
    cm = u[:, SSD_WIDTH + SSD_GROUPS * D_STATE:]

    dtv = _softplus(dt_ref[...] + dtb_ref[...])
    a = dtv * (-jnp.exp(alog_ref[...]))
    r = lax.broadcasted_iota(jnp.int32, (CHUNK, CHUNK), 0)
    cc = lax.broadcasted_iota(jnp.int32, (CHUNK, CHUNK), 1)
    causal = r >= cc
    tril = jnp.where(causal, 1.0, 0.0).astype(BF16)
    cs = _exact_dot_left(tril, a)
    cs_t = cs.T
    dte = jnp.exp(cs[CHUNK - 1:CHUNK, :] - cs)
    ecs = jnp.exp(cs)
    e01 = e_ref[...]
    dt_x = _exact_dot(dtv, e01)
    dte_x = _exact_dot(dte, e01)
    ecs_x = _exact_dot(ecs, e01)
    xdt = xs * dt_x
    xdec = (xdt * dte_x).astype(BF16)
    xdt_b = xdt.astype(BF16)
    lo = lax.broadcasted_iota(jnp.int32, (CHUNK, LANES), 1) < SSD_HEAD_DIM
    dsk = dskip_ref[...]
    for g in range(SSD_GROUPS):
        gs = slice(g * GROUP_W, (g + 1) * GROUP_W)
        bg = bm[:, g * D_STATE:(g + 1) * D_STATE]
        cg = cm[:, g * D_STATE:(g + 1) * D_STATE].astype(BF16)
        cb = _dot_nt(cg, bg.astype(BF16))
        bgt = bg.T.astype(BF16)
        prev = state_scr[g]
        y_off = _dot(cg, prev.astype(BF16)) * ecs_x[:, gs]
        state_scr[g] = ecs_x[CHUNK - 1:CHUNK, gs] * prev + _dot(bgt, xdec[:, gs])
        for m in range(GROUP_W // LANES):
            ps = slice(g * GROUP_W + m * LANES, g * GROUP_W + (m + 1) * LANES)
            xp = xdt_b[:, ps]
            ydiag = jnp.zeros((CHUNK, LANES), F32)
            for par in range(2):
                j = g * (SSD_HEADS // SSD_GROUPS) + 2 * m + par
                diff = cs[:, j:j + 1] - cs_t[j:j + 1, :]
                ldec = jnp.exp(jnp.where(causal, diff, -jnp.inf))
                mm = (cb * ldec).astype(BF16)
                xpm = jnp.where(lo, xp, 0.0) if par == 0 else jnp.where(lo, 0.0, xp)
                ydiag = ydiag + _dot(mm, xpm.astype(BF16))
            y_ref[:, ps] = ydiag + y_off[:, m * LANES:(m + 1) * LANES] + dsk[:, ps] * xs[:, ps]

    @pl.when(c == last)
    def _():
        for g in range(SSD_GROUPS):
            st_ref[0, g * GROUP_W:(g + 1) * GROUP_W, :] = state_scr[g].T


def ssd_prompt(xbc, dt_raw, cw, cb, dtb, alog, dskip_x, e01, batch, seq):
    nc = seq // CHUNK
    cur = lambda b, c: (b * nc + c, 0)
    const = lambda b, c: (0, 0)
    return pl.pallas_call(
        _ssd_prompt_kernel,
        grid=(batch, nc),
        in_specs=[pl.BlockSpec((CHUNK, CONV_DIM), cur),
                  pl.BlockSpec((CHUNK, HEAD_PAD), cur),
                  pl.BlockSpec((CONV_WIDTH, CONV_DIM), const),
                  pl.BlockSpec((1, CONV_DIM), const),
                  pl.BlockSpec((1, HEAD_PAD), const),
                  pl.BlockSpec((1, HEAD_PAD), const),
                  pl.BlockSpec((1, SSD_WIDTH), const),
                  pl.BlockSpec((HEAD_PAD, SSD_WIDTH), const)],
        out_specs=[pl.BlockSpec((CHUNK, SSD_WIDTH), cur),
                   pl.BlockSpec((1, SSD_WIDTH, D_STATE), lambda b, c: (b, 0, 0))],
        out_shape=[jax.ShapeDtypeStruct((batch * seq, SSD_WIDTH), F32),
                   jax.ShapeDtypeStruct((batch, SSD_WIDTH, D_STATE), F32)],
        scratch_shapes=[pltpu.VMEM((SSD_GROUPS, D_STATE, GROUP_W), F32),
                        pltpu.VMEM((8, CONV_DIM), F32)],
        compiler_params=_params(("parallel", "arbitrary")),
        name="ssd_prompt",
    )(xbc, dt_raw, cw, cb, dtb, alog, dskip_x, e01)


def _swa_sample_kernel(qx_ref, ck_ref, cv_ref, kn_ref, vn_ref, bias_ref, sink_ref,
                       nk_ref, nv_ref, o_ref, *, bt):
    scale = ATT_HEAD_DIM ** -0.5
    kvw = ATT_KV_HEADS * ATT_HEAD_DIM
    rowi = lax.broadcasted_iota(jnp.int32, (WINDOW, kvw), 0)
    bias_c = bias_ref[:, 0:WINDOW]
    bias_n = bias_ref[:, WINDOW:WINDOW + 1]
    sink = sink_ref[:, 0:1]
    for bb in range(bt):
        kc = ck_ref[bb]
        vc = cv_ref[bb]
        kn = kn_ref[bb]
        vn = vn_ref[bb]
        qx = qx_ref[bb]
        s_c = _dot_nt(qx.astype(BF16), kc.astype(BF16)) * scale + bias_c
        s_n = jnp.sum(qx * kn, axis=1, keepdims=True) * scale + bias_n
        m = jnp.maximum(jnp.maximum(jnp.max(s_c, axis=1, keepdims=True), s_n), sink)
        p_c = jnp.exp(s_c - m)
        p_n = jnp.exp(s_n - m)
        den = jnp.sum(p_c, axis=1, keepdims=True) + p_n + jnp.exp(sink - m)
        o_ref[bb] = _dot((p_c / den).astype(BF16), vc.astype(BF16)) + (p_n / den) * vn
        nk_ref[bb] = jnp.where(rowi == WINDOW - 1, kn, pltpu.roll(kc, WINDOW - 1, axis=0))
        nv_ref[bb] = jnp.where(rowi == WINDOW - 1, vn, pltpu.roll(vc, WINDOW - 1, axis=0))


def swa_sample(qx, ck, cv, kn, vn, bias_s, sink_b, bt=8):
    nb = qx.shape[0]
    kvw = ATT_KV_HEADS * ATT_HEAD_DIM
    blk3 = lambda i: (i, 0, 0)
    return pl.pallas_call(
        functools.partial(_swa_sample_kernel, bt=bt),
        grid=(nb // bt,),
        in_specs=[pl.BlockSpec((bt, ATT_HEADS, kvw), blk3),
                  pl.BlockSpec((bt, WINDOW, kvw), blk3),
                  pl.BlockSpec((bt, WINDOW, kvw), blk3),
                  pl.BlockSpec((bt, 1, kvw), blk3),
                  pl.BlockSpec((bt, 1, kvw), blk3),
                  pl.BlockSpec((ATT_HEADS, 2 * WINDOW), lambda i: (0, 0)),
                  pl.BlockSpec((ATT_HEADS, LANES), lambda i: (0, 0))],
        out_specs=[pl.BlockSpec((bt, WINDOW, kvw), blk3),
                   pl.BlockSpec((bt, WINDOW, kvw), blk3),
                   pl.BlockSpec((bt, ATT_HEADS, kvw), blk3)],
        out_shape=[jax.ShapeDtypeStruct((nb, WINDOW, kvw), F32),
                   jax.ShapeDtypeStruct((nb, WINDOW, kvw), F32),
                   jax.ShapeDtypeStruct((nb, ATT_HEADS, kvw), F32)],
        compiler_params=_params(("parallel",)),
        name="swa_sample",
    )(qx, ck, cv, kn, vn, bias_s, sink_b)


def _ssd_sample_prep_kernel(xbc_ref, c0_ref, c1_ref, c2_ref, dt_ref, cw_ref, cb_ref, dtb_ref,
                            alog_ref, e_ref, dec_t_ref, xdt_t_ref, b_ref, c_ref, xs_ref):
    u = (c0_ref[...] * cw_ref[0:1, :] + c1_ref[...] * cw_ref[1:2, :]
         + c2_ref[...] * cw_ref[2:3, :] + xbc_ref[...] * cw_ref[3:4, :]) + cb_ref[...]
    u = u * _sigmoid(u)
    xs = u[:, :SSD_WIDTH]
    b_ref[...] = u[:, SSD_WIDTH:SSD_WIDTH + SSD_GROUPS * D_STATE]
    c_ref[...] = u[:, SSD_WIDTH + SSD_GROUPS * D_STATE:]
    xs_ref[...] = xs
    dtv = _softplus(dt_ref[...] + dtb_ref[...])
    dec = jnp.exp(dtv * (-jnp.exp(alog_ref[...])))
    e01 = e_ref[...]
    dec_t_ref[...] = _exact_dot(dec, e01).T
    xdt_t_ref[...] = (xs * _exact_dot(dtv, e01)).T


def ssd_sample_prep(xbc, c0, c1, c2, dt_raw, cw, cb, dtb, alog, e01):
    nb = xbc.shape[0]
    args = (xbc, c0, c1, c2, dt_raw, cw, cb, dtb, alog, e01)
    full = lambda a: pl.BlockSpec(a.shape, lambda i: (0,) * a.ndim)
    out_shape = [jax.ShapeDtypeStruct((SSD_WIDTH, nb), F32),
                 jax.ShapeDtypeStruct((SSD_WIDTH, nb), F32),
                 jax.ShapeDtypeStruct((nb, SSD_GROUPS * D_STATE), F32),
                 jax.ShapeDtypeStruct((nb, SSD_GROUPS * D_STATE), F32),
                 jax.ShapeDtypeStruct((nb, SSD_WIDTH), F32)]
    return pl.pallas_call(
        _ssd_sample_prep_kernel,
        grid=(1,),
        in_specs=[full(a) for a in args],
        out_specs=[full(s) for s in out_shape],
        out_shape=out_shape,
        compiler_params=_params(("arbitrary",)),
        name="ssd_sample_prep",
    )(*args)


def _ssd_sample_step_kernel(st_ref, dec_t_ref, xdt_t_ref, b_ref, c_ref, xs_ref, dskip_ref,
                            ns_ref, y_ref, yt_scr):
    b = pl.program_id(0)
    nb = dec_t_ref.shape[1]

    @pl.when(b == 0)
    def _():
        yt_scr[...] = jnp.zeros(yt_scr.shape, F32)

    pick = jnp.where(lax.broadcasted_iota(jnp.int32, (nb, D_STATE), 0) == b,
                     1.0, 0.0).astype(BF16)
    dcol = _exact_dot(dec_t_ref[...], pick)
    xcol = _exact_dot(xdt_t_ref[...], pick)
    brow = b_ref[pl.ds(b, 1), :]
    crow = c_ref[pl.ds(b, 1), :]
    lane = lax.broadcasted_iota(jnp.int32, (GROUP_W, nb), 1)
    for g in range(SSD_GROUPS):
        gs = slice(g * GROUP_W, (g + 1) * GROUP_W)
        ns = slice(g * D_STATE, (g + 1) * D_STATE)
        hn = dcol[gs] * st_ref[0, gs, :] + xcol[gs] * brow[:, ns]
        ns_ref[0, gs, :] = hn
        ycol = jnp.sum(hn * crow[:, ns], axis=1, keepdims=True)
        yt_scr[gs, :] = jnp.where(lane == b, ycol, yt_scr[gs, :])

    @pl.when(b == nb - 1)
    def _():
        y_ref[...] = yt_scr[...].T + dskip_ref[...] * xs_ref[...]


def ssd_sample_step(state, dec_t, xdt_t, bmat, cmat, xs, dskip_x):
    nb = state.shape[0]
    assert nb == D_STATE
    const2 = lambda i: (0, 0)
    return pl.pallas_call(
        _ssd_sample_step_kernel,
        grid=(nb,),
        in_specs=[pl.BlockSpec((1, SSD_WIDTH, D_STATE), lambda i: (i, 0, 0)),
                  pl.BlockSpec((SSD_WIDTH, nb), const2),
                  pl.BlockSpec((SSD_WIDTH, nb), const2),
                  pl.BlockSpec((nb, SSD_GROUPS * D_STATE), const2),
                  pl.BlockSpec((nb, SSD_GROUPS * D_STATE), const2),
                  pl.BlockSpec((nb, SSD_WIDTH), const2),
                  pl.BlockSpec((1, SSD_WIDTH), const2)],
        out_specs=[pl.BlockSpec((1, SSD_WIDTH, D_STATE), lambda i: (i, 0, 0)),
                   pl.BlockSpec((nb, SSD_WIDTH), const2)],
        out_shape=[jax.ShapeDtypeStruct((nb, SSD_WIDTH, D_STATE), F32),
                   jax.ShapeDtypeStruct((nb, SSD_WIDTH), F32)],
        scratch_shapes=[pltpu.VMEM((SSD_WIDTH, nb), F32)],
        compiler_params=_params(("arbitrary",)),
        name="ssd_sample_step",
    )(state, dec_t, xdt_t, bmat, cmat, xs, dskip_x)


def _merge_kernel(x_ref, o_ref, y_ref, z_ref, anw_ref, snw_ref, w1_ref, w2_ref, out_ref):
    ya = _rms(o_ref[...], anw_ref[...]).astype(BF16)
    z = z_ref[...]
    ys = _rms(y_ref[...] * (z * _sigmoid(z)), snw_ref[...]).astype(BF16)
    out_ref[...] = x_ref[...] + _dot(ya, w1_ref[...]) + _dot(ys, w2_ref[...])


def merge(x, o_att, y_ssd, z, anw, snw, w1, w2, tm):
    t, d = x.shape
    row = pl.BlockSpec((tm, d), lambda i: (i, 0))
    vec = pl.BlockSpec((1, d), lambda i: (0, 0))
    mat = pl.BlockSpec((d, d), lambda i: (0, 0))
    return pl.pallas_call(
        _merge_kernel,
        grid=(t // tm,),
        in_specs=[row, row, row, row, vec, vec, mat, mat],
        out_specs=row,
        out_shape=jax.ShapeDtypeStruct((t, d), F32),
        compiler_params=_params(("parallel",)),
        name="merge",
    )(x, o_att, y_ssd, z, anw.reshape(1, d), snw.reshape(1, d), w1, w2)


def _mem_prompt_kernel(x_ref, nw_ref, wq_ref, mk_ref, mv_ref, wo_ref, out_ref):
    x = x_ref[...]
    q = _dot(_rms(x, nw_ref[...]).astype(BF16), wq_ref[...])
    scale = MEM_HEAD_DIM ** -0.5
    outs = []
    for h in range(MEM_HEADS):
        hs = slice(h * MEM_HEAD_DIM, (h + 1) * MEM_HEAD_DIM)
        s = _dot_nt(q[:, hs].astype(BF16), mk_ref[:, hs]) * scale
        p = jnp.exp(s - jnp.max(s, axis=-1, keepdims=True))
        p = p / jnp.sum(p, axis=-1, keepdims=True)
        outs.append(_dot(p.astype(BF16), mv_ref[:, hs]).astype(BF16))
    out_ref[...] = x + _dot(jnp.concatenate(outs, axis=1), wo_ref[...])


def mem_attn_prompt(x, nw, wq, mk, mv, wo, seq, tm):
    t, d = x.shape
    per = seq // tm
    row = pl.BlockSpec((tm, d), lambda i: (i, 0))
    mat = pl.BlockSpec((d, d), lambda i: (0, 0))
    mem = pl.BlockSpec((N_MEM, d), lambda i: (i // per, 0))
    return pl.pallas_call(
        _mem_prompt_kernel,
        grid=(t // tm,),
        in_specs=[row, pl.BlockSpec((1, d), lambda i: (0, 0)), mat, mem, mem, mat],
        out_specs=row,
        out_shape=jax.ShapeDtypeStruct((t, d), F32),
        compiler_params=_params(("parallel",)),
        name="mem_prompt",
    )(x, nw.reshape(1, d), wq, mk, mv, wo)


def _mem_sample_kernel(q_ref, k_ref, v_ref, o_ref):
    q = q_ref[0]
    rows = 8
    hid = lax.broadcasted_iota(jnp.int32, (rows, D_MODEL), 0)
    own = lax.broadcasted_iota(jnp.int32, (rows, D_MODEL), 1) // MEM_HEAD_DIM == hid
    qx = jnp.where(own, q, 0.0).astype(BF16)
    s = _dot_nt(qx, k_ref[0].astype(BF16)) * (MEM_HEAD_DIM ** -0.5)
    p = jnp.exp(s - jnp.max(s, axis=-1, keepdims=True))
    p = p / jnp.sum(p, axis=-1, keepdims=True)
    o = _dot(p.astype(BF16), v_ref[0].astype(BF16))
    o_ref[0] = jnp.sum(jnp.where(own, o, 0.0), axis=0, keepdims=True)


def mem_sample(q, ck, cv):
    nb, d = q.shape
    blk = pl.BlockSpec((1, N_MEM, d), lambda i: (i, 0, 0))
    vec = pl.BlockSpec((1, 1, d), lambda i: (i, 0, 0))
    return pl.pallas_call(
        _mem_sample_kernel,
        grid=(nb,),
        in_specs=[vec, blk, blk],
        out_specs=vec,
        out_shape=jax.ShapeDtypeStruct((nb, 1, d), F32),
        compiler_params=_params(("parallel",)),
        name="mem_sample",
    )(q.reshape(nb, 1, d), ck, cv).reshape(nb, d)


def _matmul_res_kernel(a_ref, w_ref, r_ref, out_ref):
    out_ref[...] = r_ref[...] + _dot(a_ref[...].astype(BF16), w_ref[...])


def matmul_res(a, w, res, tm):
    t, d = res.shape
    kdim = a.shape[1]
    return pl.pallas_call(
        _matmul_res_kernel,
        grid=(t // tm,),
        in_specs=[pl.BlockSpec((tm, kdim), lambda i: (i, 0)),
                  pl.BlockSpec((kdim, d), lambda i: (0, 0)),
                  pl.BlockSpec((tm, d), lambda i: (i, 0))],
        out_specs=pl.BlockSpec((tm, d), lambda i: (i, 0)),
        out_shape=jax.ShapeDtypeStruct((t, d), F32),
        compiler_params=_params(("parallel",)),
        name="matmul_res",
    )(a, w, res)


def _topk_rows(work, k):
    rows = work.shape[0]
    iota = lax.broadcasted_iota(jnp.int32, work.shape, 0)
    rank = jnp.full(work.shape, k, jnp.int32)
    vals = []
    for r in range(k):
        m = jnp.max(work, axis=0, keepdims=True)
        idx = jnp.min(jnp.where(work == m, iota, rows), axis=0, keepdims=True)
        hit = iota == idx
        rank = jnp.where(hit, r, rank)
        work = jnp.where(hit, -jnp.inf, work)
        vals.append(m)
    return jnp.concatenate(vals, axis=0), rank


def _peer_select_kernel(pq_ref, keys_ref, sel_ref):
    k = PEER_TOPK
    for h in range(PEER_HEADS):
        subs, tops, ranks = [], [], []
        for p in range(2):
            hp = 2 * h + p
            qhp = pq_ref[:, hp * PEER_HALF:(hp + 1) * PEER_HALF].astype(BF16)
            sub = _dot_nt(keys_ref[hp], qhp)
            top, rank = _topk_rows(sub, k)
            subs.append(sub)
            tops.append(top)
            ranks.append(rank)
        a, b = tops
        cand = jnp.concatenate([a[r:r + 1, :] + b for r in range(k)], axis=0)
        _, crank = _topk_rows(cand, k)
        chosen = crank < k
        cmax = a[0:1, :] + b[0:1, :]
        z = jnp.sum(jnp.where(chosen, jnp.exp(cand - cmax), 0.0), axis=0, keepdims=True)
        lim0 = jnp.zeros(ranks[0].shape, F32)
        for r in range(k):
            n_r = jnp.sum(jnp.where(chosen[r * k:(r + 1) * k], 1.0, 0.0), axis=0, keepdims=True)
            lim0 = jnp.where(ranks[0] == r, n_r, lim0)
        sel_ref[h, 0] = jnp.exp(subs[0] - a[0:1, :]) / z
        sel_ref[h, 1] = lim0
        sel_ref[h, 2] = jnp.exp(subs[1] - b[0:1, :])
        sel_ref[h, 3] = ranks[1].astype(F32)


def peer_select(pq, keys_bf16, tm):
    t = pq.shape[0]
    return pl.pallas_call(
        _peer_select_kernel,
        grid=(t // tm,),
        in_specs=[pl.BlockSpec((tm, pq.shape[1]), lambda i: (i, 0)),
                  pl.BlockSpec(keys_bf16.shape, lambda i: (0, 0, 0))],
        out_specs=pl.BlockSpec((PEER_HEADS, 4, N_KEYS, tm), lambda i: (0, 0, 0, i)),
        out_shape=jax.ShapeDtypeStruct((PEER_HEADS, 4, N_KEYS, t), F32),
        compiler_params=_params(("parallel",)),
        name="peer_select",
    )(pq, keys_bf16)


def _gelu(x):
    return 0.5 * x * (1.0 + lax.erf(x * np.float32(math.sqrt(0.5))))


def _peer_dense_kernel(xn_ref, x_ref, u_ref, vt_ref, sel_ref, fnw_ref, y_ref,
                       acc_scr, act_scr, w_scr, *, tm, ib):
    e = pl.program_id(1)
    ne = pl.num_programs(1)

    @pl.when(e == 0)
    def _():
        acc_scr[...] = jnp.zeros(acc_scr.shape, F32)

    act_scr[...] = _dot_nt(u_ref[...], xn_ref[...])

    i0 = pl.ds(pl.multiple_of(e * ib, ib), ib)
    for lt in range(tm // LANES):
        ls = slice(lt * LANES, (lt + 1) * LANES)
        e0s = [sel_ref[h, 0, i0, ls] for h in range(PEER_HEADS)]
        l0s = [sel_ref[h, 1, i0, ls] for h in range(PEER_HEADS)]
        for ii in range(ib):
            rs = slice(ii * N_KEYS, (ii + 1) * N_KEYS)
            g = jnp.zeros((N_KEYS, LANES), F32)
            for h in range(PEER_HEADS):
                keep = sel_ref[h, 3, :, ls] < l0s[h][ii:ii + 1, :]
                g = g + jnp.where(keep, sel_ref[h, 2, :, ls], 0.0) * e0s[h][ii:ii + 1, :]
            w_scr[rs, ls] = (g * _gelu(act_scr[rs, ls])).astype(BF16)
    acc_scr[...] += _dot(vt_ref[...], w_scr[...])

    @pl.when(e == ne - 1)
    def _():
        y_ref[...] = _rms(x_ref[...] + acc_scr[...].T, fnw_ref[...])


def peer_dense(xn, x, u_bf16, vt_bf16, sel, fnw, tm, ib=8):
    t, d = x.shape
    eb = ib * N_KEYS
    return pl.pallas_call(
        functools.partial(_peer_dense_kernel, tm=tm, ib=ib),
        grid=(t // tm, N_EXPERTS // eb),
        in_specs=[pl.BlockSpec((tm, d), lambda i, e: (i, 0)),
                  pl.BlockSpec((tm, d), lambda i, e: (i, 0)),
                  pl.BlockSpec((eb, d), lambda i, e: (e, 0)),
                  pl.BlockSpec((d, eb), lambda i, e: (0, e)),
                  pl.BlockSpec((PEER_HEADS, 4, N_KEYS, tm), lambda i, e: (0, 0, 0, i)),
                  pl.BlockSpec((1, d), lambda i, e: (0, 0))],
        out_specs=pl.BlockSpec((tm, d), lambda i, e: (i, 0)),
        out_shape=jax.ShapeDtypeStruct((t, d), F32),
        scratch_shapes=[pltpu.VMEM((d, tm), F32),
                        pltpu.VMEM((eb, tm), F32),
                        pltpu.VMEM((eb, tm), BF16)],
        compiler_params=_params(("parallel", "arbitrary")),
        name="peer_dense",
    )(xn, x, u_bf16, vt_bf16, sel, fnw.reshape(1, d))


def _tail(x, mk, mv, prm, seq, tm, mem_is_shared, tm_peer):
    if mem_is_shared:
        x = mem_attn_prompt(x, prm["norm_mem_w"], prm["w_mq"], mk, mv, prm["w_mo"], seq, tm)
    else:
        (q,) = norm_matmul(x, prm["norm_mem_w"], prm["w_mq"], (D_MODEL,), tm)
        x = matmul_res(mem_sample(q, mk, mv), prm["w_mo"], x, tm)
    xn, pq = norm_matmul(x, prm["norm_ffn_w"], prm["peer_wq"], (prm["peer_wq"].shape[1],), tm,
                         emit_h=True)
    sel = peer_select(pq, prm["peer_keys"], min(tm, 256))
    return peer_dense(xn, x, prm["peer_u"], prm["peer_vt"], sel, prm["final_norm_w"], tm_peer)


def kernel(x_prompt, x_sample, mem_prompt, cache_swa_k, cache_swa_v, state_ssm, state_conv, cache_mem_k, cache_mem_v, norm_mix_w, w_in, attn_sinks, rel_bias, attn_norm_w, conv_w, conv_b, dt_bias, a_log, d_skip, ssd_norm_w, w_out, norm_mem_w, mem_norm_w, w_mq, w_mk, w_mv, w_mo, norm_ffn_w, peer_wq, peer_keys, peer_u, peer_v, final_norm_w):
    assert w_in.shape[0] == 1, "single-layer step"
    batch, seq, d = x_prompt.shape
    nb = x_sample.shape[0]
    kvw = ATT_KV_HEADS * ATT_HEAD_DIM
    tm = 256

    n_main = D_MODEL + 2 * kvw + SSD_WIDTH + CONV_DIM
    w_in_p = jnp.concatenate(
        [w_in[0], jnp.zeros((d, HEAD_PAD - SSD_HEADS), F32)], axis=1).astype(BF16)
    in_splits = (D_MODEL, kvw, kvw, SSD_WIDTH, CONV_DIM, HEAD_PAD)
    assert n_main + SSD_HEADS == w_in.shape[2]
    pad_h = lambda v: jnp.pad(v.reshape(1, SSD_HEADS), ((0, 0), (0, HEAD_PAD - SSD_HEADS)))
    dtb, alog = pad_h(dt_bias[0]), pad_h(a_log[0])
    dskip_x = jnp.repeat(d_skip[0], SSD_HEAD_DIM).reshape(1, SSD_WIDTH)
    e01 = (jnp.arange(HEAD_PAD)[:, None] == jnp.arange(SSD_WIDTH)[None, :] // SSD_HEAD_DIM
           ).astype(BF16)
    cw, cb = conv_w[0], conv_b[0].reshape(1, CONV_DIM)
    w_out1 = w_out[0, :D_MODEL].astype(BF16)
    w_out2 = w_out[0, D_MODEL:].astype(BF16)
    prm = dict(norm_mem_w=norm_mem_w[0], w_mq=w_mq[0].astype(BF16), w_mo=w_mo[0].astype(BF16),
               norm_ffn_w=norm_ffn_w[0], peer_wq=peer_wq[0].astype(BF16),
               peer_keys=peer_keys[0].reshape(2 * PEER_HEADS, N_KEYS, PEER_HALF).astype(BF16),
               peer_u=peer_u[0].astype(BF16), peer_vt=peer_v[0].astype(BF16).T,
               final_norm_w=final_norm_w)

    qi = jnp.arange(WINDOW)[:, None] + WINDOW
    bias_p = rel_bias_rows(rel_bias, _t5_bucket(qi - jnp.arange(2 * WINDOW)[None, :]))
    dist_s = jnp.broadcast_to(jnp.maximum(WINDOW - jnp.arange(2 * WINDOW), 0)[None, :],
                              (8, 2 * WINDOW))
    bias_s = rel_bias_rows(rel_bias, _t5_bucket(dist_s))[:, 0, :]
    sinks = attn_sinks[0]

    xp = x_prompt.reshape(batch * seq, d)
    q, k, v, z, xbc, dt_raw = norm_matmul(xp, norm_mix_w[0], w_in_p, in_splits, tm)
    o_att = swa_prompt(q, k, v, bias_p, sinks, batch, seq)
    y_ssd, p_state = ssd_prompt(xbc, dt_raw, cw, cb, dtb, alog, dskip_x, e01, batch, seq)
    x1 = merge(xp, o_att, y_ssd, z, attn_norm_w[0], ssd_norm_w[0], w_out1, w_out2, tm)
    mkv_w = jnp.concatenate([w_mk[0], w_mv[0]], axis=1).astype(BF16)
    mk, mv = norm_matmul(mem_prompt.reshape(batch * N_MEM, d), mem_norm_w[0], mkv_w, (d, d), tm)
    y_p = _tail(x1, mk.astype(BF16), mv.astype(BF16), prm, seq, tm, True, tm)

    xs_in = x_sample.reshape(nb, d)
    tms = nb
    qs, ks, vs, zs, xbcs, dts = norm_matmul(xs_in, norm_mix_w[0], w_in_p, in_splits, tms)
    own = (jnp.arange(kvw)[None, :] // ATT_HEAD_DIM) == (jnp.arange(ATT_HEADS)[:, None] // (ATT_HEADS // ATT_KV_HEADS))
    qx = jnp.where(own[None], jnp.tile(qs.reshape(nb, ATT_HEADS, ATT_HEAD_DIM), (1, 1, ATT_KV_HEADS)), 0.0)
    ck = cache_swa_k[0].reshape(nb, WINDOW, kvw)
    cv = cache_swa_v[0].reshape(nb, WINDOW, kvw)
    sink_b = jnp.broadcast_to(sinks[:, None], (ATT_HEADS, LANES))
    nk, nv, ox = swa_sample(qx, ck, cv, ks.reshape(nb, 1, kvw), vs.reshape(nb, 1, kvw), bias_s, sink_b)
    ox5 = ox.reshape(nb, ATT_KV_HEADS, ATT_HEADS // ATT_KV_HEADS, ATT_KV_HEADS, ATT_HEAD_DIM)
    o_att_s = jnp.stack([ox5[:, g, :, g, :] for g in range(ATT_KV_HEADS)], axis=1).reshape(nb, d)
    sc = state_conv[0]
    dec_t, xdt_t, bmat, cmat, xs_s = ssd_sample_prep(xbcs, sc[:, 0], sc[:, 1], sc[:, 2], dts, cw, cb, dtb, alog, e01)
    new_state, y_s = ssd_sample_step(state_ssm[0].reshape(nb, SSD_WIDTH, D_STATE), dec_t, xdt_t, bmat, cmat, xs_s, dskip_x)
    x1s = merge(xs_in, o_att_s, y_s, zs, attn_norm_w[0], ssd_norm_w[0], w_out1, w_out2, tms)
    y_s_out = _tail(x1s, cache_mem_k[0].reshape(nb, N_MEM, d), cache_mem_v[0].reshape(nb, N_MEM, d),
                    prm, seq, tms, False, tms)

    k4 = k.reshape(batch, seq, ATT_KV_HEADS, ATT_HEAD_DIM)
    v4 = v.reshape(batch, seq, ATT_KV_HEADS, ATT_HEAD_DIM)
    xbc3 = xbc.reshape(batch, seq, CONV_DIM)
    return (y_p.reshape(batch, seq, d),
            y_s_out.reshape(nb, 1, d),
            k4[None, :, seq - WINDOW:],
            v4[None, :, seq - WINDOW:],
            p_state.reshape(1, batch, SSD_HEADS, SSD_HEAD_DIM, D_STATE),
            xbc3[None, :, seq - (CONV_WIDTH - 1):],
            mk.reshape(1, batch, N_MEM, MEM_HEADS, MEM_HEAD_DIM),
            mv.reshape(1, batch, N_MEM, MEM_HEADS, MEM_HEAD_DIM),
            nk.reshape(1, nb, WINDOW, ATT_KV_HEADS, ATT_HEAD_DIM),
            nv.reshape(1, nb, WINDOW, ATT_KV_HEADS, ATT_HEAD_DIM),
            new_state.reshape(1, nb, SSD_HEADS, SSD_HEAD_DIM, D_STATE),
            jnp.concatenate([sc[:, 1:], xbcs[:, None, :]], axis=1)[None])
```

```python
import functools
import math

import numpy as np
import jax
import jax.numpy as jnp
from jax import lax
from jax.experimental import pallas as pl
from jax.experimental.pallas import tpu as pltpu

F32 = jnp.float32
BF16 = jnp.bfloat16
EPS = 1e-6

D_MODEL = 1024
ATT_HEADS = 16
ATT_KV_HEADS = 4
ATT_HEAD_DIM = 64
WINDOW = 128
N_BUCKETS = 32
MAX_DISTANCE = 128
SSD_HEADS = 16
SSD_HEAD_DIM = 64
SSD_WIDTH = SSD_HEADS * SSD_HEAD_DIM
SSD_GROUPS = 2
D_STATE = 128
CONV_WIDTH = 4
CONV_DIM = SSD_WIDTH + 2 * SSD_GROUPS * D_STATE
CHUNK = 128
N_MEM = 256
MEM_HEADS = 4
MEM_HEAD_DIM = D_MODEL // MEM_HEADS
PEER_HEADS = 8
N_KEYS = 128
N_EXPERTS = N_KEYS * N_KEYS
PEER_TOPK = 16
PEER_HALF = 128

LANES = 128
HEAD_PAD = 128
GROUP_W = SSD_WIDTH // SSD_GROUPS
VMEM_LIMIT = 56 * 1024 * 1024


def _params(sem):
    return pltpu.CompilerParams(dimension_semantics=sem, vmem_limit_bytes=VMEM_LIMIT)


def _dot(a, b):
    return jnp.dot(a, b, preferred_element_type=F32)


def _dot_nt(a, b):
    return lax.dot_general(a, b, (((1,), (1,)), ((), ())), preferred_element_type=F32)


def _rms(x, w):
    var = jnp.mean(x * x, axis=-1, keepdims=True)
    return x * lax.rsqrt(var + EPS) * w


def _sigmoid(x):
    return 1.0 / (1.0 + jnp.exp(-x))


def _softplus(x):
    return jnp.maximum(x, 0.0) + jnp.log1p(jnp.exp(-jnp.abs(x)))


def _split3(v):
    v1 = v.astype(BF16)
    r1 = v - v1.astype(F32)
    v2 = r1.astype(BF16)
    r2 = r1 - v2.astype(F32)
    return v1, v2, r2.astype(BF16)


def _exact_dot(v, m01):
    v1, v2, v3 = _split3(v)
    return _dot(v1, m01) + _dot(v2, m01) + _dot(v3, m01)


def _exact_dot_left(m01, v):
    v1, v2, v3 = _split3(v)
    return _dot(m01, v1) + _dot(m01, v2) + _dot(m01, v3)


def _norm_matmul_kernel(x_ref, nw_ref, w_ref, *out_refs, splits, emit_h):
    h = _rms(x_ref[...], nw_ref[...]).astype(BF16)
    refs = list(out_refs)
    if emit_h:
        refs.pop(0)[...] = h
    off = 0
    for o_ref, n in zip(refs, splits):
        o_ref[...] = _dot(h, w_ref[:, off:off + n])
        off += n


def norm_matmul(x, nw, w_bf16, splits, tm, emit_h=False):
    t, d = x.shape
    n = w_bf16.shape[1]
    assert sum(splits) == n and t % tm == 0
    out_shape = [jax.ShapeDtypeStruct((t, s), F32) for s in splits]
    out_specs = [pl.BlockSpec((tm, s), lambda i: (i, 0)) for s in splits]
    if emit_h:
        out_shape.insert(0, jax.ShapeDtypeStruct((t, d), BF16))
        out_specs.insert(0, pl.BlockSpec((tm, d), lambda i: (i, 0)))
    return pl.pallas_call(
        functools.partial(_norm_matmul_kernel, splits=tuple(splits), emit_h=emit_h),
        grid=(t // tm,),
        in_specs=[pl.BlockSpec((tm, d), lambda i: (i, 0)),
                  pl.BlockSpec((1, d), lambda i: (0, 0)),
                  pl.BlockSpec((d, n), lambda i: (0, 0))],
        out_specs=out_specs,
        out_shape=out_shape,
        compiler_params=_params(("parallel",)),
        name="norm_matmul",
    )(x, nw.reshape(1, d), w_bf16)


def _bias_kernel(table_ref, bucket_ref, out_ref):
    h = pl.program_id(0)
    bk = bucket_ref[...]
    acc = jnp.zeros(bk.shape, F32)
    for b in range(N_BUCKETS):
        acc = jnp.where(bk == b, table_ref[b, h], acc)
    out_ref[0] = acc


def rel_bias_rows(table, bucket):
    r, c = bucket.shape
    return pl.pallas_call(
        _bias_kernel,
        grid=(ATT_HEADS,),
        in_specs=[pl.BlockSpec(memory_space=pltpu.SMEM),
                  pl.BlockSpec((r, c), lambda h: (0, 0))],
        out_specs=pl.BlockSpec((1, r, c), lambda h: (h, 0, 0)),
        out_shape=jax.ShapeDtypeStruct((ATT_HEADS, r, c), F32),
        compiler_params=_params(("parallel",)),
        name="rel_bias",
    )(table, bucket)


def _t5_bucket(dist):
    n = jnp.maximum(dist, 0)
    max_exact = N_BUCKETS // 2
    nf = jnp.maximum(n, 1).astype(F32)
    large = max_exact + (jnp.log(nf / max_exact) / math.log(MAX_DISTANCE / max_exact)
                         * (N_BUCKETS - max_exact)).astype(jnp.int32)
    large = jnp.minimum(large, N_BUCKETS - 1)
    return jnp.where(n < max_exact, n, large)


def _half_pair(pair, rolled, which, lo):
    if which == 0:
        return jnp.where(lo, pair, 0.0), jnp.where(lo, 0.0, rolled)
    return jnp.where(lo, rolled, 0.0), jnp.where(lo, 0.0, pair)


def _swa_prompt_kernel(sink_ref, q_ref, kc_ref, kp_ref, vc_ref, vp_ref, bias_ref, o_ref):
    i = pl.program_id(1)
    blk = WINDOW
    row = lax.broadcasted_iota(jnp.int32, (blk, 2 * blk), 0)
    col = lax.broadcasted_iota(jnp.int32, (blk, 2 * blk), 1)
    dist = row + blk - col
    ok = (dist >= 0) & (dist <= WINDOW) & ((col >= blk) | (i > 0))
    lo = lax.broadcasted_iota(jnp.int32, (2 * blk, LANES), 1) < ATT_HEAD_DIM
    scale = ATT_HEAD_DIM ** -0.5
    for pair in range(ATT_KV_HEADS // 2):
        sl = slice(pair * LANES, (pair + 1) * LANES)
        kpair = jnp.concatenate([kp_ref[:, sl], kc_ref[:, sl]], axis=0)
        vpair = jnp.concatenate([vp_ref[:, sl], vc_ref[:, sl]], axis=0)
        kroll = pltpu.roll(kpair, ATT_HEAD_DIM, axis=1)
        vroll = pltpu.roll(vpair, ATT_HEAD_DIM, axis=1)
        for gi in range(2):
            g = 2 * pair + gi
            k_lo, k_hi = _half_pair(kpair, kroll, gi, lo)
            v_lo, v_hi = _half_pair(vpair, vroll, gi, lo)
            k_halves = (k_lo.astype(BF16), k_hi.astype(BF16))
            v_halves = (v_lo.astype(BF16), v_hi.astype(BF16))
            for qp in range(2):
                qsl = slice((2 * g + qp) * LANES, (2 * g + qp + 1) * LANES)
                qpair = q_ref[:, qsl].astype(BF16)
                acc = jnp.zeros((blk, LANES), F32)
                for par in range(2):
                    h = 4 * g + 2 * qp + par
                    s = _dot_nt(qpair, k_halves[par]) * scale + bias_ref[h]
                    s = jnp.where(ok, s, -jnp.inf)
                    sink = sink_ref[h]
                    m = jnp.maximum(jnp.max(s, axis=-1, keepdims=True), sink)
                    p = jnp.exp(s - m)
                    den = jnp.sum(p, axis=-1, keepdims=True) + jnp.exp(sink - m)
                    acc = acc + _dot((p / den).astype(BF16), v_halves[par])
                o_ref[:, qsl] = acc


def swa_prompt(q, k, v, bias, sinks, batch, seq):
    nb = seq // WINDOW
    cur = lambda b, i: (b * nb + i, 0)
    prev = lambda b, i: (b * nb + jnp.maximum(i - 1, 0), 0)
    kvw = ATT_KV_HEADS * ATT_HEAD_DIM
    return pl.pallas_call(
        _swa_prompt_kernel,
        grid=(batch, nb),
        in_specs=[pl.BlockSpec(memory_space=pltpu.SMEM),
                  pl.BlockSpec((WINDOW, D_MODEL), cur),
                  pl.BlockSpec((WINDOW, kvw), cur),
                  pl.BlockSpec((WINDOW, kvw), prev),
                  pl.BlockSpec((WINDOW, kvw), cur),
                  pl.BlockSpec((WINDOW, kvw), prev),
                  pl.BlockSpec((ATT_HEADS, WINDOW, 2 * WINDOW), lambda b, i: (0, 0, 0))],
        out_specs=pl.BlockSpec((WINDOW, D_MODEL), cur),
        out_shape=jax.ShapeDtypeStruct((batch * seq, D_MODEL), F32),
        compiler_params=_params(("parallel", "parallel")),
        name="swa_prompt",
    )(sinks, q, k, k, v, v, bias)


def _ssd_prompt_kernel(xbc_ref, dt_ref, cw_ref, cb_ref, dtb_ref, alog_ref, dskip_ref, e_ref,
                       y_ref, st_ref, state_scr, tail_scr):
    c = pl.program_id(1)
    last = pl.num_programs(1) - 1

    @pl.when(c == 0)
    def _():
        state_scr[...] = jnp.zeros(state_scr.shape, F32)
        tail_scr[...] = jnp.zeros(tail_scr.shape, F32)

    x = xbc_ref[...]
    tail = tail_scr[...]
    row8 = lax.broadcasted_iota(jnp.int32, tail.shape, 0)
    acc = x * cw_ref[CONV_WIDTH - 1:CONV_WIDTH, :]
    for s in range(1, CONV_WIDTH):
        xr = pltpu.roll(x, s, axis=0)
        tr = pltpu.roll(tail, s, axis=0)
        head = jnp.where(row8 < s, tr, xr[0:8])
        shifted = jnp.concatenate([head, xr[8:]], axis=0)
        acc = acc + shifted * cw_ref[CONV_WIDTH - 1 - s:CONV_WIDTH - s, :]
    tail_scr[...] = x[CHUNK - 8:CHUNK]
    u = acc + cb_ref[...]
    u = u * _sigmoid(u)
    xs = u[:, :SSD_WIDTH]
    bm = u[:, SSD_WIDTH:SSD_WIDTH + SSD_GROUPS * D_STATE]
    cm = u[:, SSD_WIDTH + SSD_GROUPS * D_STATE:]

    dtv = _softplus(dt_ref[...] + dtb_ref[...])
    a = dtv * (-jnp.exp(alog_ref[...]))
    r = lax.broadcasted_iota(jnp.int32, (CHUNK, CHUNK), 0)
    cc = lax.broadcasted_iota(jnp.int32, (CHUNK, CHUNK), 1)
    causal = r >= cc
    tril = jnp.where(causal, 1.0, 0.0).astype(BF16)
    cs = _exact_dot_left(tril, a)
    cs_t = cs.T
    dte = jnp.exp(cs[CHUNK - 1:CHUNK, :] - cs)
    ecs = jnp.exp(cs)
    e01 = e_ref[...]
    dt_x = _exact_dot(dtv, e01)
    dte_x = _exact_dot(dte, e01)
    ecs_x = _exact_dot(ecs, e01)
    xdt = xs * dt_x
    xdec = (xdt * dte_x).astype(BF16)
    xdt_b = xdt.astype(BF16)
    lo = lax.broadcasted_iota(jnp.int32, (CHUNK, LANES), 1) < SSD_HEAD_DIM
    dsk = dskip_ref[...]
    for g in range(SSD_GROUPS):
        gs = slice(g * GROUP_W, (g + 1) * GROUP_W)
        bg = bm[:, g * D_STATE:(g + 1) * D_STATE]
        cg = cm[:, g * D_STATE:(g + 1) * D_STATE].astype(BF16)
        cb = _dot_nt(cg, bg.astype(BF16))
        bgt = bg.T.astype(BF16)
        prev = state_scr[g]
        y_off = _dot(cg, prev.astype(BF16)) * ecs_x[:, gs]
        state_scr[g] = ecs_x[CHUNK - 1:CHUNK, gs] * prev + _dot(bgt, xdec[:, gs])
        for m in range(GROUP_W // LANES):
            ps = slice(g * GROUP_W + m * LANES, g * GROUP_W + (m + 1) * LANES)
            xp = xdt_b[:, ps]
            ydiag = jnp.zeros((CHUNK, LANES), F32)
            for par in range(2):
                j = g * (SSD_HEADS // SSD_GROUPS) + 2 * m + par
                diff = cs[:, j:j + 1] - cs_t[j:j + 1, :]
                ldec = jnp.exp(jnp.where(causal, diff, -jnp.inf))
                mm = (cb * ldec).astype(BF16)
                xpm = jnp.where(lo, xp, 0.0) if par == 0 else jnp.where(lo, 0.0, xp)
                ydiag = ydiag + _dot(mm, xpm.astype(BF16))
            y_ref[:, ps] = ydiag + y_off[:, m * LANES:(m + 1) * LANES] + dsk[:, ps] * xs[:, ps]

    @pl.when(c == last)
    def _():
        for g in range(SSD_GROUPS):
            st_ref[0, g * GROUP_W:(g + 1) * GROUP_W, :] = state_scr[g].T


def ssd_prompt(xbc, dt_raw, cw, cb, dtb, alog, dskip_x, e01, batch, seq):
    nc = seq // CHUNK
    cur = lambda b, c: (b * nc + c, 0)
    const = lambda b, c: (0, 0)
    return pl.pallas_call(
        _ssd_prompt_kernel,
        grid=(batch, nc),
        in_specs=[pl.BlockSpec((CHUNK, CONV_DIM), cur),
                  pl.BlockSpec((CHUNK, HEAD_PAD), cur),
                  pl.BlockSpec((CONV_WIDTH, CONV_DIM), const),
                  pl.BlockSpec((1, CONV_DIM), const),
                  pl.BlockSpec((1, HEAD_PAD), const),
                  pl.BlockSpec((1, HEAD_PAD), const),
                  pl.BlockSpec((1, SSD_WIDTH), const),
                  pl.BlockSpec((HEAD_PAD, SSD_WIDTH), const)],
        out_specs=[pl.BlockSpec((CHUNK, SSD_WIDTH), cur),
                   pl.BlockSpec((1, SSD_WIDTH, D_STATE), lambda b, c: (b, 0, 0))],
        out_shape=[jax.ShapeDtypeStruct((batch * seq, SSD_WIDTH), F32),
                   jax.ShapeDtypeStruct((batch, SSD_WIDTH, D_STATE), F32)],
        scratch_shapes=[pltpu.VMEM((SSD_GROUPS, D_STATE, GROUP_W), F32),
                        pltpu.VMEM((8, CONV_DIM), F32)],
        compiler_params=_params(("parallel", "arbitrary")),
        name="ssd_prompt",
    )(xbc, dt_raw, cw, cb, dtb, alog, dskip_x, e01)


def _swa_sample_kernel(qx_ref, ck_ref, cv_ref, kn_ref, vn_ref, bias_ref, sink_ref,
                       nk_ref, nv_ref, o_ref, *, bt):
    scale = ATT_HEAD_DIM ** -0.5
    kvw = ATT_KV_HEADS * ATT_HEAD_DIM
    rowi = lax.broadcasted_iota(jnp.int32, (WINDOW, kvw), 0)
    bias_c = bias_ref[:, 0:WINDOW]
    bias_n = bias_ref[:, WINDOW:WINDOW + 1]
    sink = sink_ref[:, 0:1]
    for bb in range(bt):
        kc = ck_ref[bb]
        vc = cv_ref[bb]
        kn = kn_ref[bb]
        vn = vn_ref[bb]
        qx = qx_ref[bb]
        s_c = _dot_nt(qx.astype(BF16), kc.astype(BF16)) * scale + bias_c
        s_n = jnp.sum(qx * kn, axis=1, keepdims=True) * scale + bias_n
        m = jnp.maximum(jnp.maximum(jnp.max(s_c, axis=1, keepdims=True), s_n), sink)
        p_c = jnp.exp(s_c - m)
        p_n = jnp.exp(s_n - m)
        den = jnp.sum(p_c, axis=1, keepdims=True) + p_n + jnp.exp(sink - m)
        o_ref[bb] = _dot((p_c / den).astype(BF16), vc.astype(BF16)) + (p_n / den) * vn
        nk_ref[bb] = jnp.where(rowi == WINDOW - 1, kn, pltpu.roll(kc, WINDOW - 1, axis=0))
        nv_ref[bb] = jnp.where(rowi == WINDOW - 1, vn, pltpu.roll(vc, WINDOW - 1, axis=0))


def swa_sample(qx, ck, cv, kn, vn, bias_s, sink_b, bt=8):
    nb = qx.shape[0]
    kvw = ATT_KV_HEADS * ATT_HEAD_DIM
    blk3 = lambda i: (i, 0, 0)
    return pl.pallas_call(
        functools.partial(_swa_sample_kernel, bt=bt),
        grid=(nb // bt,),
        in_specs=[pl.BlockSpec((bt, ATT_HEADS, kvw), blk3),
                  pl.BlockSpec((bt, WINDOW, kvw), blk3),
                  pl.BlockSpec((bt, WINDOW, kvw), blk3),
                  pl.BlockSpec((bt, 1, kvw), blk3),
                  pl.BlockSpec((bt, 1, kvw), blk3),
                  pl.BlockSpec((ATT_HEADS, 2 * WINDOW), lambda i: (0, 0)),
                  pl.BlockSpec((ATT_HEADS, LANES), lambda i: (0, 0))],
        out_specs=[pl.BlockSpec((bt, WINDOW, kvw), blk3),
                   pl.BlockSpec((bt, WINDOW, kvw), blk3),
                   pl.BlockSpec((bt, ATT_HEADS, kvw), blk3)],
        out_shape=[jax.ShapeDtypeStruct((nb, WINDOW, kvw), F32),
                   jax.ShapeDtypeStruct((nb, WINDOW, kvw), F32),
                   jax.ShapeDtypeStruct((nb, ATT_HEADS, kvw), F32)],
        compiler_params=_params(("parallel",)),
        name="swa_sample",
    )(qx, ck, cv, kn, vn, bias_s, sink_b)


def _ssd_sample_prep_kernel(xbc_ref, c0_ref, c1_ref, c2_ref, dt_ref, cw_ref, cb_ref, dtb_ref,
                            alog_ref, e_ref, dec_t_ref, xdt_t_ref, b_ref, c_ref, xs_ref):
    u = (c0_ref[...] * cw_ref[0:1, :] + c1_ref[...] * cw_ref[1:2, :]
         + c2_ref[...] * cw_ref[2:3, :] + xbc_ref[...] * cw_ref[3:4, :]) + cb_ref[...]
    u = u * _sigmoid(u)
    xs = u[:, :SSD_WIDTH]
    b_ref[...] = u[:, SSD_WIDTH:SSD_WIDTH + SSD_GROUPS * D_STATE]
    c_ref[...] = u[:, SSD_WIDTH + SSD_GROUPS * D_STATE:]
    xs_ref[...] = xs
    dtv = _softplus(dt_ref[...] + dtb_ref[...])
    dec = jnp.exp(dtv * (-jnp.exp(alog_ref[...])))
    e01 = e_ref[...]
    dec_t_ref[...] = _exact_dot(dec, e01).T
    xdt_t_ref[...] = (xs * _exact_dot(dtv, e01)).T


def ssd_sample_prep(xbc, c0, c1, c2, dt_raw, cw, cb, dtb, alog, e01):
    nb = xbc.shape[0]
    args = (xbc, c0, c1, c2, dt_raw, cw, cb, dtb, alog, e01)
    full = lambda a: pl.BlockSpec(a.shape, lambda i: (0,) * a.ndim)
    out_shape = [jax.ShapeDtypeStruct((SSD_WIDTH, nb), F32),
                 jax.ShapeDtypeStruct((SSD_WIDTH, nb), F32),
                 jax.ShapeDtypeStruct((nb, SSD_GROUPS * D_STATE), F32),
                 jax.ShapeDtypeStruct((nb, SSD_GROUPS * D_STATE), F32),
                 jax.ShapeDtypeStruct((nb, SSD_WIDTH), F32)]
    return pl.pallas_call(
        _ssd_sample_prep_kernel,
        grid=(1,),
        in_specs=[full(a) for a in args],
        out_specs=[full(s) for s in out_shape],
        out_shape=out_shape,
        compiler_params=_params(("arbitrary",)),
        name="ssd_sample_prep",
    )(*args)


def _ssd_sample_step_kernel(st_ref, dec_t_ref, xdt_t_ref, b_ref, c_ref, xs_ref, dskip_ref,
                            ns_ref, y_ref, yt_scr):
    b = pl.program_id(0)
    nb = dec_t_ref.shape[1]

    @pl.when(b == 0)
    def _():
        yt_scr[...] = jnp.zeros(yt_scr.shape, F32)

    pick = jnp.where(lax.broadcasted_iota(jnp.int32, (nb, D_STATE), 0) == b,
                     1.0, 0.0).astype(BF16)
    dcol = _exact_dot(dec_t_ref[...], pick)
    xcol = _exact_dot(xdt_t_ref[...], pick)
    brow = b_ref[pl.ds(b, 1), :]
    crow = c_ref[pl.ds(b, 1), :]
    lane = lax.broadcasted_iota(jnp.int32, (GROUP_W, nb), 1)
    for g in range(SSD_GROUPS):
        gs = slice(g * GROUP_W, (g + 1) * GROUP_W)
        ns = slice(g * D_STATE, (g + 1) * D_STATE)
        hn = dcol[gs] * st_ref[0, gs, :] + xcol[gs] * brow[:, ns]
        ns_ref[0, gs, :] = hn
        ycol = jnp.sum(hn * crow[:, ns], axis=1, keepdims=True)
        yt_scr[gs, :] = jnp.where(lane == b, ycol, yt_scr[gs, :])

    @pl.when(b == nb - 1)
    def _():
        y_ref[...] = yt_scr[...].T + dskip_ref[...] * xs_ref[...]


def ssd_sample_step(state, dec_t, xdt_t, bmat, cmat, xs, dskip_x):
    nb = state.shape[0]
    assert nb == D_STATE
    const2 = lambda i: (0, 0)
    return pl.pallas_call(
        _ssd_sample_step_kernel,
        grid=(nb,),
        in_specs=[pl.BlockSpec((1, SSD_WIDTH, D_STATE), lambda i: (i, 0, 0)),
                  pl.BlockSpec((SSD_WIDTH, nb), const2),
                  pl.BlockSpec((SSD_WIDTH, nb), const2),
                  pl.BlockSpec((nb, SSD_GROUPS * D_STATE), const2),
                  pl.BlockSpec((nb, SSD_GROUPS * D_STATE), const2),
                  pl.BlockSpec((nb, SSD_WIDTH), const2),
                  pl.BlockSpec((1, SSD_WIDTH), const2)],
        out_specs=[pl.BlockSpec((1, SSD_WIDTH, D_STATE), lambda i: (i, 0, 0)),
                   pl.BlockSpec((nb, SSD_WIDTH), const2)],
        out_shape=[jax.ShapeDtypeStruct((nb, SSD_WIDTH, D_STATE), F32),
                   jax.ShapeDtypeStruct((nb, SSD_WIDTH), F32)],
        scratch_shapes=[pltpu.VMEM((SSD_WIDTH, nb), F32)],
        compiler_params=_params(("arbitrary",)),
        name="ssd_sample_step",
    )(state, dec_t, xdt_t, bmat, cmat, xs, dskip_x)


def _merge_kernel(x_ref, o_ref, y_ref, z_ref, anw_ref, snw_ref, w1_ref, w2_ref, out_ref):
    ya = _rms(o_ref[...], anw_ref[...]).astype(BF16)
    z = z_ref[...]
    ys = _rms(y_ref[...] * (z * _sigmoid(z)), snw_ref[...]).astype(BF16)
    out_ref[...] = x_ref[...] + _dot(ya, w1_ref[...]) + _dot(ys, w2_ref[...])


def merge(x, o_att, y_ssd, z, anw, snw, w1, w2, tm):
    t, d = x.shape
    row = pl.BlockSpec((tm, d), lambda i: (i, 0))
    vec = pl.BlockSpec((1, d), lambda i: (0, 0))
    mat = pl.BlockSpec((d, d), lambda i: (0, 0))
    return pl.pallas_call(
        _merge_kernel,
        grid=(t // tm,),
        in_specs=[row, row, row, row, vec, vec, mat, mat],
        out_specs=row,
        out_shape=jax.ShapeDtypeStruct((t, d), F32),
        compiler_params=_params(("parallel",)),
        name="merge",
    )(x, o_att, y_ssd, z, anw.reshape(1, d), snw.reshape(1, d), w1, w2)


def _mem_prompt_kernel(x_ref, nw_ref, wq_ref, mk_ref, mv_ref, wo_ref, out_ref):
    x = x_ref[...]
    q = _dot(_rms(x, nw_ref[...]).astype(BF16), wq_ref[...])
    scale = MEM_HEAD_DIM ** -0.5
    outs = []
    for h in range(MEM_HEADS):
        hs = slice(h * MEM_HEAD_DIM, (h + 1) * MEM_HEAD_DIM)
        s = _dot_nt(q[:, hs].astype(BF16), mk_ref[:, hs]) * scale
        p = jnp.exp(s - jnp.max(s, axis=-1, keepdims=True))
        p = p / jnp.sum(p, axis=-1, keepdims=True)
        outs.append(_dot(p.astype(BF16), mv_ref[:, hs]).astype(BF16))
    out_ref[...] = x + _dot(jnp.concatenate(outs, axis=1), wo_ref[...])


def mem_attn_prompt(x, nw, wq, mk, mv, wo, seq, tm):
    t, d = x.shape
    per = seq // tm
    row = pl.BlockSpec((tm, d), lambda i: (i, 0))
    mat = pl.BlockSpec((d, d), lambda i: (0, 0))
    mem = pl.BlockSpec((N_MEM, d), lambda i: (i // per, 0))
    return pl.pallas_call(
        _mem_prompt_kernel,
        grid=(t // tm,),
        in_specs=[row, pl.BlockSpec((1, d), lambda i: (0, 0)), mat, mem, mem, mat],
        out_specs=row,
        out_shape=jax.ShapeDtypeStruct((t, d), F32),
        compiler_params=_params(("parallel",)),
        name="mem_prompt",
    )(x, nw.reshape(1, d), wq, mk, mv, wo)


def _mem_sample_kernel(q_ref, k_ref, v_ref, o_ref):
    q = q_ref[0]
    outs = []
    for h in range(MEM_HEADS):
        qh = jnp.broadcast_to(q[:, h * MEM_HEAD_DIM:(h + 1) * MEM_HEAD_DIM], (8, MEM_HEAD_DIM))
        s = _dot_nt(qh.astype(BF16), k_ref[0, :, h, :].astype(BF16)) * (MEM_HEAD_DIM ** -0.5)
        p = jnp.exp(s - jnp.max(s, axis=-1, keepdims=True))
        p = p / jnp.sum(p, axis=-1, keepdims=True)
        outs.append(_dot(p.astype(BF16), v_ref[0, :, h, :].astype(BF16))[0:1])
    o_ref[0] = jnp.concatenate(outs, axis=1)


def mem_sample(q, ck, cv):
    nb, d = q.shape
    blk = pl.BlockSpec((1, N_MEM, MEM_HEADS, MEM_HEAD_DIM), lambda i: (i, 0, 0, 0))
    vec = pl.BlockSpec((1, 1, d), lambda i: (i, 0, 0))
    return pl.pallas_call(
        _mem_sample_kernel,
        grid=(nb,),
        in_specs=[vec, blk, blk],
        out_specs=vec,
        out_shape=jax.ShapeDtypeStruct((nb, 1, d), F32),
        compiler_params=_params(("parallel",)),
        name="mem_sample",
    )(q.reshape(nb, 1, d), ck, cv).reshape(nb, d)


def _matmul_res_kernel(a_ref, w_ref, r_ref, out_ref):
    out_ref[...] = r_ref[...] + _dot(a_ref[...].astype(BF16), w_ref[...])


def matmul_res(a, w, res, tm):
    t, d = res.shape
    kdim = a.shape[1]
    return pl.pallas_call(
        _matmul_res_kernel,
        grid=(t // tm,),
        in_specs=[pl.BlockSpec((tm, kdim), lambda i: (i, 0)),
                  pl.BlockSpec((kdim, d), lambda i: (0, 0)),
                  pl.BlockSpec((tm, d), lambda i: (i, 0))],
        out_specs=pl.BlockSpec((tm, d), lambda i: (i, 0)),
        out_shape=jax.ShapeDtypeStruct((t, d), F32),
        compiler_params=_params(("parallel",)),
        name="matmul_res",
    )(a, w, res)


def _topk_rows(work, k):
    rows = work.shape[0]
    iota = lax.broadcasted_iota(jnp.int32, work.shape, 0)
    rank = jnp.full(work.shape, k, jnp.int32)
    vals = []
    for r in range(k):
        m = jnp.max(work, axis=0, keepdims=True)
        idx = jnp.min(jnp.where(work == m, iota, rows), axis=0, keepdims=True)
        hit = iota == idx
        rank = jnp.where(hit, r, rank)
        work = jnp.where(hit, -jnp.inf, work)
        vals.append(m)
    return jnp.concatenate(vals, axis=0), rank


def _peer_select_kernel(pq_ref, keys_ref, sel_ref):
    k = PEER_TOPK
    for h in range(PEER_HEADS):
        subs, tops, ranks = [], [], []
        for p in range(2):
            hp = 2 * h + p
            qhp = pq_ref[:, hp * PEER_HALF:(hp + 1) * PEER_HALF].astype(BF16)
            sub = _dot_nt(keys_ref[hp], qhp)
            top, rank = _topk_rows(sub, k)
            subs.append(sub)
            tops.append(top)
            ranks.append(rank)
        a, b = tops
        lo4 = lax.broadcasted_iota(jnp.int32, (8, a.shape[1]), 0) < 4
        b8 = b[0:8]
        b44 = jnp.where(lo4, b8, pltpu.roll(b8, 4, axis=0))
        cand = jnp.concatenate(
            [a[0:1] + b8, a[0:1] + b[8:16], a[1:2] + b8, a[2:3] + b8, a[3:4] + b8,
             jnp.where(lo4, a[4:5], a[5:6]) + b44, jnp.where(lo4, a[6:7], a[7:8]) + b44,
             a[8:16] + b[0:1]], axis=0)
        _, crank = _topk_rows(cand, k)
        chosen = crank < k
        cmax = a[0:1, :] + b[0:1, :]
        z = jnp.sum(jnp.where(chosen, jnp.exp(cand - cmax), 0.0), axis=0, keepdims=True)
        ch = [jnp.where(chosen[8 * i:8 * i + 8], 1.0, 0.0) for i in range(8)]
        cnt = lambda v: jnp.sum(v, axis=0, keepdims=True)
        n = [cnt(ch[0]) + cnt(ch[1]), cnt(ch[2]), cnt(ch[3]), cnt(ch[4]),
             cnt(jnp.where(lo4, ch[5], 0.0)), cnt(jnp.where(lo4, 0.0, ch[5])),
             cnt(jnp.where(lo4, ch[6], 0.0)), cnt(jnp.where(lo4, 0.0, ch[6]))]
        n += [ch[7][i:i + 1] for i in range(8)]
        lim0 = jnp.zeros(ranks[0].shape, F32)
        for r in range(k):
            lim0 = jnp.where(ranks[0] == r, n[r], lim0)
        sel_ref[h, 0] = jnp.exp(subs[0] - a[0:1, :]) / z
        sel_ref[h, 1] = lim0
        sel_ref[h, 2] = jnp.exp(subs[1] - b[0:1, :])
        sel_ref[h, 3] = ranks[1].astype(F32)


def peer_select(pq, keys_bf16, tm):
    t = pq.shape[0]
    return pl.pallas_call(
        _peer_select_kernel,
        grid=(t // tm,),
        in_specs=[pl.BlockSpec((tm, pq.shape[1]), lambda i: (i, 0)),
                  pl.BlockSpec(keys_bf16.shape, lambda i: (0, 0, 0))],
        out_specs=pl.BlockSpec((PEER_HEADS, 4, N_KEYS, tm), lambda i: (0, 0, 0, i)),
        out_shape=jax.ShapeDtypeStruct((PEER_HEADS, 4, N_KEYS, t), F32),
        compiler_params=_params(("parallel",)),
        name="peer_select",
    )(pq, keys_bf16)


def _gelu(x):
    return 0.5 * x * (1.0 + lax.erf(x * np.float32(math.sqrt(0.5))))


def _peer_dense_kernel(xn_ref, x_ref, u_ref, vt_ref, sel_ref, fnw_ref, y_ref,
                       acc_scr, act_scr, w_scr, *, tm, ib):
    e = pl.program_id(1)
    ne = pl.num_programs(1)

    @pl.when(e == 0)
    def _():
        acc_scr[...] = jnp.zeros(acc_scr.shape, F32)

    act_scr[...] = _dot_nt(u_ref[...], xn_ref[...])

    i0 = pl.ds(pl.multiple_of(e * ib, ib), ib)
    for lt in range(tm // LANES):
        ls = slice(lt * LANES, (lt + 1) * LANES)
        e0s = [sel_ref[h, 0, i0, ls] for h in range(PEER_HEADS)]
        l0s = [sel_ref[h, 1, i0, ls] for h in range(PEER_HEADS)]
        for ii in range(ib):
            rs = slice(ii * N_KEYS, (ii + 1) * N_KEYS)
            g = jnp.zeros((N_KEYS, LANES), F32)
            for h in range(PEER_HEADS):
                keep = sel_ref[h, 3, :, ls] < l0s[h][ii:ii + 1, :]
                g = g + jnp.where(keep, sel_ref[h, 2, :, ls], 0.0) * e0s[h][ii:ii + 1, :]
            w_scr[rs, ls] = (g * _gelu(act_scr[rs, ls])).astype(BF16)
    acc_scr[...] += _dot(vt_ref[...], w_scr[...])

    @pl.when(e == ne - 1)
    def _():
        y_ref[...] = _rms(x_ref[...] + acc_scr[...].T, fnw_ref[...])


def peer_dense(xn, x, u_bf16, vt_bf16, sel, fnw, tm, ib=8):
    t, d = x.shape
    eb = ib * N_KEYS
    return pl.pallas_call(
        functools.partial(_peer_dense_kernel, tm=tm, ib=ib),
        grid=(t // tm, N_EXPERTS // eb),
        in_specs=[pl.BlockSpec((tm, d), lambda i, e: (i, 0)),
                  pl.BlockSpec((tm, d), lambda i, e: (i, 0)),
                  pl.BlockSpec((eb, d), lambda i, e: (e, 0)),
                  pl.BlockSpec((d, eb), lambda i, e: (0, e)),
                  pl.BlockSpec((PEER_HEADS, 4, N_KEYS, tm), lambda i, e: (0, 0, 0, i)),
                  pl.BlockSpec((1, d), lambda i, e: (0, 0))],
        out_specs=pl.BlockSpec((tm, d), lambda i, e: (i, 0)),
        out_shape=jax.ShapeDtypeStruct((t, d), F32),
        scratch_shapes=[pltpu.VMEM((d, tm), F32),
                        pltpu.VMEM((eb, tm), F32),
                        pltpu.VMEM((eb, tm), BF16)],
        compiler_params=_params(("parallel", "arbitrary")),
        name="peer_dense",
    )(xn, x, u_bf16, vt_bf16, sel, fnw.reshape(1, d))


def _tail(x, mk, mv, prm, seq, tm, mem_is_shared, tm_peer):
    if mem_is_shared:
        x = mem_attn_prompt(x, prm["norm_mem_w"], prm["w_mq"], mk, mv, prm["w_mo"], seq, tm)
    else:
        (q,) = norm_matmul(x, prm["norm_mem_w"], prm["w_mq"], (D_MODEL,), tm)
        x = matmul_res(mem_sample(q, mk, mv), prm["w_mo"], x, tm)
    xn, pq = norm_matmul(x, prm["norm_ffn_w"], prm["peer_wq"], (prm["peer_wq"].shape[1],), tm,
                         emit_h=True)
    sel = peer_select(pq, prm["peer_keys"], min(tm, 256))
    return peer_dense(xn, x, prm["peer_u"], prm["peer_vt"], sel, prm["final_norm_w"], tm_peer)


def kernel(x_prompt, x_sample, mem_prompt, cache_swa_k, cache_swa_v, state_ssm, state_conv, cache_mem_k, cache_mem_v, norm_mix_w, w_in, attn_sinks, rel_bias, attn_norm_w, conv_w, conv_b, dt_bias, a_log, d_skip, ssd_norm_w, w_out, norm_mem_w, mem_norm_w, w_mq, w_mk, w_mv, w_mo, norm_ffn_w, peer_wq, peer_keys, peer_u, peer_v, final_norm_w):
    assert w_in.shape[0] == 1, "single-layer step"
    batch, seq, d = x_prompt.shape
    nb = x_sample.shape[0]
    kvw = ATT_KV_HEADS * ATT_HEAD_DIM
    tm = 256

    n_main = D_MODEL + 2 * kvw + SSD_WIDTH + CONV_DIM
    w_in_p = jnp.concatenate(
        [w_in[0], jnp.zeros((d, HEAD_PAD - SSD_HEADS), F32)], axis=1).astype(BF16)
    in_splits = (D_MODEL, kvw, kvw, SSD_WIDTH, CONV_DIM, HEAD_PAD)
    assert n_main + SSD_HEADS == w_in.shape[2]
    pad_h = lambda v: jnp.pad(v.reshape(1, SSD_HEADS), ((0, 0), (0, HEAD_PAD - SSD_HEADS)))
    dtb, alog = pad_h(dt_bias[0]), pad_h(a_log[0])
    dskip_x = jnp.repeat(d_skip[0], SSD_HEAD_DIM).reshape(1, SSD_WIDTH)
    e01 = (jnp.arange(HEAD_PAD)[:, None] == jnp.arange(SSD_WIDTH)[None, :] // SSD_HEAD_DIM
           ).astype(BF16)
    cw, cb = conv_w[0], conv_b[0].reshape(1, CONV_DIM)
    w_out1 = w_out[0, :D_MODEL].astype(BF16)
    w_out2 = w_out[0, D_MODEL:].astype(BF16)
    prm = dict(norm_mem_w=norm_mem_w[0], w_mq=w_mq[0].astype(BF16), w_mo=w_mo[0].astype(BF16),
               norm_ffn_w=norm_ffn_w[0], peer_wq=peer_wq[0].astype(BF16),
               peer_keys=peer_keys[0].reshape(2 * PEER_HEADS, N_KEYS, PEER_HALF).astype(BF16),
               peer_u=peer_u[0].astype(BF16), peer_vt=peer_v[0].astype(BF16).T,
               final_norm_w=final_norm_w)

    qi = jnp.arange(WINDOW)[:, None] + WINDOW
    bias_p = rel_bias_rows(rel_bias, _t5_bucket(qi - jnp.arange(2 * WINDOW)[None, :]))
    dist_s = jnp.broadcast_to(jnp.maximum(WINDOW - jnp.arange(2 * WINDOW), 0)[None, :],
                              (8, 2 * WINDOW))
    bias_s = rel_bias_rows(rel_bias, _t5_bucket(dist_s))[:, 0, :]
    sinks = attn_sinks[0]

    xp = x_prompt.reshape(batch * seq, d)
    q, k, v, z, xbc, dt_raw = norm_matmul(xp, norm_mix_w[0], w_in_p, in_splits, tm)
    o_att = swa_prompt(q, k, v, bias_p, sinks, batch, seq)
    y_ssd, p_state = ssd_prompt(xbc, dt_raw, cw, cb, dtb, alog, dskip_x, e01, batch, seq)
    x1 = merge(xp, o_att, y_ssd, z, attn_norm_w[0], ssd_norm_w[0], w_out1, w_out2, tm)
    mkv_w = jnp.concatenate([w_mk[0], w_mv[0]], axis=1).astype(BF16)
    mk, mv = norm_matmul(mem_prompt.reshape(batch * N_MEM, d), mem_norm_w[0], mkv_w, (d, d), tm)
    y_p = _tail(x1, mk.astype(BF16), mv.astype(BF16), prm, seq, tm, True, 2 * tm)

    xs_in = x_sample.reshape(nb, d)
    tms = nb
    qs, ks, vs, zs, xbcs, dts = norm_matmul(xs_in, norm_mix_w[0], w_in_p, in_splits, tms)
    own = (jnp.arange(kvw)[None, :] // ATT_HEAD_DIM) == (jnp.arange(ATT_HEADS)[:, None] // (ATT_HEADS // ATT_KV_HEADS))
    qx = jnp.where(own[None], jnp.tile(qs.reshape(nb, ATT_HEADS, ATT_HEAD_DIM), (1, 1, ATT_KV_HEADS)), 0.0)
    ck = cache_swa_k[0].reshape(nb, WINDOW, kvw)
    cv = cache_swa_v[0].reshape(nb, WINDOW, kvw)
    sink_b = jnp.broadcast_to(sinks[:, None], (ATT_HEADS, LANES))
    nk, nv, ox = swa_sample(qx, ck, cv, ks.reshape(nb, 1, kvw), vs.reshape(nb, 1, kvw), bias_s, sink_b)
    ox5 = ox.reshape(nb, ATT_KV_HEADS, ATT_HEADS // ATT_KV_HEADS, ATT_KV_HEADS, ATT_HEAD_DIM)
    o_att_s = jnp.stack([ox5[:, g, :, g, :] for g in range(ATT_KV_HEADS)], axis=1).reshape(nb, d)
    sc = state_conv[0]
    dec_t, xdt_t, bmat, cmat, xs_s = ssd_sample_prep(xbcs, sc[:, 0], sc[:, 1], sc[:, 2], dts, cw, cb, dtb, alog, e01)
    new_state, y_s = ssd_sample_step(state_ssm[0].reshape(nb, SSD_WIDTH, D_STATE), dec_t, xdt_t, bmat, cmat, xs_s, dskip_x)
    x1s = merge(xs_in, o_att_s, y_s, zs, attn_norm_w[0], ssd_norm_w[0], w_out1, w_out2, tms)
    y_s_out = _tail(x1s, cache_mem_k[0], cache_mem_v[0], prm, seq, tms, False, tms)

    k4 = k.reshape(batch, seq, ATT_KV_HEADS, ATT_HEAD_DIM)
    v4 = v.reshape(batch, seq, ATT_KV_HEADS, ATT_HEAD_DIM)
    xbc3 = xbc.reshape(batch, seq, CONV_DIM)
    return (y_p.reshape(batch, seq, d),
            y_s_out.reshape(nb, 1, d),
            k4[None, :, seq - WINDOW:],
            v4[None, :, seq - WINDOW:],
            p_state.reshape(1, batch, SSD_HEADS, SSD_HEAD_DIM, D_STATE),
            xbc3[None, :, seq - (CONV_WIDTH - 1):],
            mk.reshape(1, batch, N_MEM, MEM_HEADS, MEM_HEAD_DIM),
            mv.reshape(1, batch, N_MEM, MEM_HEADS, MEM_HEAD_DIM),
            nk.reshape(1, nb, WINDOW, ATT_KV_HEADS, ATT_HEAD_DIM),
            nv.reshape(1, nb, WINDOW, ATT_KV_HEADS, ATT_HEAD_DIM),
            new_state.reshape(1, nb, SSD_HEADS, SSD_HEAD_DIM, D_STATE),
            jnp.concatenate([sc[:, 1:], xbcs[:, None, :]], axis=1)[None])
```

```python
import functools
import math

import numpy as np
import jax
import jax.numpy as jnp
from jax import lax
from jax.experimental import pallas as pl
from jax.experimental.pallas import tpu as pltpu

F32 = jnp.float32
BF16 = jnp.bfloat16
EPS = 1e-6

D_MODEL = 1024
ATT_HEADS = 16
ATT_KV_HEADS = 4
ATT_HEAD_DIM = 64
WINDOW = 128
N_BUCKETS = 32
MAX_DISTANCE = 128
SSD_HEADS = 16
SSD_HEAD_DIM = 64
SSD_WIDTH = SSD_HEADS * SSD_HEAD_DIM
SSD_GROUPS = 2
D_STATE = 128
CONV_WIDTH = 4
CONV_DIM = SSD_WIDTH + 2 * SSD_GROUPS * D_STATE
CHUNK = 128
N_MEM = 256
MEM_HEADS = 4
MEM_HEAD_DIM = D_MODEL // MEM_HEADS
PEER_HEADS = 8
N_KEYS = 128
N_EXPERTS = N_KEYS * N_KEYS
PEER_TOPK = 16
PEER_HALF = 128

LANES = 128
HEAD_PAD = 128
GROUP_W = SSD_WIDTH // SSD_GROUPS
VMEM_LIMIT = 56 * 1024 * 1024


def _params(sem):
    return pltpu.CompilerParams(dimension_semantics=sem, vmem_limit_bytes=VMEM_LIMIT)


def _dot(a, b):
    return jnp.dot(a, b, preferred_element_type=F32)


def _dot_nt(a, b):
    return lax.dot_general(a, b, (((1,), (1,)), ((), ())), preferred_element_type=F32)


def _rms(x, w):
    var = jnp.mean(x * x, axis=-1, keepdims=True)
    return x * lax.rsqrt(var + EPS) * w


def _sigmoid(x):
    return 1.0 / (1.0 + jnp.exp(-x))


def _softplus(x):
    return jnp.maximum(x, 0.0) + jnp.log1p(jnp.exp(-jnp.abs(x)))


def _split3(v):
    v1 = v.astype(BF16)
    r1 = v - v1.astype(F32)
    v2 = r1.astype(BF16)
    r2 = r1 - v2.astype(F32)
    return v1, v2, r2.astype(BF16)


def _exact_dot(v, m01):
    v1, v2, v3 = _split3(v)
    return _dot(v1, m01) + _dot(v2, m01) + _dot(v3, m01)


def _exact_dot_left(m01, v):
    v1, v2, v3 = _split3(v)
    return _dot(m01, v1) + _dot(m01, v2) + _dot(m01, v3)


def _norm_matmul_kernel(x_ref, nw_ref, w_ref, *out_refs, splits, emit_h):
    h = _rms(x_ref[...], nw_ref[...]).astype(BF16)
    refs = list(out_refs)
    if emit_h:
        refs.pop(0)[...] = h
    off = 0
    for o_ref, n in zip(refs, splits):
        o_ref[...] = _dot(h, w_ref[:, off:off + n])
        off += n


def norm_matmul(x, nw, w_bf16, splits, tm, emit_h=False):
    t, d = x.shape
    n = w_bf16.shape[1]
    assert sum(splits) == n and t % tm == 0
    out_shape = [jax.ShapeDtypeStruct((t, s), F32) for s in splits]
    out_specs = [pl.BlockSpec((tm, s), lambda i: (i, 0)) for s in splits]
    if emit_h:
        out_shape.insert(0, jax.ShapeDtypeStruct((t, d), BF16))
        out_specs.insert(0, pl.BlockSpec((tm, d), lambda i: (i, 0)))
    return pl.pallas_call(
        functools.partial(_norm_matmul_kernel, splits=tuple(splits), emit_h=emit_h),
        grid=(t // tm,),
        in_specs=[pl.BlockSpec((tm, d), lambda i: (i, 0)),
                  pl.BlockSpec((1, d), lambda i: (0, 0)),
                  pl.BlockSpec((d, n), lambda i: (0, 0))],
        out_specs=out_specs,
        out_shape=out_shape,
        compiler_params=_params(("parallel",)),
        name="norm_matmul",
    )(x, nw.reshape(1, d), w_bf16)


def _bias_kernel(table_ref, bucket_ref, out_ref):
    h = pl.program_id(0)
    bk = bucket_ref[...]
    acc = jnp.zeros(bk.shape, F32)
    for b in range(N_BUCKETS):
        acc = jnp.where(bk == b, table_ref[b, h], acc)
    out_ref[0] = acc


def rel_bias_rows(table, bucket):
    r, c = bucket.shape
    return pl.pallas_call(
        _bias_kernel,
        grid=(ATT_HEADS,),
        in_specs=[pl.BlockSpec(memory_space=pltpu.SMEM),
                  pl.BlockSpec((r, c), lambda h: (0, 0))],
        out_specs=pl.BlockSpec((1, r, c), lambda h: (h, 0, 0)),
        out_shape=jax.ShapeDtypeStruct((ATT_HEADS, r, c), F32),
        compiler_params=_params(("parallel",)),
        name="rel_bias",
    )(table, bucket)


def _t5_bucket(dist):
    n = jnp.maximum(dist, 0)
    max_exact = N_BUCKETS // 2
    nf = jnp.maximum(n, 1).astype(F32)
    large = max_exact + (jnp.log(nf / max_exact) / math.log(MAX_DISTANCE / max_exact)
                         * (N_BUCKETS - max_exact)).astype(jnp.int32)
    large = jnp.minimum(large, N_BUCKETS - 1)
    return jnp.where(n < max_exact, n, large)


def _half_pair(pair, rolled, which, lo):
    if which == 0:
        return jnp.where(lo, pair, 0.0), jnp.where(lo, 0.0, rolled)
    return jnp.where(lo, rolled, 0.0), jnp.where(lo, 0.0, pair)


def _swa_prompt_kernel(sink_ref, q_ref, kc_ref, kp_ref, vc_ref, vp_ref, bias_ref, o_ref):
    i = pl.program_id(1)
    blk = WINDOW
    row = lax.broadcasted_iota(jnp.int32, (blk, 2 * blk), 0)
    col = lax.broadcasted_iota(jnp.int32, (blk, 2 * blk), 1)
    dist = row + blk - col
    ok = (dist >= 0) & (dist <= WINDOW) & ((col >= blk) | (i > 0))
    lo = lax.broadcasted_iota(jnp.int32, (2 * blk, LANES), 1) < ATT_HEAD_DIM
    scale = ATT_HEAD_DIM ** -0.5
    for pair in range(ATT_KV_HEADS // 2):
        sl = slice(pair * LANES, (pair + 1) * LANES)
        kpair = jnp.concatenate([kp_ref[:, sl], kc_ref[:, sl]], axis=0)
        vpair = jnp.concatenate([vp_ref[:, sl], vc_ref[:, sl]], axis=0)
        kroll = pltpu.roll(kpair, ATT_HEAD_DIM, axis=1)
        vroll = pltpu.roll(vpair, ATT_HEAD_DIM, axis=1)
        for gi in range(2):
            g = 2 * pair + gi
            k_lo, k_hi = _half_pair(kpair, kroll, gi, lo)
            v_lo, v_hi = _half_pair(vpair, vroll, gi, lo)
            k_halves = (k_lo.astype(BF16), k_hi.astype(BF16))
            v_halves = (v_lo.astype(BF16), v_hi.astype(BF16))
            for qp in range(2):
                qsl = slice((2 * g + qp) * LANES, (2 * g + qp + 1) * LANES)
                qpair = q_ref[:, qsl].astype(BF16)
                acc = jnp.zeros((blk, LANES), F32)
                for par in range(2):
                    h = 4 * g + 2 * qp + par
                    s = _dot_nt(qpair, k_halves[par]) * scale + bias_ref[h]
                    s = jnp.where(ok, s, -jnp.inf)
                    sink = sink_ref[h]
                    m = jnp.maximum(jnp.max(s, axis=-1, keepdims=True), sink)
                    p = jnp.exp(s - m)
                    den = jnp.sum(p, axis=-1, keepdims=True) + jnp.exp(sink - m)
                    acc = acc + _dot((p / den).astype(BF16), v_halves[par])
                o_ref[:, qsl] = acc


def swa_prompt(q, k, v, bias, sinks, batch, seq):
    nb = seq // WINDOW
    cur = lambda b, i: (b * nb + i, 0)
    prev = lambda b, i: (b * nb + jnp.maximum(i - 1, 0), 0)
    kvw = ATT_KV_HEADS * ATT_HEAD_DIM
    return pl.pallas_call(
        _swa_prompt_kernel,
        grid=(batch, nb),
        in_specs=[pl.BlockSpec(memory_space=pltpu.SMEM),
                  pl.BlockSpec((WINDOW, D_MODEL), cur),
                  pl.BlockSpec((WINDOW, kvw), cur),
                  pl.BlockSpec((WINDOW, kvw), prev),
                  pl.BlockSpec((WINDOW, kvw), cur),
                  pl.BlockSpec((WINDOW, kvw), prev),
                  pl.BlockSpec((ATT_HEADS, WINDOW, 2 * WINDOW), lambda b, i: (0, 0, 0))],
        out_specs=pl.BlockSpec((WINDOW, D_MODEL), cur),
        out_shape=jax.ShapeDtypeStruct((batch * seq, D_MODEL), F32),
        compiler_params=_params(("parallel", "parallel")),
        name="swa_prompt",
    )(sinks, q, k, k, v, v, bias)


def _ssd_prompt_kernel(xbc_ref, dt_ref, cw_ref, cb_ref, dtb_ref, alog_ref, dskip_ref, e_ref,
                       y_ref, st_ref, state_scr, tail_scr):
    c = pl.program_id(1)
    last = pl.num_programs(1) - 1

    @pl.when(c == 0)
    def _():
        state_scr[...] = jnp.zeros(state_scr.shape, F32)
        tail_scr[...] = jnp.zeros(tail_scr.shape, F32)

    x = xbc_ref[...]
    tail = tail_scr[...]
    row8 = lax.broadcasted_iota(jnp.int32, tail.shape, 0)
    acc = x * cw_ref[CONV_WIDTH - 1:CONV_WIDTH, :]
    for s in range(1, CONV_WIDTH):
        xr = pltpu.roll(x, s, axis=0)
        tr = pltpu.roll(tail, s, axis=0)
        head = jnp.where(row8 < s, tr, xr[0:8])
        shifted = jnp.concatenate([head, xr[8:]], axis=0)
        acc = acc + shifted * cw_ref[CONV_WIDTH - 1 - s:CONV_WIDTH - s, :]
    tail_scr[...] = x[CHUNK - 8:CHUNK]
    u = acc + cb_ref[...]
    u = u * _sigmoid(u)
    xs = u[:, :SSD_WIDTH]
    bm = u[:, SSD_WIDTH:SSD_WIDTH + SSD_GROUPS * D_STATE]
    cm = u[:, SSD_WIDTH + SSD_GROUPS * D_STATE:]

    dtv = _softplus(dt_ref[...] + dtb_ref[...])
    a = dtv * (-jnp.exp(alog_ref[...]))
    r = lax.broadcasted_iota(jnp.int32, (CHUNK, CHUNK), 0)
    cc = lax.broadcasted_iota(jnp.int32, (CHUNK, CHUNK), 1)
    causal = r >= cc
    tril = jnp.where(causal, 1.0, 0.0).astype(BF16)
    cs = _exact_dot_left(tril, a)
    cs_t = cs.T
    dte = jnp.exp(cs[CHUNK - 1:CHUNK, :] - cs)
    ecs = jnp.exp(cs)
    e01 = e_ref[...]
    dt_x = _exact_dot(dtv, e01)
    dte_x = _exact_dot(dte, e01)
    ecs_x = _exact_dot(ecs, e01)
    xdt = xs * dt_x
    xdec = (xdt * dte_x).astype(BF16)
    xdt_b = xdt.astype(BF16)
    lo = lax.broadcasted_iota(jnp.int32, (CHUNK, LANES), 1) < SSD_HEAD_DIM
    dsk = dskip_ref[...]
    for g in range(SSD_GROUPS):
        gs = slice(g * GROUP_W, (g + 1) * GROUP_W)
        bg = bm[:, g * D_STATE:(g + 1) * D_STATE]
        cg = cm[:, g * D_STATE:(g + 1) * D_STATE].astype(BF16)
        cb = _dot_nt(cg, bg.astype(BF16))
        bgt = bg.T.astype(BF16)
        prev = state_scr[g]
        y_off = _dot(cg, prev.astype(BF16)) * ecs_x[:, gs]
        state_scr[g] = ecs_x[CHUNK - 1:CHUNK, gs] * prev + _dot(bgt, xdec[:, gs])
        for m in range(GROUP_W // LANES):
            ps = slice(g * GROUP_W + m * LANES, g * GROUP_W + (m + 1) * LANES)
            xp = xdt_b[:, ps]
            ydiag = jnp.zeros((CHUNK, LANES), F32)
            for par in range(2):
                j = g * (SSD_HEADS // SSD_GROUPS) + 2 * m + par
                diff = cs[:, j:j + 1] - cs_t[j:j + 1, :]
                ldec = jnp.exp(jnp.where(causal, diff, -jnp.inf))
                mm = (cb * ldec).astype(BF16)
                xpm = jnp.where(lo, xp, 0.0) if par == 0 else jnp.where(lo, 0.0, xp)
                ydiag = ydiag + _dot(mm, xpm.astype(BF16))
            y_ref[:, ps] = ydiag + y_off[:, m * LANES:(m + 1) * LANES] + dsk[:, ps] * xs[:, ps]

    @pl.when(c == last)
    def _():
        for g in range(SSD_GROUPS):
            st_ref[0, g * GROUP_W:(g + 1) * GROUP_W, :] = state_scr[g].T


def ssd_prompt(xbc, dt_raw, cw, cb, dtb, alog, dskip_x, e01, batch, seq):
    nc = seq // CHUNK
    cur = lambda b, c: (b * nc + c, 0)
    const = lambda b, c: (0, 0)
    return pl.pallas_call(
        _ssd_prompt_kernel,
        grid=(batch, nc),
        in_specs=[pl.BlockSpec((CHUNK, CONV_DIM), cur),
                  pl.BlockSpec((CHUNK, HEAD_PAD), cur),
                  pl.BlockSpec((CONV_WIDTH, CONV_DIM), const),
                  pl.BlockSpec((1, CONV_DIM), const),
                  pl.BlockSpec((1, HEAD_PAD), const),
                  pl.BlockSpec((1, HEAD_PAD), const),
                  pl.BlockSpec((1, SSD_WIDTH), const),
                  pl.BlockSpec((HEAD_PAD, SSD_WIDTH), const)],
        out_specs=[pl.BlockSpec((CHUNK, SSD_WIDTH), cur),
                   pl.BlockSpec((1, SSD_WIDTH, D_STATE), lambda b, c: (b, 0, 0))],
        out_shape=[jax.ShapeDtypeStruct((batch * seq, SSD_WIDTH), F32),
                   jax.ShapeDtypeStruct((batch, SSD_WIDTH, D_STATE), F32)],
        scratch_shapes=[pltpu.VMEM((SSD_GROUPS, D_STATE, GROUP_W), F32),
                        pltpu.VMEM((8, CONV_DIM), F32)],
        compiler_params=_params(("parallel", "arbitrary")),
        name="ssd_prompt",
    )(xbc, dt_raw, cw, cb, dtb, alog, dskip_x, e01)


def _swa_sample_kernel(qx_ref, ck_ref, cv_ref, kn_ref, vn_ref, bias_ref, sink_ref,
                       nk_ref, nv_ref, o_ref, *, bt):
    scale = ATT_HEAD_DIM ** -0.5
    kvw = ATT_KV_HEADS * ATT_HEAD_DIM
    rowi = lax.broadcasted_iota(jnp.int32, (WINDOW, kvw), 0)
    bias_c = bias_ref[:, 0:WINDOW]
    bias_n = bias_ref[:, WINDOW:WINDOW + 1]
    sink = sink_ref[:, 0:1]
    for bb in range(bt):
        kc = ck_ref[bb]
        vc = cv_ref[bb]
        kn = kn_ref[bb]
        vn = vn_ref[bb]
        qx = qx_ref[bb]
        s_c = _dot_nt(qx.astype(BF16), kc.astype(BF16)) * scale + bias_c
        s_n = jnp.sum(qx * kn, axis=1, keepdims=True) * scale + bias_n
        m = jnp.maximum(jnp.maximum(jnp.max(s_c, axis=1, keepdims=True), s_n), sink)
        p_c = jnp.exp(s_c - m)
        p_n = jnp.exp(s_n - m)
        den = jnp.sum(p_c, axis=1, keepdims=True) + p_n + jnp.exp(sink - m)
        o_ref[bb] = _dot((p_c / den).astype(BF16), vc.astype(BF16)) + (p_n / den) * vn
        nk_ref[bb] = jnp.where(rowi == WINDOW - 1, kn, pltpu.roll(kc, WINDOW - 1, axis=0))
        nv_ref[bb] = jnp.where(rowi == WINDOW - 1, vn, pltpu.roll(vc, WINDOW - 1, axis=0))


def swa_sample(qx, ck, cv, kn, vn, bias_s, sink_b, bt=8):
    nb = qx.shape[0]
    kvw = ATT_KV_HEADS * ATT_HEAD_DIM
    blk3 = lambda i: (i, 0, 0)
    return pl.pallas_call(
        functools.partial(_swa_sample_kernel, bt=bt),
        grid=(nb // bt,),
        in_specs=[pl.BlockSpec((bt, ATT_HEADS, kvw), blk3),
                  pl.BlockSpec((bt, WINDOW, kvw), blk3),
                  pl.BlockSpec((bt, WINDOW, kvw), blk3),
                  pl.BlockSpec((bt, 1, kvw), blk3),
                  pl.BlockSpec((bt, 1, kvw), blk3),
                  pl.BlockSpec((ATT_HEADS, 2 * WINDOW), lambda i: (0, 0)),
                  pl.BlockSpec((ATT_HEADS, LANES), lambda i: (0, 0))],
        out_specs=[pl.BlockSpec((bt, WINDOW, kvw), blk3),
                   pl.BlockSpec((bt, WINDOW, kvw), blk3),
                   pl.BlockSpec((bt, ATT_HEADS, kvw), blk3)],
        out_shape=[jax.ShapeDtypeStruct((nb, WINDOW, kvw), F32),
                   jax.ShapeDtypeStruct((nb, WINDOW, kvw), F32),
                   jax.ShapeDtypeStruct((nb, ATT_HEADS, kvw), F32)],
        compiler_params=_params(("parallel",)),
        name="swa_sample",
    )(qx, ck, cv, kn, vn, bias_s, sink_b)


def _ssd_sample_prep_kernel(xbc_ref, c0_ref, c1_ref, c2_ref, dt_ref, cw_ref, cb_ref, dtb_ref,
                            alog_ref, e_ref, dec_t_ref, xdt_t_ref, b_ref, c_ref, xs_ref):
    u = (c0_ref[...] * cw_ref[0:1, :] + c1_ref[...] * cw_ref[1:2, :]
         + c2_ref[...] * cw_ref[2:3, :] + xbc_ref[...] * cw_ref[3:4, :]) + cb_ref[...]
    u = u * _sigmoid(u)
    xs = u[:, :SSD_WIDTH]
    b_ref[...] = u[:, SSD_WIDTH:SSD_WIDTH + SSD_GROUPS * D_STATE]
    c_ref[...] = u[:, SSD_WIDTH + SSD_GROUPS * D_STATE:]
    xs_ref[...] = xs
    dtv = _softplus(dt_ref[...] + dtb_ref[...])
    dec = jnp.exp(dtv * (-jnp.exp(alog_ref[...])))
    e01 = e_ref[...]
    dec_t_ref[...] = _exact_dot(dec, e01).T
    xdt_t_ref[...] = (xs * _exact_dot(dtv, e01)).T


def ssd_sample_prep(xbc, c0, c1, c2, dt_raw, cw, cb, dtb, alog, e01):
    nb = xbc.shape[0]
    args = (xbc, c0, c1, c2, dt_raw, cw, cb, dtb, alog, e01)
    full = lambda a: pl.BlockSpec(a.shape, lambda i: (0,) * a.ndim)
    out_shape = [jax.ShapeDtypeStruct((SSD_WIDTH, nb), F32),
                 jax.ShapeDtypeStruct((SSD_WIDTH, nb), F32),
                 jax.ShapeDtypeStruct((nb, SSD_GROUPS * D_STATE), F32),
                 jax.ShapeDtypeStruct((nb, SSD_GROUPS * D_STATE), F32),
                 jax.ShapeDtypeStruct((nb, SSD_WIDTH), F32)]
    return pl.pallas_call(
        _ssd_sample_prep_kernel,
        grid=(1,),
        in_specs=[full(a) for a in args],
        out_specs=[full(s) for s in out_shape],
        out_shape=out_shape,
        compiler_params=_params(("arbitrary",)),
        name="ssd_sample_prep",
    )(*args)


def _ssd_sample_step_kernel(st_ref, dec_t_ref, xdt_t_ref, b_ref, c_ref, xs_ref, dskip_ref,
                            ns_ref, y_ref, yt_scr):
    b = pl.program_id(0)
    nb = dec_t_ref.shape[1]

    @pl.when(b == 0)
    def _():
        yt_scr[...] = jnp.zeros(yt_scr.shape, F32)

    pick = jnp.where(lax.broadcasted_iota(jnp.int32, (nb, D_STATE), 0) == b,
                     1.0, 0.0).astype(BF16)
    dcol = _exact_dot(dec_t_ref[...], pick)
    xcol = _exact_dot(xdt_t_ref[...], pick)
    brow = b_ref[pl.ds(b, 1), :]
    crow = c_ref[pl.ds(b, 1), :]
    lane = lax.broadcasted_iota(jnp.int32, (GROUP_W, nb), 1)
    for g in range(SSD_GROUPS):
        gs = slice(g * GROUP_W, (g + 1) * GROUP_W)
        ns = slice(g * D_STATE, (g + 1) * D_STATE)
        hn = dcol[gs] * st_ref[0, gs, :] + xcol[gs] * brow[:, ns]
        ns_ref[0, gs, :] = hn
        ycol = jnp.sum(hn * crow[:, ns], axis=1, keepdims=True)
        yt_scr[gs, :] = jnp.where(lane == b, ycol, yt_scr[gs, :])

    @pl.when(b == nb - 1)
    def _():
        y_ref[...] = yt_scr[...].T + dskip_ref[...] * xs_ref[...]


def ssd_sample_step(state, dec_t, xdt_t, bmat, cmat, xs, dskip_x):
    nb = state.shape[0]
    assert nb == D_STATE
    const2 = lambda i: (0, 0)
    return pl.pallas_call(
        _ssd_sample_step_kernel,
        grid=(nb,),
        in_specs=[pl.BlockSpec((1, SSD_WIDTH, D_STATE), lambda i: (i, 0, 0)),
                  pl.BlockSpec((SSD_WIDTH, nb), const2),
                  pl.BlockSpec((SSD_WIDTH, nb), const2),
                  pl.BlockSpec((nb, SSD_GROUPS * D_STATE), const2),
                  pl.BlockSpec((nb, SSD_GROUPS * D_STATE), const2),
                  pl.BlockSpec((nb, SSD_WIDTH), const2),
                  pl.BlockSpec((1, SSD_WIDTH), const2)],
        out_specs=[pl.BlockSpec((1, SSD_WIDTH, D_STATE), lambda i: (i, 0, 0)),
                   pl.BlockSpec((nb, SSD_WIDTH), const2)],
        out_shape=[jax.ShapeDtypeStruct((nb, SSD_WIDTH, D_STATE), F32),
                   jax.ShapeDtypeStruct((nb, SSD_WIDTH), F32)],
        scratch_shapes=[pltpu.VMEM((SSD_WIDTH, nb), F32)],
        compiler_params=_params(("arbitrary",)),
        name="ssd_sample_step",
    )(state, dec_t, xdt_t, bmat, cmat, xs, dskip_x)


def _merge_kernel(x_ref, o_ref, y_ref, z_ref, anw_ref, snw_ref, w1_ref, w2_ref, out_ref):
    ya = _rms(o_ref[...], anw_ref[...]).astype(BF16)
    z = z_ref[...]
    ys = _rms(y_ref[...] * (z * _sigmoid(z)), snw_ref[...]).astype(BF16)
    out_ref[...] = x_ref[...] + _dot(ya, w1_ref[...]) + _dot(ys, w2_ref[...])


def merge(x, o_att, y_ssd, z, anw, snw, w1, w2, tm):
    t, d = x.shape
    row = pl.BlockSpec((tm, d), lambda i: (i, 0))
    vec = pl.BlockSpec((1, d), lambda i: (0, 0))
    mat = pl.BlockSpec((d, d), lambda i: (0, 0))
    return pl.pallas_call(
        _merge_kernel,
        grid=(t // tm,),
        in_specs=[row, row, row, row, vec, vec, mat, mat],
        out_specs=row,
        out_shape=jax.ShapeDtypeStruct((t, d), F32),
        compiler_params=_params(("parallel",)),
        name="merge",
    )(x, o_att, y_ssd, z, anw.reshape(1, d), snw.reshape(1, d), w1, w2)


def _mem_prompt_kernel(x_ref, nw_ref, wq_ref, mk_ref, mv_ref, wo_ref, out_ref):
    x = x_ref[...]
    q = _dot(_rms(x, nw_ref[...]).astype(BF16), wq_ref[...])
    scale = MEM_HEAD_DIM ** -0.5
    outs = []
    for h in range(MEM_HEADS):
        hs = slice(h * MEM_HEAD_DIM, (h + 1) * MEM_HEAD_DIM)
        s = _dot_nt(q[:, hs].astype(BF16), mk_ref[:, hs]) * scale
        p = jnp.exp(s - jnp.max(s, axis=-1, keepdims=True))
        p = p / jnp.sum(p, axis=-1, keepdims=True)
        outs.append(_dot(p.astype(BF16), mv_ref[:, hs]).astype(BF16))
    out_ref[...] = x + _dot(jnp.concatenate(outs, axis=1), wo_ref[...])


def mem_attn_prompt(x, nw, wq, mk, mv, wo, seq, tm):
    t, d = x.shape
    per = seq // tm
    row = pl.BlockSpec((tm, d), lambda i: (i, 0))
    mat = pl.BlockSpec((d, d), lambda i: (0, 0))
    mem = pl.BlockSpec((N_MEM, d), lambda i: (i // per, 0))
    return pl.pallas_call(
        _mem_prompt_kernel,
        grid=(t // tm,),
        in_specs=[row, pl.BlockSpec((1, d), lambda i: (0, 0)), mat, mem, mem, mat],
        out_specs=row,
        out_shape=jax.ShapeDtypeStruct((t, d), F32),
        compiler_params=_params(("parallel",)),
        name="mem_prompt",
    )(x, nw.reshape(1, d), wq, mk, mv, wo)


def _mem_sample_kernel(q_ref, k_ref, v_ref, o_ref):
    q4 = q_ref[0]
    q8 = jnp.concatenate([q4, jnp.zeros_like(q4)], axis=0).astype(BF16)
    rows = N_MEM * MEM_HEADS
    k2 = k_ref[0].reshape(rows, MEM_HEAD_DIM).astype(BF16)
    v2 = v_ref[0].reshape(rows, MEM_HEAD_DIM).astype(BF16)
    s = _dot_nt(q8, k2) * (MEM_HEAD_DIM ** -0.5)
    qh = lax.broadcasted_iota(jnp.int32, s.shape, 0)
    kh = lax.broadcasted_iota(jnp.int32, s.shape, 1) % MEM_HEADS
    s = jnp.where(kh == qh, s, -jnp.inf)
    live = qh < MEM_HEADS
    p = jnp.exp(s - jnp.where(live, jnp.max(s, axis=1, keepdims=True), 0.0))
    p = p / jnp.where(live, jnp.sum(p, axis=1, keepdims=True), 1.0)
    o_ref[0] = _dot(p.astype(BF16), v2)[0:MEM_HEADS]


def mem_sample(q, ck, cv):
    nb, d = q.shape
    blk = pl.BlockSpec((1, N_MEM, MEM_HEADS, MEM_HEAD_DIM), lambda i: (i, 0, 0, 0))
    vec = pl.BlockSpec((1, MEM_HEADS, MEM_HEAD_DIM), lambda i: (i, 0, 0))
    return pl.pallas_call(
        _mem_sample_kernel,
        grid=(nb,),
        in_specs=[vec, blk, blk],
        out_specs=vec,
        out_shape=jax.ShapeDtypeStruct((nb, MEM_HEADS, MEM_HEAD_DIM), F32),
        compiler_params=_params(("parallel",)),
        name="mem_sample",
    )(q.reshape(nb, MEM_HEADS, MEM_HEAD_DIM), ck, cv).reshape(nb, d)


def _matmul_res_kernel(a_ref, w_ref, r_ref, out_ref):
    out_ref[...] = r_ref[...] + _dot(a_ref[...].astype(BF16), w_ref[...])


def matmul_res(a, w, res, tm):
    t, d = res.shape
    kdim = a.shape[1]
    return pl.pallas_call(
        _matmul_res_kernel,
        grid=(t // tm,),
        in_specs=[pl.BlockSpec((tm, kdim), lambda i: (i, 0)),
                  pl.BlockSpec((kdim, d), lambda i: (0, 0)),
                  pl.BlockSpec((tm, d), lambda i: (i, 0))],
        out_specs=pl.BlockSpec((tm, d), lambda i: (i, 0)),
        out_shape=jax.ShapeDtypeStruct((t, d), F32),
        compiler_params=_params(("parallel",)),
        name="matmul_res",
    )(a, w, res)


def _topk_rows(work, k):
    rows = work.shape[0]
    iota = lax.broadcasted_iota(jnp.int32, work.shape, 0)
    rank = jnp.full(work.shape, k, jnp.int32)
    vals = []
    for r in range(k):
        m = jnp.max(work, axis=0, keepdims=True)
        idx = jnp.min(jnp.where(work == m, iota, rows), axis=0, keepdims=True)
        hit = iota == idx
        rank = jnp.where(hit, r, rank)
        work = jnp.where(hit, -jnp.inf, work)
        vals.append(m)
    return jnp.concatenate(vals, axis=0), rank


def _peer_select_kernel(pq_ref, keys_ref, seli_ref, selj_ref):
    k = PEER_TOPK
    for h in range(PEER_HEADS):
        subs, tops, ranks = [], [], []
        for p in range(2):
            hp = 2 * h + p
            qhp = pq_ref[:, hp * PEER_HALF:(hp + 1) * PEER_HALF].astype(BF16)
            sub = _dot_nt(keys_ref[hp], qhp)
            top, rank = _topk_rows(sub, k)
            subs.append(sub)
            tops.append(top)
            ranks.append(rank)
        a, b = tops
        lo4 = lax.broadcasted_iota(jnp.int32, (8, a.shape[1]), 0) < 4
        b8 = b[0:8]
        b44 = jnp.where(lo4, b8, pltpu.roll(b8, 4, axis=0))
        cand = jnp.concatenate(
            [a[0:1] + b8, a[0:1] + b[8:16], a[1:2] + b8, a[2:3] + b8, a[3:4] + b8,
             jnp.where(lo4, a[4:5], a[5:6]) + b44, jnp.where(lo4, a[6:7], a[7:8]) + b44,
             a[8:16] + b[0:1]], axis=0)
        _, crank = _topk_rows(cand, k)
        chosen = crank < k
        cmax = a[0:1, :] + b[0:1, :]
        z = jnp.sum(jnp.where(chosen, jnp.exp(cand - cmax), 0.0), axis=0, keepdims=True)
        ch = [jnp.where(chosen[8 * i:8 * i + 8], 1.0, 0.0) for i in range(8)]
        cnt = lambda v: jnp.sum(v, axis=0, keepdims=True)
        n = [cnt(ch[0]) + cnt(ch[1]), cnt(ch[2]), cnt(ch[3]), cnt(ch[4]),
             cnt(jnp.where(lo4, ch[5], 0.0)), cnt(jnp.where(lo4, 0.0, ch[5])),
             cnt(jnp.where(lo4, ch[6], 0.0)), cnt(jnp.where(lo4, 0.0, ch[6]))]
        n += [ch[7][i:i + 1] for i in range(8)]
        lim0 = jnp.zeros(ranks[0].shape, F32)
        for r in range(k):
            lim0 = jnp.where(ranks[0] == r, n[r], lim0)
        seli_ref[h, 0] = jnp.exp(subs[0] - a[0:1, :]) / z
        seli_ref[h, 1] = lim0
        selj_ref[h, 0] = pltpu.bitcast(jnp.exp(subs[1] - b[0:1, :]).astype(BF16), jnp.uint32)
        selj_ref[h, 1] = pltpu.bitcast(ranks[1].astype(F32).astype(BF16), jnp.uint32)


def peer_select(pq, keys_bf16, tm):
    t = pq.shape[0]
    spec = lambda rows: pl.BlockSpec((PEER_HEADS, 2, rows, tm), lambda i: (0, 0, 0, i))
    return pl.pallas_call(
        _peer_select_kernel,
        grid=(t // tm,),
        in_specs=[pl.BlockSpec((tm, pq.shape[1]), lambda i: (i, 0)),
                  pl.BlockSpec(keys_bf16.shape, lambda i: (0, 0, 0))],
        out_specs=[spec(N_KEYS), spec(N_KEYS // 2)],
        out_shape=[jax.ShapeDtypeStruct((PEER_HEADS, 2, N_KEYS, t), F32),
                   jax.ShapeDtypeStruct((PEER_HEADS, 2, N_KEYS // 2, t), jnp.uint32)],
        compiler_params=_params(("parallel",)),
        name="peer_select",
    )(pq, keys_bf16)


def _gelu(x):
    return 0.5 * x * (1.0 + lax.erf(x * np.float32(math.sqrt(0.5))))


def _peer_dense_kernel(xn_ref, x_ref, u_ref, vt_ref, seli_ref, selj_ref, fnw_ref, y_ref,
                       acc_scr, act_scr, w_scr, *, tm, ib, sub):
    e = pl.program_id(1)
    ne = pl.num_programs(1)

    @pl.when(e == 0)
    def _():
        acc_scr[...] = jnp.zeros(acc_scr.shape, F32)

    i8 = pl.ds(pl.multiple_of(e * ib, 8), ib)
    xn = xn_ref[...]
    nsub = ib // sub
    sub_rows = lambda k: slice(k * sub * N_KEYS, (k + 1) * sub * N_KEYS)

    def activations(k):
        act_scr[sub_rows(k), :] = _dot_nt(u_ref[sub_rows(k), :], xn)

    activations(0)
    for k in range(nsub):
        ks = sub_rows(k)
        if k + 1 < nsub:
            activations(k + 1)
        for lt in range(tm // LANES):
            ls = slice(lt * LANES, (lt + 1) * LANES)
            for ii in range(k * sub, (k + 1) * sub):
                rs = slice(ii * N_KEYS, (ii + 1) * N_KEYS)
                g = jnp.zeros((N_KEYS, LANES), BF16)
                for h in range(PEER_HEADS):
                    row = lambda c: jnp.broadcast_to(seli_ref[h, c, i8, ls][ii:ii + 1, :],
                                                     (N_KEYS, LANES)).astype(BF16)
                    e1 = pltpu.bitcast(selj_ref[h, 0, :, ls], BF16)
                    rank1 = pltpu.bitcast(selj_ref[h, 1, :, ls], BF16)
                    g = g + jnp.where(rank1 < row(1), e1, jnp.zeros_like(e1)) * row(0)
                w_scr[rs, ls] = g * _gelu(act_scr[rs, ls]).astype(BF16)
        acc_scr[...] += _dot(vt_ref[:, ks], w_scr[ks, :])

    @pl.when(e == ne - 1)
    def _():
        y_ref[...] = _rms(x_ref[...] + acc_scr[...].T, fnw_ref[...])


def peer_dense(xn, x, u_bf16, vt_bf16, seli, selj, fnw, tm, ib=8, sub=2):
    t, d = x.shape
    eb = ib * N_KEYS
    sel_spec = lambda rows: pl.BlockSpec((PEER_HEADS, 2, rows, tm), lambda i, e: (0, 0, 0, i))
    return pl.pallas_call(
        functools.partial(_peer_dense_kernel, tm=tm, ib=ib, sub=sub),
        grid=(t // tm, N_EXPERTS // eb),
        in_specs=[pl.BlockSpec((tm, d), lambda i, e: (i, 0)),
                  pl.BlockSpec((tm, d), lambda i, e: (i, 0)),
                  pl.BlockSpec((eb, d), lambda i, e: (e, 0)),
                  pl.BlockSpec((d, eb), lambda i, e: (0, e)),
                  sel_spec(N_KEYS), sel_spec(N_KEYS // 2),
                  pl.BlockSpec((1, d), lambda i, e: (0, 0))],
        out_specs=pl.BlockSpec((tm, d), lambda i, e: (i, 0)),
        out_shape=jax.ShapeDtypeStruct((t, d), F32),
        scratch_shapes=[pltpu.VMEM((d, tm), F32),
                        pltpu.VMEM((eb, tm), F32),
                        pltpu.VMEM((eb, tm), BF16)],
        compiler_params=_params(("parallel", "arbitrary")),
        name="peer_dense",
    )(xn, x, u_bf16, vt_bf16, seli, selj, fnw.reshape(1, d))


def _tail(x, mk, mv, prm, seq, tm, mem_is_shared, tm_peer):
    if mem_is_shared:
        x = mem_attn_prompt(x, prm["norm_mem_w"], prm["w_mq"], mk, mv, prm["w_mo"], seq, tm)
    else:
        (q,) = norm_matmul(x, prm["norm_mem_w"], prm["w_mq"], (D_MODEL,), tm)
        x = matmul_res(mem_sample(q, mk, mv), prm["w_mo"], x, tm)
    xn, pq = norm_matmul(x, prm["norm_ffn_w"], prm["peer_wq"], (prm["peer_wq"].shape[1],), tm,
                         emit_h=True)
    seli, selj = peer_select(pq, prm["peer_keys"], min(tm, 256))
    return peer_dense(xn, x, prm["peer_u"], prm["peer_vt"], seli, selj, prm["final_norm_w"],
                      tm_peer)


def kernel(x_prompt, x_sample, mem_prompt, cache_swa_k, cache_swa_v, state_ssm, state_conv, cache_mem_k, cache_mem_v, norm_mix_w, w_in, attn_sinks, rel_bias, attn_norm_w, conv_w, conv_b, dt_bias, a_log, d_skip, ssd_norm_w, w_out, norm_mem_w, mem_norm_w, w_mq, w_mk, w_mv, w_mo, norm_ffn_w, peer_wq, peer_keys, peer_u, peer_v, final_norm_w):
    assert w_in.shape[0] == 1, "single-layer step"
    batch, seq, d = x_prompt.shape
    nb = x_sample.shape[0]
    kvw = ATT_KV_HEADS * ATT_HEAD_DIM
    tm = 512

    n_main = D_MODEL + 2 * kvw + SSD_WIDTH + CONV_DIM
    w_in_p = jnp.concatenate(
        [w_in[0], jnp.zeros((d, HEAD_PAD - SSD_HEADS), F32)], axis=1).astype(BF16)
    in_splits = (D_MODEL, kvw, kvw, SSD_WIDTH, CONV_DIM, HEAD_PAD)
    assert n_main + SSD_HEADS == w_in.shape[2]
    pad_h = lambda v: jnp.pad(v.reshape(1, SSD_HEADS), ((0, 0), (0, HEAD_PAD - SSD_HEADS)))
    dtb, alog = pad_h(dt_bias[0]), pad_h(a_log[0])
    dskip_x = jnp.repeat(d_skip[0], SSD_HEAD_DIM).reshape(1, SSD_WIDTH)
    e01 = (jnp.arange(HEAD_PAD)[:, None] == jnp.arange(SSD_WIDTH)[None, :] // SSD_HEAD_DIM
           ).astype(BF16)
    cw, cb = conv_w[0], conv_b[0].reshape(1, CONV_DIM)
    w_out1 = w_out[0, :D_MODEL].astype(BF16)
    w_out2 = w_out[0, D_MODEL:].astype(BF16)
    prm = dict(norm_mem_w=norm_mem_w[0], w_mq=w_mq[0].astype(BF16), w_mo=w_mo[0].astype(BF16),
               norm_ffn_w=norm_ffn_w[0], peer_wq=peer_wq[0].astype(BF16),
               peer_keys=peer_keys[0].reshape(2 * PEER_HEADS, N_KEYS, PEER_HALF).astype(BF16),
               peer_u=peer_u[0].astype(BF16), peer_vt=peer_v[0].astype(BF16).T,
               final_norm_w=final_norm_w)

    qi = jnp.arange(WINDOW)[:, None] + WINDOW
    bias_p = rel_bias_rows(rel_bias, _t5_bucket(qi - jnp.arange(2 * WINDOW)[None, :]))
    dist_s = jnp.broadcast_to(jnp.maximum(WINDOW - jnp.arange(2 * WINDOW), 0)[None, :],
                              (8, 2 * WINDOW))
    bias_s = rel_bias_rows(rel_bias, _t5_bucket(dist_s))[:, 0, :]
    sinks = attn_sinks[0]

    xp = x_prompt.reshape(batch * seq, d)
    q, k, v, z, xbc, dt_raw = norm_matmul(xp, norm_mix_w[0], w_in_p, in_splits, tm)
    o_att = swa_prompt(q, k, v, bias_p, sinks, batch, seq)
    y_ssd, p_state = ssd_prompt(xbc, dt_raw, cw, cb, dtb, alog, dskip_x, e01, batch, seq)
    x1 = merge(xp, o_att, y_ssd, z, attn_norm_w[0], ssd_norm_w[0], w_out1, w_out2, tm)
    mkv_w = jnp.concatenate([w_mk[0], w_mv[0]], axis=1).astype(BF16)
    mk, mv = norm_matmul(mem_prompt.reshape(batch * N_MEM, d), mem_norm_w[0], mkv_w, (d, d), tm)
    y_p = _tail(x1, mk.astype(BF16), mv.astype(BF16), prm, seq, tm, True, tm)

    xs_in = x_sample.reshape(nb, d)
    tms = nb
    qs, ks, vs, zs, xbcs, dts = norm_matmul(xs_in, norm_mix_w[0], w_in_p, in_splits, tms)
    own = (jnp.arange(kvw)[None, :] // ATT_HEAD_DIM) == (jnp.arange(ATT_HEADS)[:, None] // (ATT_HEADS // ATT_KV_HEADS))
    qx = jnp.where(own[None], jnp.tile(qs.reshape(nb, ATT_HEADS, ATT_HEAD_DIM), (1, 1, ATT_KV_HEADS)), 0.0)
    ck = cache_swa_k[0].reshape(nb, WINDOW, kvw)
    cv = cache_swa_v[0].reshape(nb, WINDOW, kvw)
    sink_b = jnp.broadcast_to(sinks[:, None], (ATT_HEADS, LANES))
    nk, nv, ox = swa_sample(qx, ck, cv, ks.reshape(nb, 1, kvw), vs.reshape(nb, 1, kvw), bias_s, sink_b)
    ox5 = ox.reshape(nb, ATT_KV_HEADS, ATT_HEADS // ATT_KV_HEADS, ATT_KV_HEADS, ATT_HEAD_DIM)
    o_att_s = jnp.stack([ox5[:, g, :, g, :] for g in range(ATT_KV_HEADS)], axis=1).reshape(nb, d)
    sc = state_conv[0]
    dec_t, xdt_t, bmat, cmat, xs_s = ssd_sample_prep(xbcs, sc[:, 0], sc[:, 1], sc[:, 2], dts, cw, cb, dtb, alog, e01)
    new_state, y_s = ssd_sample_step(state_ssm[0].reshape(nb, SSD_WIDTH, D_STATE), dec_t, xdt_t, bmat, cmat, xs_s, dskip_x)
    x1s = merge(xs_in, o_att_s, y_s, zs, attn_norm_w[0], ssd_norm_w[0], w_out1, w_out2, tms)
    y_s_out = _tail(x1s, cache_mem_k[0], cache_mem_v[0], prm, seq, tms, False, tms)

    k4 = k.reshape(batch, seq, ATT_KV_HEADS, ATT_HEAD_DIM)
    v4 = v.reshape(batch, seq, ATT_KV_HEADS, ATT_HEAD_DIM)
    xbc3 = xbc.reshape(batch, seq, CONV_DIM)
    return (y_p.reshape(batch, seq, d),
            y_s_out.reshape(nb, 1, d),
            k4[None, :, seq - WINDOW:],
            v4[None, :, seq - WINDOW:],
            p_state.reshape(1, batch, SSD_HEADS, SSD_HEAD_DIM, D_STATE),
            xbc3[None, :, seq - (CONV_WIDTH - 1):],
            mk.reshape(1, batch, N_MEM, MEM_HEADS, MEM_HEAD_DIM),
            mv.reshape(1, batch, N_MEM, MEM_HEADS, MEM_HEAD_DIM),
            nk.reshape(1, nb, WINDOW, ATT_KV_HEADS, ATT_HEAD_DIM),
            nv.reshape(1, nb, WINDOW, ATT_KV_HEADS, ATT_HEAD_DIM),
            new_state.reshape(1, nb, SSD_HEADS, SSD_HEAD_DIM, D_STATE),
            jnp.concatenate([sc[:, 1:], xbcs[:, None, :]], axis=1)[None])
```

```python
import functools
import math

import numpy as np
import jax
import jax.numpy as jnp
from jax import lax
from jax.experimental import pallas as pl
from jax.experimental.pallas import tpu as pltpu

F32 = jnp.float32
BF16 = jnp.bfloat16
EPS = 1e-6

D_MODEL = 1024
ATT_HEADS = 16
ATT_KV_HEADS = 4
ATT_HEAD_DIM = 64
WINDOW = 128
N_BUCKETS = 32
MAX_DISTANCE = 128
SSD_HEADS = 16
SSD_HEAD_DIM = 64
SSD_WIDTH = SSD_HEADS * SSD_HEAD_DIM
SSD_GROUPS = 2
D_STATE = 128
CONV_WIDTH = 4
CONV_DIM = SSD_WIDTH + 2 * SSD_GROUPS * D_STATE
CHUNK = 128
N_MEM = 256
MEM_HEADS = 4
MEM_HEAD_DIM = D_MODEL // MEM_HEADS
PEER_HEADS = 8
N_KEYS = 128
N_EXPERTS = N_KEYS * N_KEYS
PEER_TOPK = 16
PEER_HALF = 128

LANES = 128
HEAD_PAD = 128
GROUP_W = SSD_WIDTH // SSD_GROUPS
VMEM_LIMIT = 56 * 1024 * 1024


def _params(sem):
    return pltpu.CompilerParams(dimension_semantics=sem, vmem_limit_bytes=VMEM_LIMIT)


def _dot(a, b):
    return jnp.dot(a, b, preferred_element_type=F32)


def _dot_nt(a, b):
    return lax.dot_general(a, b, (((1,), (1,)), ((), ())), preferred_element_type=F32)


def _rms(x, w):
    var = jnp.mean(x * x, axis=-1, keepdims=True)
    return x * lax.rsqrt(var + EPS) * w


def _sigmoid(x):
    return 1.0 / (1.0 + jnp.exp(-x))


def _softplus(x):
    return jnp.maximum(x, 0.0) + jnp.log1p(jnp.exp(-jnp.abs(x)))


def _split3(v):
    v1 = v.astype(BF16)
    r1 = v - v1.astype(F32)
    v2 = r1.astype(BF16)
    r2 = r1 - v2.astype(F32)
    return v1, v2, r2.astype(BF16)


def _exact_dot(v, m01):
    v1, v2, v3 = _split3(v)
    return _dot(v1, m01) + _dot(v2, m01) + _dot(v3, m01)


def _exact_dot_left(m01, v):
    v1, v2, v3 = _split3(v)
    return _dot(m01, v1) + _dot(m01, v2) + _dot(m01, v3)


def _norm_matmul_kernel(x_ref, nw_ref, w_ref, *out_refs, splits, emit_ht):
    hf = _rms(x_ref[...], nw_ref[...])
    h = hf.astype(BF16)
    refs = list(out_refs)
    if emit_ht:
        refs.pop(0)[...] = pltpu.bitcast(hf.T.astype(BF16), jnp.uint32)
    off = 0
    for o_ref, n in zip(refs, splits):
        o_ref[...] = _dot(h, w_ref[:, off:off + n])
        off += n


def norm_matmul(x, nw, w_bf16, splits, tm, emit_ht=False):
    t, d = x.shape
    n = w_bf16.shape[1]
    assert sum(splits) == n and t % tm == 0
    out_shape = [jax.ShapeDtypeStruct((t, s), F32) for s in splits]
    out_specs = [pl.BlockSpec((tm, s), lambda i: (i, 0)) for s in splits]
    if emit_ht:
        out_shape.insert(0, jax.ShapeDtypeStruct((d // 2, t), jnp.uint32))
        out_specs.insert(0, pl.BlockSpec((d // 2, tm), lambda i: (0, i)))
    return pl.pallas_call(
        functools.partial(_norm_matmul_kernel, splits=tuple(splits), emit_ht=emit_ht),
        grid=(t // tm,),
        in_specs=[pl.BlockSpec((tm, d), lambda i: (i, 0)),
                  pl.BlockSpec((1, d), lambda i: (0, 0)),
                  pl.BlockSpec((d, n), lambda i: (0, 0))],
        out_specs=out_specs,
        out_shape=out_shape,
        compiler_params=_params(("parallel",)),
        name="norm_matmul",
    )(x, nw.reshape(1, d), w_bf16)


def _bias_kernel(table_ref, bucket_ref, out_ref):
    h = pl.program_id(0)
    bk = bucket_ref[...]
    acc = jnp.zeros(bk.shape, F32)
    for b in range(N_BUCKETS):
        acc = jnp.where(bk == b, table_ref[b, h], acc)
    out_ref[0] = acc


def rel_bias_rows(table, bucket):
    r, c = bucket.shape
    return pl.pallas_call(
        _bias_kernel,
        grid=(ATT_HEADS,),
        in_specs=[pl.BlockSpec(memory_space=pltpu.SMEM),
                  pl.BlockSpec((r, c), lambda h: (0, 0))],
        out_specs=pl.BlockSpec((1, r, c), lambda h: (h, 0, 0)),
        out_shape=jax.ShapeDtypeStruct((ATT_HEADS, r, c), F32),
        compiler_params=_params(("parallel",)),
        name="rel_bias",
    )(table, bucket)


def _t5_bucket(dist):
    n = jnp.maximum(dist, 0)
    max_exact = N_BUCKETS // 2
    nf = jnp.maximum(n, 1).astype(F32)
    large = max_exact + (jnp.log(nf / max_exact) / math.log(MAX_DISTANCE / max_exact)
                         * (N_BUCKETS - max_exact)).astype(jnp.int32)
    large = jnp.minimum(large, N_BUCKETS - 1)
    return jnp.where(n < max_exact, n, large)


def _half_pair(pair, rolled, which, lo):
    if which == 0:
        return jnp.where(lo, pair, 0.0), jnp.where(lo, 0.0, rolled)
    return jnp.where(lo, rolled, 0.0), jnp.where(lo, 0.0, pair)


def _swa_prompt_kernel(sink_ref, q_ref, kc_ref, kp_ref, vc_ref, vp_ref, bias_ref, o_ref):
    i = pl.program_id(1)
    blk = WINDOW
    row = lax.broadcasted_iota(jnp.int32, (blk, 2 * blk), 0)
    col = lax.broadcasted_iota(jnp.int32, (blk, 2 * blk), 1)
    dist = row + blk - col
    ok = (dist >= 0) & (dist <= WINDOW) & ((col >= blk) | (i > 0))
    lo = lax.broadcasted_iota(jnp.int32, (2 * blk, LANES), 1) < ATT_HEAD_DIM
    scale = ATT_HEAD_DIM ** -0.5
    for pair in range(ATT_KV_HEADS // 2):
        sl = slice(pair * LANES, (pair + 1) * LANES)
        kpair = jnp.concatenate([kp_ref[:, sl], kc_ref[:, sl]], axis=0)
        vpair = jnp.concatenate([vp_ref[:, sl], vc_ref[:, sl]], axis=0)
        kroll = pltpu.roll(kpair, ATT_HEAD_DIM, axis=1)
        vroll = pltpu.roll(vpair, ATT_HEAD_DIM, axis=1)
        for gi in range(2):
            g = 2 * pair + gi
            k_lo, k_hi = _half_pair(kpair, kroll, gi, lo)
            v_lo, v_hi = _half_pair(vpair, vroll, gi, lo)
            k_halves = (k_lo.astype(BF16), k_hi.astype(BF16))
            v_halves = (v_lo.astype(BF16), v_hi.astype(BF16))
            for qp in range(2):
                qsl = slice((2 * g + qp) * LANES, (2 * g + qp + 1) * LANES)
                qpair = q_ref[:, qsl].astype(BF16)
                acc = jnp.zeros((blk, LANES), F32)
                for par in range(2):
                    h = 4 * g + 2 * qp + par
                    s = _dot_nt(qpair, k_halves[par]) * scale + bias_ref[h]
                    s = jnp.where(ok, s, -jnp.inf)
                    sink = sink_ref[h]
                    m = jnp.maximum(jnp.max(s, axis=-1, keepdims=True), sink)
                    p = jnp.exp(s - m)
                    den = jnp.sum(p, axis=-1, keepdims=True) + jnp.exp(sink - m)
                    acc = acc + _dot((p / den).astype(BF16), v_halves[par])
                o_ref[:, qsl] = acc


def swa_prompt(q, k, v, bias, sinks, batch, seq):
    nb = seq // WINDOW
    cur = lambda b, i: (b * nb + i, 0)
    prev = lambda b, i: (b * nb + jnp.maximum(i - 1, 0), 0)
    kvw = ATT_KV_HEADS * ATT_HEAD_DIM
    return pl.pallas_call(
        _swa_prompt_kernel,
        grid=(batch, nb),
        in_specs=[pl.BlockSpec(memory_space=pltpu.SMEM),
                  pl.BlockSpec((WINDOW, D_MODEL), cur),
                  pl.BlockSpec((WINDOW, kvw), cur),
                  pl.BlockSpec((WINDOW, kvw), prev),
                  pl.BlockSpec((WINDOW, kvw), cur),
                  pl.BlockSpec((WINDOW, kvw), prev),
                  pl.BlockSpec((ATT_HEADS, WINDOW, 2 * WINDOW), lambda b, i: (0, 0, 0))],
        out_specs=pl.BlockSpec((WINDOW, D_MODEL), cur),
        out_shape=jax.ShapeDtypeStruct((batch * seq, D_MODEL), F32),
        compiler_params=_params(("parallel", "parallel")),
        name="swa_prompt",
    )(sinks, q, k, k, v, v, bias)


def _ssd_prompt_kernel(xbc_ref, dt_ref, cw_ref, cb_ref, dtb_ref, alog_ref, dskip_ref, e_ref,
                       y_ref, st_ref, state_scr, tail_scr):
    c = pl.program_id(1)
    last = pl.num_programs(1) - 1

    @pl.when(c == 0)
    def _():
        state_scr[...] = jnp.zeros(state_scr.shape, F32)
        tail_scr[...] = jnp.zeros(tail_scr.shape, F32)

    x = xbc_ref[...]
    tail = tail_scr[...]
    row8 = lax.broadcasted_iota(jnp.int32, tail.shape, 0)
    acc = x * cw_ref[CONV_WIDTH - 1:CONV_WIDTH, :]
    for s in range(1, CONV_WIDTH):
        xr = pltpu.roll(x, s, axis=0)
        tr = pltpu.roll(tail, s, axis=0)
        head = jnp.where(row8 < s, tr, xr[0:8])
        shifted = jnp.concatenate([head, xr[8:]], axis=0)
        acc = acc + shifted * cw_ref[CONV_WIDTH - 1 - s:CONV_WIDTH - s, :]
    tail_scr[...] = x[CHUNK - 8:CHUNK]
    u = acc + cb_ref[...]
    u = u * _sigmoid(u)
    xs = u[:, :SSD_WIDTH]
    bm = u[:, SSD_WIDTH:SSD_WIDTH + SSD_GROUPS * D_STATE]
    cm = u[:, SSD_WIDTH + SSD_GROUPS * D_STATE:]

    dtv = _softplus(dt_ref[...] + dtb_ref[...])
    a = dtv * (-jnp.exp(alog_ref[...]))
    r = lax.broadcasted_iota(jnp.int32, (CHUNK, CHUNK), 0)
    cc = lax.broadcasted_iota(jnp.int32, (CHUNK, CHUNK), 1)
    causal = r >= cc
    tril = jnp.where(causal, 1.0, 0.0).astype(BF16)
    cs = _exact_dot_left(tril, a)
    cs_t = cs.T
    dte = jnp.exp(cs[CHUNK - 1:CHUNK, :] - cs)
    ecs = jnp.exp(cs)
    e01 = e_ref[...]
    dt_x = _exact_dot(dtv, e01)
    dte_x = _exact_dot(dte, e01)
    ecs_x = _exact_dot(ecs, e01)
    xdt = xs * dt_x
    xdec = (xdt * dte_x).astype(BF16)
    xdt_b = xdt.astype(BF16)
    lo = lax.broadcasted_iota(jnp.int32, (CHUNK, LANES), 1) < SSD_HEAD_DIM
    dsk = dskip_ref[...]
    for g in range(SSD_GROUPS):
        gs = slice(g * GROUP_W, (g + 1) * GROUP_W)
        bg = bm[:, g * D_STATE:(g + 1) * D_STATE]
        cg = cm[:, g * D_STATE:(g + 1) * D_STATE].astype(BF16)
        cb = _dot_nt(cg, bg.astype(BF16))
        bgt = bg.T.astype(BF16)
        prev = state_scr[g]
        y_off = _dot(cg, prev.astype(BF16)) * ecs_x[:, gs]
        state_scr[g] = ecs_x[CHUNK - 1:CHUNK, gs] * prev + _dot(bgt, xdec[:, gs])
        for m in range(GROUP_W // LANES):
            ps = slice(g * GROUP_W + m * LANES, g * GROUP_W + (m + 1) * LANES)
            xp = xdt_b[:, ps]
            ydiag = jnp.zeros((CHUNK, LANES), F32)
            for par in range(2):
                j = g * (SSD_HEADS // SSD_GROUPS) + 2 * m + par
                diff = cs[:, j:j + 1] - cs_t[j:j + 1, :]
                ldec = jnp.exp(jnp.where(causal, diff, -jnp.inf))
                mm = (cb * ldec).astype(BF16)
                xpm = jnp.where(lo, xp, 0.0) if par == 0 else jnp.where(lo, 0.0, xp)
                ydiag = ydiag + _dot(mm, xpm.astype(BF16))
            y_ref[:, ps] = ydiag + y_off[:, m * LANES:(m + 1) * LANES] + dsk[:, ps] * xs[:, ps]

    @pl.when(c == last)
    def _():
        for g in range(SSD_GROUPS):
            st_ref[0, g * GROUP_W:(g + 1) * GROUP_W, :] = state_scr[g].T


def ssd_prompt(xbc, dt_raw, cw, cb, dtb, alog, dskip_x, e01, batch, seq):
    nc = seq // CHUNK
    cur = lambda b, c: (b * nc + c, 0)
    const = lambda b, c: (0, 0)
    return pl.pallas_call(
        _ssd_prompt_kernel,
        grid=(batch, nc),
        in_specs=[pl.BlockSpec((CHUNK, CONV_DIM), cur),
                  pl.BlockSpec((CHUNK, HEAD_PAD), cur),
                  pl.BlockSpec((CONV_WIDTH, CONV_DIM), const),
                  pl.BlockSpec((1, CONV_DIM), const),
                  pl.BlockSpec((1, HEAD_PAD), const),
                  pl.BlockSpec((1, HEAD_PAD), const),
                  pl.BlockSpec((1, SSD_WIDTH), const),
                  pl.BlockSpec((HEAD_PAD, SSD_WIDTH), const)],
        out_specs=[pl.BlockSpec((CHUNK, SSD_WIDTH), cur),
                   pl.BlockSpec((1, SSD_WIDTH, D_STATE), lambda b, c: (b, 0, 0))],
        out_shape=[jax.ShapeDtypeStruct((batch * seq, SSD_WIDTH), F32),
                   jax.ShapeDtypeStruct((batch, SSD_WIDTH, D_STATE), F32)],
        scratch_shapes=[pltpu.VMEM((SSD_GROUPS, D_STATE, GROUP_W), F32),
                        pltpu.VMEM((8, CONV_DIM), F32)],
        compiler_params=_params(("parallel", "arbitrary")),
        name="ssd_prompt",
    )(xbc, dt_raw, cw, cb, dtb, alog, dskip_x, e01)


def _swa_sample_kernel(qx_ref, ck_ref, cv_ref, kn_ref, vn_ref, bias_ref, sink_ref,
                       nk_ref, nv_ref, o_ref, *, bt):
    scale = ATT_HEAD_DIM ** -0.5
    kvw = ATT_KV_HEADS * ATT_HEAD_DIM
    rowi = lax.broadcasted_iota(jnp.int32, (WINDOW, kvw), 0)
    bias_c = bias_ref[:, 0:WINDOW]
    bias_n = bias_ref[:, WINDOW:WINDOW + 1]
    sink = sink_ref[:, 0:1]
    for bb in range(bt):
        kc = ck_ref[bb]
        vc = cv_ref[bb]
        kn = kn_ref[bb]
        vn = vn_ref[bb]
        qx = qx_ref[bb]
        s_c = _dot_nt(qx.astype(BF16), kc.astype(BF16)) * scale + bias_c
        s_n = jnp.sum(qx * kn, axis=1, keepdims=True) * scale + bias_n
        m = jnp.maximum(jnp.maximum(jnp.max(s_c, axis=1, keepdims=True), s_n), sink)
        p_c = jnp.exp(s_c - m)
        p_n = jnp.exp(s_n - m)
        den = jnp.sum(p_c, axis=1, keepdims=True) + p_n + jnp.exp(sink - m)
        o_ref[bb] = _dot((p_c / den).astype(BF16), vc.astype(BF16)) + (p_n / den) * vn
        nk_ref[bb] = jnp.where(rowi == WINDOW - 1, kn, pltpu.roll(kc, WINDOW - 1, axis=0))
        nv_ref[bb] = jnp.where(rowi == WINDOW - 1, vn, pltpu.roll(vc, WINDOW - 1, axis=0))


def swa_sample(qx, ck, cv, kn, vn, bias_s, sink_b, bt=8):
    nb = qx.shape[0]
    kvw = ATT_KV_HEADS * ATT_HEAD_DIM
    blk3 = lambda i: (i, 0, 0)
    return pl.pallas_call(
        functools.partial(_swa_sample_kernel, bt=bt),
        grid=(nb // bt,),
        in_specs=[pl.BlockSpec((bt, ATT_HEADS, kvw), blk3),
                  pl.BlockSpec((bt, WINDOW, kvw), blk3),
                  pl.BlockSpec((bt, WINDOW, kvw), blk3),
                  pl.BlockSpec((bt, 1, kvw), blk3),
                  pl.BlockSpec((bt, 1, kvw), blk3),
                  pl.BlockSpec((ATT_HEADS, 2 * WINDOW), lambda i: (0, 0)),
                  pl.BlockSpec((ATT_HEADS, LANES), lambda i: (0, 0))],
        out_specs=[pl.BlockSpec((bt, WINDOW, kvw), blk3),
                   pl.BlockSpec((bt, WINDOW, kvw), blk3),
                   pl.BlockSpec((bt, ATT_HEADS, kvw), blk3)],
        out_shape=[jax.ShapeDtypeStruct((nb, WINDOW, kvw), F32),
                   jax.ShapeDtypeStruct((nb, WINDOW, kvw), F32),
                   jax.ShapeDtypeStruct((nb, ATT_HEADS, kvw), F32)],
        compiler_params=_params(("parallel",)),
        name="swa_sample",
    )(qx, ck, cv, kn, vn, bias_s, sink_b)


def _ssd_sample_prep_kernel(xbc_ref, c0_ref, c1_ref, c2_ref, dt_ref, cw_ref, cb_ref, dtb_ref,
                            alog_ref, e_ref, dec_t_ref, xdt_t_ref, b_ref, c_ref, xs_ref):
    u = (c0_ref[...] * cw_ref[0:1, :] + c1_ref[...] * cw_ref[1:2, :]
         + c2_ref[...] * cw_ref[2:3, :] + xbc_ref[...] * cw_ref[3:4, :]) + cb_ref[...]
    u = u * _sigmoid(u)
    xs = u[:, :SSD_WIDTH]
    b_ref[...] = u[:, SSD_WIDTH:SSD_WIDTH + SSD_GROUPS * D_STATE]
    c_ref[...] = u[:, SSD_WIDTH + SSD_GROUPS * D_STATE:]
    xs_ref[...] = xs
    dtv = _softplus(dt_ref[...] + dtb_ref[...])
    dec = jnp.exp(dtv * (-jnp.exp(alog_ref[...])))
    e01 = e_ref[...]
    dec_t_ref[...] = _exact_dot(dec, e01).T
    xdt_t_ref[...] = (xs * _exact_dot(dtv, e01)).T


def ssd_sample_prep(xbc, c0, c1, c2, dt_raw, cw, cb, dtb, alog, e01):
    nb = xbc.shape[0]
    args = (xbc, c0, c1, c2, dt_raw, cw, cb, dtb, alog, e01)
    full = lambda a: pl.BlockSpec(a.shape, lambda i: (0,) * a.ndim)
    out_shape = [jax.ShapeDtypeStruct((SSD_WIDTH, nb), F32),
                 jax.ShapeDtypeStruct((SSD_WIDTH, nb), F32),
                 jax.ShapeDtypeStruct((nb, SSD_GROUPS * D_STATE), F32),
                 jax.ShapeDtypeStruct((nb, SSD_GROUPS * D_STATE), F32),
                 jax.ShapeDtypeStruct((nb, SSD_WIDTH), F32)]
    return pl.pallas_call(
        _ssd_sample_prep_kernel,
        grid=(1,),
        in_specs=[full(a) for a in args],
        out_specs=[full(s) for s in out_shape],
        out_shape=out_shape,
        compiler_params=_params(("arbitrary",)),
        name="ssd_sample_prep",
    )(*args)


def _ssd_sample_step_kernel(st_ref, dec_t_ref, xdt_t_ref, b_ref, c_ref, xs_ref, dskip_ref,
                            ns_ref, y_ref, yt_scr):
    b = pl.program_id(0)
    nb = dec_t_ref.shape[1]

    @pl.when(b == 0)
    def _():
        yt_scr[...] = jnp.zeros(yt_scr.shape, F32)

    pick = jnp.where(lax.broadcasted_iota(jnp.int32, (nb, D_STATE), 0) == b,
                     1.0, 0.0).astype(BF16)
    dcol = _exact_dot(dec_t_ref[...], pick)
    xcol = _exact_dot(xdt_t_ref[...], pick)
    brow = b_ref[pl.ds(b, 1), :]
    crow = c_ref[pl.ds(b, 1), :]
    lane = lax.broadcasted_iota(jnp.int32, (GROUP_W, nb), 1)
    for g in range(SSD_GROUPS):
        gs = slice(g * GROUP_W, (g + 1) * GROUP_W)
        ns = slice(g * D_STATE, (g + 1) * D_STATE)
        hn = dcol[gs] * st_ref[0, gs, :] + xcol[gs] * brow[:, ns]
        ns_ref[0, gs, :] = hn
        ycol = jnp.sum(hn * crow[:, ns], axis=1, keepdims=True)
        yt_scr[gs, :] = jnp.where(lane == b, ycol, yt_scr[gs, :])

    @pl.when(b == nb - 1)
    def _():
        y_ref[...] = yt_scr[...].T + dskip_ref[...] * xs_ref[...]


def ssd_sample_step(state, dec_t, xdt_t, bmat, cmat, xs, dskip_x):
    nb = state.shape[0]
    assert nb == D_STATE
    const2 = lambda i: (0, 0)
    return pl.pallas_call(
        _ssd_sample_step_kernel,
        grid=(nb,),
        in_specs=[pl.BlockSpec((1, SSD_WIDTH, D_STATE), lambda i: (i, 0, 0)),
                  pl.BlockSpec((SSD_WIDTH, nb), const2),
                  pl.BlockSpec((SSD_WIDTH, nb), const2),
                  pl.BlockSpec((nb, SSD_GROUPS * D_STATE), const2),
                  pl.BlockSpec((nb, SSD_GROUPS * D_STATE), const2),
                  pl.BlockSpec((nb, SSD_WIDTH), const2),
                  pl.BlockSpec((1, SSD_WIDTH), const2)],
        out_specs=[pl.BlockSpec((1, SSD_WIDTH, D_STATE), lambda i: (i, 0, 0)),
                   pl.BlockSpec((nb, SSD_WIDTH), const2)],
        out_shape=[jax.ShapeDtypeStruct((nb, SSD_WIDTH, D_STATE), F32),
                   jax.ShapeDtypeStruct((nb, SSD_WIDTH), F32)],
        scratch_shapes=[pltpu.VMEM((SSD_WIDTH, nb), F32)],
        compiler_params=_params(("arbitrary",)),
        name="ssd_sample_step",
    )(state, dec_t, xdt_t, bmat, cmat, xs, dskip_x)


def _merge_kernel(x_ref, o_ref, y_ref, z_ref, anw_ref, snw_ref, w1_ref, w2_ref, out_ref):
    ya = _rms(o_ref[...], anw_ref[...]).astype(BF16)
    z = z_ref[...]
    ys = _rms(y_ref[...] * (z * _sigmoid(z)), snw_ref[...]).astype(BF16)
    out_ref[...] = x_ref[...] + _dot(ya, w1_ref[...]) + _dot(ys, w2_ref[...])


def merge(x, o_att, y_ssd, z, anw, snw, w1, w2, tm):
    t, d = x.shape
    row = pl.BlockSpec((tm, d), lambda i: (i, 0))
    vec = pl.BlockSpec((1, d), lambda i: (0, 0))
    mat = pl.BlockSpec((d, d), lambda i: (0, 0))
    return pl.pallas_call(
        _merge_kernel,
        grid=(t // tm,),
        in_specs=[row, row, row, row, vec, vec, mat, mat],
        out_specs=row,
        out_shape=jax.ShapeDtypeStruct((t, d), F32),
        compiler_params=_params(("parallel",)),
        name="merge",
    )(x, o_att, y_ssd, z, anw.reshape(1, d), snw.reshape(1, d), w1, w2)


def _mem_prompt_kernel(x_ref, nw_ref, wq_ref, mk_ref, mv_ref, wo_ref, out_ref):
    x = x_ref[...]
    q = _dot(_rms(x, nw_ref[...]).astype(BF16), wq_ref[...])
    scale = MEM_HEAD_DIM ** -0.5
    outs = []
    for h in range(MEM_HEADS):
        hs = slice(h * MEM_HEAD_DIM, (h + 1) * MEM_HEAD_DIM)
        s = _dot_nt(q[:, hs].astype(BF16), mk_ref[:, hs]) * scale
        p = jnp.exp(s - jnp.max(s, axis=-1, keepdims=True))
        p = p / jnp.sum(p, axis=-1, keepdims=True)
        outs.append(_dot(p.astype(BF16), mv_ref[:, hs]).astype(BF16))
    out_ref[...] = x + _dot(jnp.concatenate(outs, axis=1), wo_ref[...])


def mem_attn_prompt(x, nw, wq, mk, mv, wo, seq, tm):
    t, d = x.shape
    per = seq // tm
    row = pl.BlockSpec((tm, d), lambda i: (i, 0))
    mat = pl.BlockSpec((d, d), lambda i: (0, 0))
    mem = pl.BlockSpec((N_MEM, d), lambda i: (i // per, 0))
    return pl.pallas_call(
        _mem_prompt_kernel,
        grid=(t // tm,),
        in_specs=[row, pl.BlockSpec((1, d), lambda i: (0, 0)), mat, mem, mem, mat],
        out_specs=row,
        out_shape=jax.ShapeDtypeStruct((t, d), F32),
        compiler_params=_params(("parallel",)),
        name="mem_prompt",
    )(x, nw.reshape(1, d), wq, mk, mv, wo)


def _mem_sample_kernel(q_ref, k_ref, v_ref, o_ref):
    q4 = q_ref[0]
    q8 = jnp.concatenate([q4, jnp.zeros_like(q4)], axis=0).astype(BF16)
    rows = N_MEM * MEM_HEADS
    k2 = k_ref[0].reshape(rows, MEM_HEAD_DIM).astype(BF16)
    v2 = v_ref[0].reshape(rows, MEM_HEAD_DIM).astype(BF16)
    s = _dot_nt(q8, k2) * (MEM_HEAD_DIM ** -0.5)
    qh = lax.broadcasted_iota(jnp.int32, s.shape, 0)
    kh = lax.broadcasted_iota(jnp.int32, s.shape, 1) % MEM_HEADS
    s = jnp.where(kh == qh, s, -jnp.inf)
    live = qh < MEM_HEADS
    p = jnp.exp(s - jnp.where(live, jnp.max(s, axis=1, keepdims=True), 0.0))
    p = p / jnp.where(live, jnp.sum(p, axis=1, keepdims=True), 1.0)
    o_ref[0] = _dot(p.astype(BF16), v2)[0:MEM_HEADS]


def mem_sample(q, ck, cv):
    nb, d = q.shape
    blk = pl.BlockSpec((1, N_MEM, MEM_HEADS, MEM_HEAD_DIM), lambda i: (i, 0, 0, 0))
    vec = pl.BlockSpec((1, MEM_HEADS, MEM_HEAD_DIM), lambda i: (i, 0, 0))
    return pl.pallas_call(
        _mem_sample_kernel,
        grid=(nb,),
        in_specs=[vec, blk, blk],
        out_specs=vec,
        out_shape=jax.ShapeDtypeStruct((nb, MEM_HEADS, MEM_HEAD_DIM), F32),
        compiler_params=_params(("parallel",)),
        name="mem_sample",
    )(q.reshape(nb, MEM_HEADS, MEM_HEAD_DIM), ck, cv).reshape(nb, d)


def _matmul_res_kernel(a_ref, w_ref, r_ref, out_ref):
    out_ref[...] = r_ref[...] + _dot(a_ref[...].astype(BF16), w_ref[...])


def matmul_res(a, w, res, tm):
    t, d = res.shape
    kdim = a.shape[1]
    return pl.pallas_call(
        _matmul_res_kernel,
        grid=(t // tm,),
        in_specs=[pl.BlockSpec((tm, kdim), lambda i: (i, 0)),
                  pl.BlockSpec((kdim, d), lambda i: (0, 0)),
                  pl.BlockSpec((tm, d), lambda i: (i, 0))],
        out_specs=pl.BlockSpec((tm, d), lambda i: (i, 0)),
        out_shape=jax.ShapeDtypeStruct((t, d), F32),
        compiler_params=_params(("parallel",)),
        name="matmul_res",
    )(a, w, res)


def _topk_rows(work, k):
    rows = work.shape[0]
    iota = lax.broadcasted_iota(jnp.int32, work.shape, 0)
    rank = jnp.full(work.shape, k, jnp.int32)
    vals = []
    for r in range(k):
        m = jnp.max(work, axis=0, keepdims=True)
        idx = jnp.min(jnp.where(work == m, iota, rows), axis=0, keepdims=True)
        hit = iota == idx
        rank = jnp.where(hit, r, rank)
        work = jnp.where(hit, -jnp.inf, work)
        vals.append(m)
    return jnp.concatenate(vals, axis=0), rank


def _topk_rows_distinct(work, k):
    rank = jnp.full(work.shape, k, jnp.int32)
    vals = []
    for r in range(k):
        m = jnp.max(work, axis=0, keepdims=True)
        hit = work == m
        rank = jnp.where(hit, r, rank)
        work = jnp.where(hit, -jnp.inf, work)
        vals.append(m)
    taken = jnp.sum(jnp.where(rank < k, 1.0, 0.0), axis=0, keepdims=True)
    return jnp.concatenate(vals, axis=0), rank, taken


def _peer_candidates(a, b):
    lo4 = lax.broadcasted_iota(jnp.int32, (8, a.shape[1]), 0) < 4
    b8 = b[0:8]
    b44 = jnp.where(lo4, b8, pltpu.roll(b8, 4, axis=0))
    return jnp.concatenate(
        [a[0:1] + b8, a[0:1] + b[8:16], a[1:2] + b8, a[2:3] + b8, a[3:4] + b8,
         jnp.where(lo4, a[4:5], a[5:6]) + b44, jnp.where(lo4, a[6:7], a[7:8]) + b44,
         a[8:16] + b[0:1]], axis=0)


def _peer_emit(subs, a, b, rank0, rank1, cand, chosen, seli_ref, selj_ref):
    k = PEER_TOPK
    cmax = a[0:1, :] + b[0:1, :]
    z = jnp.sum(jnp.where(chosen, jnp.exp(cand - cmax), 0.0), axis=0, keepdims=True)
    lo4 = lax.broadcasted_iota(jnp.int32, (8, a.shape[1]), 0) < 4
    ch = [jnp.where(chosen[8 * i:8 * i + 8], 1.0, 0.0) for i in range(8)]
    cnt = lambda v: jnp.sum(v, axis=0, keepdims=True)
    n = [cnt(ch[0]) + cnt(ch[1]), cnt(ch[2]), cnt(ch[3]), cnt(ch[4]),
         cnt(jnp.where(lo4, ch[5], 0.0)), cnt(jnp.where(lo4, 0.0, ch[5])),
         cnt(jnp.where(lo4, ch[6], 0.0)), cnt(jnp.where(lo4, 0.0, ch[6]))]
    n += [ch[7][i:i + 1] for i in range(8)]
    lim0 = jnp.zeros(rank0.shape, F32)
    for r in range(k):
        lim0 = jnp.where(rank0 == r, n[r], lim0)
    seli_ref[0, 0] = jnp.exp(subs[0] - a[0:1, :]) / z
    seli_ref[0, 1] = lim0
    selj_ref[0, 0] = pltpu.bitcast(jnp.exp(subs[1] - b[0:1, :]).astype(BF16), jnp.uint32)
    selj_ref[0, 1] = pltpu.bitcast(rank1.astype(F32).astype(BF16), jnp.uint32)


def _peer_select_kernel(pq_ref, keys_ref, seli_ref, selj_ref):
    k = PEER_TOPK
    subs = [_dot_nt(keys_ref[p], pq_ref[:, p * PEER_HALF:(p + 1) * PEER_HALF].astype(BF16))
            for p in range(2)]
    a, rank0, took0 = _topk_rows_distinct(subs[0], k)
    b, rank1, took1 = _topk_rows_distinct(subs[1], k)
    cand = _peer_candidates(a, b)
    _, crank, tookc = _topk_rows_distinct(cand, k)
    _peer_emit(subs, a, b, rank0, rank1, cand, crank < k, seli_ref, selj_ref)
    tied = jnp.where((took0 != k) | (took1 != k) | (tookc != k), 1.0, 0.0)

    @pl.when(jnp.max(tied) > 0.0)
    def _():
        a, rank0 = _topk_rows(subs[0], k)
        b, rank1 = _topk_rows(subs[1], k)
        cand = _peer_candidates(a, b)
        _, crank = _topk_rows(cand, k)
        _peer_emit(subs, a, b, rank0, rank1, cand, crank < k, seli_ref, selj_ref)


def peer_select(pq, keys_bf16, tm):
    t = pq.shape[0]
    spec = lambda rows: pl.BlockSpec((1, 2, rows, tm), lambda i, h: (h, 0, 0, i))
    return pl.pallas_call(
        _peer_select_kernel,
        grid=(t // tm, PEER_HEADS),
        in_specs=[pl.BlockSpec((tm, 2 * PEER_HALF), lambda i, h: (i, h)),
                  pl.BlockSpec((2, N_KEYS, PEER_HALF), lambda i, h: (h, 0, 0))],
        out_specs=[spec(N_KEYS), spec(N_KEYS // 2)],
        out_shape=[jax.ShapeDtypeStruct((PEER_HEADS, 2, N_KEYS, t), F32),
                   jax.ShapeDtypeStruct((PEER_HEADS, 2, N_KEYS // 2, t), jnp.uint32)],
        compiler_params=_params(("parallel", "parallel")),
        name="peer_select",
    )(pq, keys_bf16)


def _gelu(x):
    return 0.5 * x * (1.0 + lax.erf(x * np.float32(math.sqrt(0.5))))


def _peer_dense_kernel(xnt_ref, x_ref, u_ref, vt_ref, seli_ref, selj_ref, fnw_ref, y_ref,
                       acc_scr, act_scr, w_scr, *, tm, ib, sub):
    e = pl.program_id(1)
    ne = pl.num_programs(1)

    @pl.when(e == 0)
    def _():
        acc_scr[...] = jnp.zeros(acc_scr.shape, F32)

    i8 = pl.ds(pl.multiple_of(e * ib, 8), ib)
    nsub = ib // sub
    d = acc_scr.shape[0]
    halves = 2
    sub_rows = lambda k: slice(k * sub * N_KEYS, (k + 1) * sub * N_KEYS)

    def activations(k, half):
        cs = slice(half * tm // halves, (half + 1) * tm // halves)
        act_scr[k % 2, :, cs] = _dot(u_ref[sub_rows(k), :], pltpu.bitcast(xnt_ref[:, cs], BF16))

    def accumulate(k, half):
        rs = slice(half * d // halves, (half + 1) * d // halves)
        acc_scr[rs, :] += _dot(vt_ref[0, rs, sub_rows(k)], w_scr[k % 3])

    def gates(k, lt):
        ls = slice(lt * LANES, (lt + 1) * LANES)
        for j in range(sub):
            ii = k * sub + j
            rs = slice(j * N_KEYS, (j + 1) * N_KEYS)
            g = jnp.zeros((N_KEYS, LANES), BF16)
            for h in range(PEER_HEADS):
                row = lambda c: jnp.broadcast_to(seli_ref[h, c, i8, ls][ii:ii + 1, :],
                                                 (N_KEYS, LANES)).astype(BF16)
                e1 = pltpu.bitcast(selj_ref[h, 0, :, ls], BF16)
                rank1 = pltpu.bitcast(selj_ref[h, 1, :, ls], BF16)
                g = g + jnp.where(rank1 < row(1), e1, jnp.zeros_like(e1)) * row(0)
            w_scr[k % 3, rs, ls] = g * _gelu(act_scr[k % 2, rs, ls]).astype(BF16)

    for half in range(halves):
        activations(0, half)
    for k in range(nsub):
        mxu_jobs = []
        if k + 1 < nsub:
            mxu_jobs += [functools.partial(activations, k + 1, half) for half in range(halves)]
        if k >= 1:
            mxu_jobs += [functools.partial(accumulate, k - 1, half) for half in range(halves)]
        for job in mxu_jobs:
            job()
        for lt in range(tm // LANES):
            gates(k, lt)
    for half in range(halves):
        accumulate(nsub - 1, half)

    @pl.when(e == ne - 1)
    def _():
        y_ref[...] = _rms(x_ref[...] + acc_scr[...].T, fnw_ref[...])


PEER_BLOCK_ROWS = 16
PEER_SUB_ROWS = 4


def peer_dense(xnt, x, u_bf16, vt_blocks, seli, selj, fnw, tm, ib=PEER_BLOCK_ROWS,
               sub=PEER_SUB_ROWS):
    t, d = x.shape
    eb = ib * N_KEYS
    nblk = N_EXPERTS // eb
    assert vt_blocks.shape == (nblk, d, eb) and xnt.shape == (d // 2, t)
    sel_spec = lambda rows: pl.BlockSpec((PEER_HEADS, 2, rows, tm), lambda i, e: (0, 0, 0, i))
    return pl.pallas_call(
        functools.partial(_peer_dense_kernel, tm=tm, ib=ib, sub=sub),
        grid=(t // tm, nblk),
        in_specs=[pl.BlockSpec((d // 2, tm), lambda i, e: (0, i)),
                  pl.BlockSpec((tm, d), lambda i, e: (i, 0)),
                  pl.BlockSpec((eb, d), lambda i, e: (e, 0)),
                  pl.BlockSpec((1, d, eb), lambda i, e: (e, 0, 0)),
                  sel_spec(N_KEYS), sel_spec(N_KEYS // 2),
                  pl.BlockSpec((1, d), lambda i, e: (0, 0))],
        out_specs=pl.BlockSpec((tm, d), lambda i, e: (i, 0)),
        out_shape=jax.ShapeDtypeStruct((t, d), F32),
        scratch_shapes=[pltpu.VMEM((d, tm), F32),
                        pltpu.VMEM((2, sub * N_KEYS, tm), F32),
                        pltpu.VMEM((3, sub * N_KEYS, tm), BF16)],
        compiler_params=_params(("parallel", "arbitrary")),
        name="peer_dense",
    )(xnt, x, u_bf16, vt_blocks, seli, selj, fnw.reshape(1, d))


def _tail(x, mk, mv, prm, seq, tm, mem_is_shared, tm_peer):
    if mem_is_shared:
        x = mem_attn_prompt(x, prm["norm_mem_w"], prm["w_mq"], mk, mv, prm["w_mo"], seq, tm)
    else:
        (q,) = norm_matmul(x, prm["norm_mem_w"], prm["w_mq"], (D_MODEL,), tm)
        x = matmul_res(mem_sample(q, mk, mv), prm["w_mo"], x, tm)
    xnt, pq = norm_matmul(x, prm["norm_ffn_w"], prm["peer_wq"], (prm["peer_wq"].shape[1],), tm,
                          emit_ht=True)
    seli, selj = peer_select(pq, prm["peer_keys"], min(tm, 256))
    return peer_dense(xnt, x, prm["peer_u"], prm["peer_vt"], seli, selj, prm["final_norm_w"],
                      tm_peer)


def kernel(x_prompt, x_sample, mem_prompt, cache_swa_k, cache_swa_v, state_ssm, state_conv, cache_mem_k, cache_mem_v, norm_mix_w, w_in, attn_sinks, rel_bias, attn_norm_w, conv_w, conv_b, dt_bias, a_log, d_skip, ssd_norm_w, w_out, norm_mem_w, mem_norm_w, w_mq, w_mk, w_mv, w_mo, norm_ffn_w, peer_wq, peer_keys, peer_u, peer_v, final_norm_w):
    assert w_in.shape[0] == 1, "single-layer step"
    batch, seq, d = x_prompt.shape
    nb = x_sample.shape[0]
    kvw = ATT_KV_HEADS * ATT_HEAD_DIM
    tm = 512

    n_main = D_MODEL + 2 * kvw + SSD_WIDTH + CONV_DIM
    w_in_p = jnp.concatenate(
        [w_in[0], jnp.zeros((d, HEAD_PAD - SSD_HEADS), F32)], axis=1).astype(BF16)
    in_splits = (D_MODEL, kvw, kvw, SSD_WIDTH, CONV_DIM, HEAD_PAD)
    assert n_main + SSD_HEADS == w_in.shape[2]
    pad_h = lambda v: jnp.pad(v.reshape(1, SSD_HEADS), ((0, 0), (0, HEAD_PAD - SSD_HEADS)))
    dtb, alog = pad_h(dt_bias[0]), pad_h(a_log[0])
    dskip_x = jnp.repeat(d_skip[0], SSD_HEAD_DIM).reshape(1, SSD_WIDTH)
    e01 = (jnp.arange(HEAD_PAD)[:, None] == jnp.arange(SSD_WIDTH)[None, :] // SSD_HEAD_DIM
           ).astype(BF16)
    cw, cb = conv_w[0], conv_b[0].reshape(1, CONV_DIM)
    w_out1 = w_out[0, :D_MODEL].astype(BF16)
    w_out2 = w_out[0, D_MODEL:].astype(BF16)
    prm = dict(norm_mem_w=norm_mem_w[0], w_mq=w_mq[0].astype(BF16), w_mo=w_mo[0].astype(BF16),
               norm_ffn_w=norm_ffn_w[0], peer_wq=peer_wq[0].astype(BF16),
               peer_keys=peer_keys[0].reshape(2 * PEER_HEADS, N_KEYS, PEER_HALF).astype(BF16),
               peer_u=peer_u[0].astype(BF16),
               peer_vt=jnp.swapaxes(peer_v[0].astype(BF16).reshape(
                   N_EXPERTS // (PEER_BLOCK_ROWS * N_KEYS), PEER_BLOCK_ROWS * N_KEYS, d), 1, 2),
               final_norm_w=final_norm_w)

    qi = jnp.arange(WINDOW)[:, None] + WINDOW
    bias_p = rel_bias_rows(rel_bias, _t5_bucket(qi - jnp.arange(2 * WINDOW)[None, :]))
    dist_s = jnp.broadcast_to(jnp.maximum(WINDOW - jnp.arange(2 * WINDOW), 0)[None, :],
                              (8, 2 * WINDOW))
    bias_s = rel_bias_rows(rel_bias, _t5_bucket(dist_s))[:, 0, :]
    sinks = attn_sinks[0]

    xp = x_prompt.reshape(batch * seq, d)
    q, k, v, z, xbc, dt_raw = norm_matmul(xp, norm_mix_w[0], w_in_p, in_splits, tm)
    o_att = swa_prompt(q, k, v, bias_p, sinks, batch, seq)
    y_ssd, p_state = ssd_prompt(xbc, dt_raw, cw, cb, dtb, alog, dskip_x, e01, batch, seq)
    x1 = merge(xp, o_att, y_ssd, z, attn_norm_w[0], ssd_norm_w[0], w_out1, w_out2, tm)
    mkv_w = jnp.concatenate([w_mk[0], w_mv[0]], axis=1).astype(BF16)
    mk, mv = norm_matmul(mem_prompt.reshape(batch * N_MEM, d), mem_norm_w[0], mkv_w, (d, d), tm)
    y_p = _tail(x1, mk.astype(BF16), mv.astype(BF16), prm, seq, tm, True, tm)

    xs_in = x_sample.reshape(nb, d)
    tms = nb
    qs, ks, vs, zs, xbcs, dts = norm_matmul(xs_in, norm_mix_w[0], w_in_p, in_splits, tms)
    own = (jnp.arange(kvw)[None, :] // ATT_HEAD_DIM) == (jnp.arange(ATT_HEADS)[:, None] // (ATT_HEADS // ATT_KV_HEADS))
    qx = jnp.where(own[None], jnp.tile(qs.reshape(nb, ATT_HEADS, ATT_HEAD_DIM), (1, 1, ATT_KV_HEADS)), 0.0)
    ck = cache_swa_k[0].reshape(nb, WINDOW, kvw)
    cv = cache_swa_v[0].reshape(nb, WINDOW, kvw)
    sink_b = jnp.broadcast_to(sinks[:, None], (ATT_HEADS, LANES))
    nk, nv, ox = swa_sample(qx, ck, cv, ks.reshape(nb, 1, kvw), vs.reshape(nb, 1, kvw), bias_s, sink_b)
    ox5 = ox.reshape(nb, ATT_KV_HEADS, ATT_HEADS // ATT_KV_HEADS, ATT_KV_HEADS, ATT_HEAD_DIM)
    o_att_s = jnp.stack([ox5[:, g, :, g, :] for g in range(ATT_KV_HEADS)], axis=1).reshape(nb, d)
    sc = state_conv[0]
    dec_t, xdt_t, bmat, cmat, xs_s = ssd_sample_prep(xbcs, sc[:, 0], sc[:, 1], sc[:, 2], dts, cw, cb, dtb, alog, e01)
    new_state, y_s = ssd_sample_step(state_ssm[0].reshape(nb, SSD_WIDTH, D_STATE), dec_t, xdt_t, bmat, cmat, xs_s, dskip_x)
    x1s = merge(xs_in, o_att_s, y_s, zs, attn_norm_w[0], ssd_norm_w[0], w_out1, w_out2, tms)
    y_s_out = _tail(x1s, cache_mem_k[0], cache_mem_v[0], prm, seq, tms, False, tms)

    k4 = k.reshape(batch, seq, ATT_KV_HEADS, ATT_HEAD_DIM)
    v4 = v.reshape(batch, seq, ATT_KV_HEADS, ATT_HEAD_DIM)
    xbc3 = xbc.reshape(batch, seq, CONV_DIM)
    return (y_p.reshape(batch, seq, d),
            y_s_out.reshape(nb, 1, d),
            k4[None, :, seq - WINDOW:],
            v4[None, :, seq - WINDOW:],
            p_state.reshape(1, batch, SSD_HEADS, SSD_HEAD_DIM, D_STATE),
            xbc3[None, :, seq - (CONV_WIDTH - 1):],
            mk.reshape(1, batch, N_MEM, MEM_HEADS, MEM_HEAD_DIM),
            mv.reshape(1, batch, N_MEM, MEM_HEADS, MEM_HEAD_DIM),
            nk.reshape(1, nb, WINDOW, ATT_KV_HEADS, ATT_HEAD_DIM),
            nv.reshape(1, nb, WINDOW, ATT_KV_HEADS, ATT_HEAD_DIM),
            new_state.reshape(1, nb, SSD_HEADS, SSD_HEAD_DIM, D_STATE),
            jnp.concatenate([sc[:, 1:], xbcs[:, None, :]], axis=1)[None])
```

```python
import functools
import math

import numpy as np
import jax
import jax.numpy as jnp
from jax import lax
from jax.experimental import pallas as pl
from jax.experimental.pallas import tpu as pltpu

F32 = jnp.float32
BF16 = jnp.bfloat16
EPS = 1e-6

D_MODEL = 1024
ATT_HEADS = 16
ATT_KV_HEADS = 4
ATT_HEAD_DIM = 64
WINDOW = 128
N_BUCKETS = 32
MAX_DISTANCE = 128
SSD_HEADS = 16
SSD_HEAD_DIM = 64
SSD_WIDTH = SSD_HEADS * SSD_HEAD_DIM
SSD_GROUPS = 2
D_STATE = 128
CONV_WIDTH = 4
CONV_DIM = SSD_WIDTH + 2 * SSD_GROUPS * D_STATE
CHUNK = 128
N_MEM = 256
MEM_HEADS = 4
MEM_HEAD_DIM = D_MODEL // MEM_HEADS
PEER_HEADS = 8
N_KEYS = 128
N_EXPERTS = N_KEYS * N_KEYS
PEER_TOPK = 16
PEER_HALF = 128

LANES = 128
HEAD_PAD = 128
GROUP_W = SSD_WIDTH // SSD_GROUPS
VMEM_LIMIT = 56 * 1024 * 1024


def _params(sem):
    return pltpu.CompilerParams(dimension_semantics=sem, vmem_limit_bytes=VMEM_LIMIT)


def _dot(a, b):
    return jnp.dot(a, b, preferred_element_type=F32)


def _dot_nt(a, b):
    return lax.dot_general(a, b, (((1,), (1,)), ((), ())), preferred_element_type=F32)


def _rms(x, w):
    var = jnp.mean(x * x, axis=-1, keepdims=True)
    return x * lax.rsqrt(var + EPS) * w


def _sigmoid(x):
    return 1.0 / (1.0 + jnp.exp(-x))


def _softplus(x):
    return jnp.maximum(x, 0.0) + jnp.log1p(jnp.exp(-jnp.abs(x)))


def _split3(v):
    v1 = v.astype(BF16)
    r1 = v - v1.astype(F32)
    v2 = r1.astype(BF16)
    r2 = r1 - v2.astype(F32)
    return v1, v2, r2.astype(BF16)


def _exact_dot(v, m01):
    v1, v2, v3 = _split3(v)
    return _dot(v1, m01) + _dot(v2, m01) + _dot(v3, m01)


def _exact_dot_left(m01, v):
    v1, v2, v3 = _split3(v)
    return _dot(m01, v1) + _dot(m01, v2) + _dot(m01, v3)


def _norm_matmul_kernel(x_ref, nw_ref, w_ref, *out_refs, splits, emit_ht):
    hf = _rms(x_ref[...], nw_ref[...])
    h = hf.astype(BF16)
    refs = list(out_refs)
    if emit_ht:
        refs.pop(0)[...] = pltpu.bitcast(hf.T.astype(BF16), jnp.uint32)
    off = 0
    for o_ref, n in zip(refs, splits):
        o_ref[...] = _dot(h, w_ref[:, off:off + n])
        off += n


def norm_matmul(x, nw, w_bf16, splits, tm, emit_ht=False):
    t, d = x.shape
    n = w_bf16.shape[1]
    assert sum(splits) == n and t % tm == 0
    out_shape = [jax.ShapeDtypeStruct((t, s), F32) for s in splits]
    out_specs = [pl.BlockSpec((tm, s), lambda i: (i, 0)) for s in splits]
    if emit_ht:
        out_shape.insert(0, jax.ShapeDtypeStruct((d // 2, t), jnp.uint32))
        out_specs.insert(0, pl.BlockSpec((d // 2, tm), lambda i: (0, i)))
    return pl.pallas_call(
        functools.partial(_norm_matmul_kernel, splits=tuple(splits), emit_ht=emit_ht),
        grid=(t // tm,),
        in_specs=[pl.BlockSpec((tm, d), lambda i: (i, 0)),
                  pl.BlockSpec((1, d), lambda i: (0, 0)),
                  pl.BlockSpec((d, n), lambda i: (0, 0))],
        out_specs=out_specs,
        out_shape=out_shape,
        compiler_params=_params(("parallel",)),
        name="norm_matmul",
    )(x, nw.reshape(1, d), w_bf16)


def _bias_kernel(table_ref, bucket_ref, out_ref):
    h = pl.program_id(0)
    bk = bucket_ref[...]
    acc = jnp.zeros(bk.shape, F32)
    for b in range(N_BUCKETS):
        acc = jnp.where(bk == b, table_ref[b, h], acc)
    out_ref[0] = acc


def rel_bias_rows(table, bucket):
    r, c = bucket.shape
    return pl.pallas_call(
        _bias_kernel,
        grid=(ATT_HEADS,),
        in_specs=[pl.BlockSpec(memory_space=pltpu.SMEM),
                  pl.BlockSpec((r, c), lambda h: (0, 0))],
        out_specs=pl.BlockSpec((1, r, c), lambda h: (h, 0, 0)),
        out_shape=jax.ShapeDtypeStruct((ATT_HEADS, r, c), F32),
        compiler_params=_params(("parallel",)),
        name="rel_bias",
    )(table, bucket)


def _t5_bucket(dist):
    n = jnp.maximum(dist, 0)
    max_exact = N_BUCKETS // 2
    nf = jnp.maximum(n, 1).astype(F32)
    large = max_exact + (jnp.log(nf / max_exact) / math.log(MAX_DISTANCE / max_exact)
                         * (N_BUCKETS - max_exact)).astype(jnp.int32)
    large = jnp.minimum(large, N_BUCKETS - 1)
    return jnp.where(n < max_exact, n, large)


def _half_pair(pair, rolled, which, lo):
    if which == 0:
        return jnp.where(lo, pair, 0.0), jnp.where(lo, 0.0, rolled)
    return jnp.where(lo, rolled, 0.0), jnp.where(lo, 0.0, pair)


def _swa_prompt_kernel(sink_ref, q_ref, kc_ref, kp_ref, vc_ref, vp_ref, bias_ref, o_ref):
    i = pl.program_id(1)
    blk = WINDOW
    row = lax.broadcasted_iota(jnp.int32, (blk, 2 * blk), 0)
    col = lax.broadcasted_iota(jnp.int32, (blk, 2 * blk), 1)
    dist = row + blk - col
    ok1 = (dist >= 0) & (dist <= WINDOW) & ((col >= blk) | (i > 0))
    ok = jnp.concatenate([ok1, ok1], axis=0)
    first = lax.broadcasted_iota(jnp.int32, (2 * blk, 1), 0) < blk
    lo = lax.broadcasted_iota(jnp.int32, (2 * blk, LANES), 1) < ATT_HEAD_DIM
    scale = ATT_HEAD_DIM ** -0.5
    for pair in range(ATT_KV_HEADS // 2):
        sl = slice(pair * LANES, (pair + 1) * LANES)
        kpair = jnp.concatenate([kp_ref[:, sl], kc_ref[:, sl]], axis=0)
        vpair = jnp.concatenate([vp_ref[:, sl], vc_ref[:, sl]], axis=0)
        kroll = pltpu.roll(kpair, ATT_HEAD_DIM, axis=1)
        vroll = pltpu.roll(vpair, ATT_HEAD_DIM, axis=1)
        for gi in range(2):
            g = 2 * pair + gi
            k_lo, k_hi = _half_pair(kpair, kroll, gi, lo)
            v_lo, v_hi = _half_pair(vpair, vroll, gi, lo)
            k_halves = (k_lo.astype(BF16), k_hi.astype(BF16))
            v_halves = (v_lo.astype(BF16), v_hi.astype(BF16))
            qsl = slice(2 * g * LANES, (2 * g + 2) * LANES)
            q2 = jnp.concatenate([q_ref[:, 2 * g * LANES:(2 * g + 1) * LANES],
                                  q_ref[:, (2 * g + 1) * LANES:(2 * g + 2) * LANES]],
                                 axis=0).astype(BF16)
            acc = jnp.zeros((2 * blk, LANES), F32)
            for par in range(2):
                h0, h1 = 4 * g + par, 4 * g + 2 + par
                bias = jnp.concatenate([bias_ref[h0], bias_ref[h1]], axis=0)
                s = jnp.where(ok, _dot_nt(q2, k_halves[par]) * scale + bias, -jnp.inf)
                sink = jnp.where(first, sink_ref[h0], sink_ref[h1])
                m = jnp.maximum(jnp.max(s, axis=-1, keepdims=True), sink)
                p = jnp.exp(s - m)
                den = jnp.sum(p, axis=-1, keepdims=True) + jnp.exp(sink - m)
                acc = acc + _dot((p / den).astype(BF16), v_halves[par])
            o_ref[:, qsl] = jnp.concatenate([acc[:blk], acc[blk:]], axis=1)


def swa_prompt(q, k, v, bias, sinks, batch, seq):
    nb = seq // WINDOW
    cur = lambda b, i: (b * nb + i, 0)
    prev = lambda b, i: (b * nb + jnp.maximum(i - 1, 0), 0)
    kvw = ATT_KV_HEADS * ATT_HEAD_DIM
    return pl.pallas_call(
        _swa_prompt_kernel,
        grid=(batch, nb),
        in_specs=[pl.BlockSpec(memory_space=pltpu.SMEM),
                  pl.BlockSpec((WINDOW, D_MODEL), cur),
                  pl.BlockSpec((WINDOW, kvw), cur),
                  pl.BlockSpec((WINDOW, kvw), prev),
                  pl.BlockSpec((WINDOW, kvw), cur),
                  pl.BlockSpec((WINDOW, kvw), prev),
                  pl.BlockSpec((ATT_HEADS, WINDOW, 2 * WINDOW), lambda b, i: (0, 0, 0))],
        out_specs=pl.BlockSpec((WINDOW, D_MODEL), cur),
        out_shape=jax.ShapeDtypeStruct((batch * seq, D_MODEL), F32),
        compiler_params=_params(("parallel", "parallel")),
        name="swa_prompt",
    )(sinks, q, k, k, v, v, bias)


def _ssd_prompt_kernel(xbc_ref, dt_ref, cw_ref, cb_ref, dtb_ref, alog_ref, dskip_ref, e_ref,
                       y_ref, st_ref, state_scr, tail_scr):
    c = pl.program_id(1)
    last = pl.num_programs(1) - 1

    @pl.when(c == 0)
    def _():
        state_scr[...] = jnp.zeros(state_scr.shape, F32)
        tail_scr[...] = jnp.zeros(tail_scr.shape, F32)

    x = xbc_ref[...]
    tail = tail_scr[...]
    row8 = lax.broadcasted_iota(jnp.int32, tail.shape, 0)
    acc = x * cw_ref[CONV_WIDTH - 1:CONV_WIDTH, :]
    for s in range(1, CONV_WIDTH):
        xr = pltpu.roll(x, s, axis=0)
        tr = pltpu.roll(tail, s, axis=0)
        head = jnp.where(row8 < s, tr, xr[0:8])
        shifted = jnp.concatenate([head, xr[8:]], axis=0)
        acc = acc + shifted * cw_ref[CONV_WIDTH - 1 - s:CONV_WIDTH - s, :]
    tail_scr[...] = x[CHUNK - 8:CHUNK]
    u = acc + cb_ref[...]
    u = u * _sigmoid(u)
    xs = u[:, :SSD_WIDTH]
    bm = u[:, SSD_WIDTH:SSD_WIDTH + SSD_GROUPS * D_STATE]
    cm = u[:, SSD_WIDTH + SSD_GROUPS * D_STATE:]

    dtv = _softplus(dt_ref[...] + dtb_ref[...])
    a = dtv * (-jnp.exp(alog_ref[...]))
    r = lax.broadcasted_iota(jnp.int32, (CHUNK, CHUNK), 0)
    cc = lax.broadcasted_iota(jnp.int32, (CHUNK, CHUNK), 1)
    causal = r >= cc
    tril = jnp.where(causal, 1.0, 0.0).astype(BF16)
    cs = _exact_dot_left(tril, a)
    cs_t = cs.T
    dte = jnp.exp(cs[CHUNK - 1:CHUNK, :] - cs)
    ecs = jnp.exp(cs)
    e01 = e_ref[...]
    dt_x = _exact_dot(dtv, e01)
    dte_x = _exact_dot(dte, e01)
    ecs_x = _exact_dot(ecs, e01)
    xdt = xs * dt_x
    xdec = (xdt * dte_x).astype(BF16)
    xdt_b = xdt.astype(BF16)
    lo = lax.broadcasted_iota(jnp.int32, (CHUNK, LANES), 1) < SSD_HEAD_DIM
    dsk = dskip_ref[...]
    for g in range(SSD_GROUPS):
        gs = slice(g * GROUP_W, (g + 1) * GROUP_W)
        bg = bm[:, g * D_STATE:(g + 1) * D_STATE]
        cg = cm[:, g * D_STATE:(g + 1) * D_STATE].astype(BF16)
        cb = _dot_nt(cg, bg.astype(BF16))
        bgt = bg.T.astype(BF16)
        prev = state_scr[g]
        y_off = _dot(cg, prev.astype(BF16)) * ecs_x[:, gs]
        state_scr[g] = ecs_x[CHUNK - 1:CHUNK, gs] * prev + _dot(bgt, xdec[:, gs])
        for m in range(GROUP_W // LANES):
            ps = slice(g * GROUP_W + m * LANES, g * GROUP_W + (m + 1) * LANES)
            xp = xdt_b[:, ps]
            ydiag = jnp.zeros((CHUNK, LANES), F32)
            for par in range(2):
                j = g * (SSD_HEADS // SSD_GROUPS) + 2 * m + par
                diff = cs[:, j:j + 1] - cs_t[j:j + 1, :]
                ldec = jnp.exp(jnp.where(causal, diff, -jnp.inf))
                mm = (cb * ldec).astype(BF16)
                xpm = jnp.where(lo, xp, 0.0) if par == 0 else jnp.where(lo, 0.0, xp)
                ydiag = ydiag + _dot(mm, xpm.astype(BF16))
            y_ref[:, ps] = ydiag + y_off[:, m * LANES:(m + 1) * LANES] + dsk[:, ps] * xs[:, ps]

    @pl.when(c == last)
    def _():
        for g in range(SSD_GROUPS):
            st_ref[0, g * GROUP_W:(g + 1) * GROUP_W, :] = state_scr[g].T


def ssd_prompt(xbc, dt_raw, cw, cb, dtb, alog, dskip_x, e01, batch, seq):
    nc = seq // CHUNK
    cur = lambda b, c: (b * nc + c, 0)
    const = lambda b, c: (0, 0)
    return pl.pallas_call(
        _ssd_prompt_kernel,
        grid=(batch, nc),
        in_specs=[pl.BlockSpec((CHUNK, CONV_DIM), cur),
                  pl.BlockSpec((CHUNK, HEAD_PAD), cur),
                  pl.BlockSpec((CONV_WIDTH, CONV_DIM), const),
                  pl.BlockSpec((1, CONV_DIM), const),
                  pl.BlockSpec((1, HEAD_PAD), const),
                  pl.BlockSpec((1, HEAD_PAD), const),
                  pl.BlockSpec((1, SSD_WIDTH), const),
                  pl.BlockSpec((HEAD_PAD, SSD_WIDTH), const)],
        out_specs=[pl.BlockSpec((CHUNK, SSD_WIDTH), cur),
                   pl.BlockSpec((1, SSD_WIDTH, D_STATE), lambda b, c: (b, 0, 0))],
        out_shape=[jax.ShapeDtypeStruct((batch * seq, SSD_WIDTH), F32),
                   jax.ShapeDtypeStruct((batch, SSD_WIDTH, D_STATE), F32)],
        scratch_shapes=[pltpu.VMEM((SSD_GROUPS, D_STATE, GROUP_W), F32),
                        pltpu.VMEM((8, CONV_DIM), F32)],
        compiler_params=_params(("parallel", "arbitrary")),
        name="ssd_prompt",
    )(xbc, dt_raw, cw, cb, dtb, alog, dskip_x, e01)


def _swa_sample_kernel(qx_ref, ck_ref, cv_ref, kn_ref, vn_ref, bias_ref, sink_ref,
                       nk_ref, nv_ref, o_ref, *, bt):
    scale = ATT_HEAD_DIM ** -0.5
    kvw = ATT_KV_HEADS * ATT_HEAD_DIM
    rowi = lax.broadcasted_iota(jnp.int32, (WINDOW, kvw), 0)
    bias_c = bias_ref[:, 0:WINDOW]
    bias_n = bias_ref[:, WINDOW:WINDOW + 1]
    sink = sink_ref[:, 0:1]
    for bb in range(bt):
        kc = ck_ref[bb]
        vc = cv_ref[bb]
        kn = kn_ref[bb]
        vn = vn_ref[bb]
        qx = qx_ref[bb]
        s_c = _dot_nt(qx.astype(BF16), kc.astype(BF16)) * scale + bias_c
        s_n = jnp.sum(qx * kn, axis=1, keepdims=True) * scale + bias_n
        m = jnp.maximum(jnp.maximum(jnp.max(s_c, axis=1, keepdims=True), s_n), sink)
        p_c = jnp.exp(s_c - m)
        p_n = jnp.exp(s_n - m)
        den = jnp.sum(p_c, axis=1, keepdims=True) + p_n + jnp.exp(sink - m)
        o_ref[bb] = _dot((p_c / den).astype(BF16), vc.astype(BF16)) + (p_n / den) * vn
        nk_ref[bb] = jnp.where(rowi == WINDOW - 1, kn, pltpu.roll(kc, WINDOW - 1, axis=0))
        nv_ref[bb] = jnp.where(rowi == WINDOW - 1, vn, pltpu.roll(vc, WINDOW - 1, axis=0))


def swa_sample(qx, ck, cv, kn, vn, bias_s, sink_b, bt=8):
    nb = qx.shape[0]
    kvw = ATT_KV_HEADS * ATT_HEAD_DIM
    blk3 = lambda i: (i, 0, 0)
    return pl.pallas_call(
        functools.partial(_swa_sample_kernel, bt=bt),
        grid=(nb // bt,),
        in_specs=[pl.BlockSpec((bt, ATT_HEADS, kvw), blk3),
                  pl.BlockSpec((bt, WINDOW, kvw), blk3),
                  pl.BlockSpec((bt, WINDOW, kvw), blk3),
                  pl.BlockSpec((bt, 1, kvw), blk3),
                  pl.BlockSpec((bt, 1, kvw), blk3),
                  pl.BlockSpec((ATT_HEADS, 2 * WINDOW), lambda i: (0, 0)),
                  pl.BlockSpec((ATT_HEADS, LANES), lambda i: (0, 0))],
        out_specs=[pl.BlockSpec((bt, WINDOW, kvw), blk3),
                   pl.BlockSpec((bt, WINDOW, kvw), blk3),
                   pl.BlockSpec((bt, ATT_HEADS, kvw), blk3)],
        out_shape=[jax.ShapeDtypeStruct((nb, WINDOW, kvw), F32),
                   jax.ShapeDtypeStruct((nb, WINDOW, kvw), F32),
                   jax.ShapeDtypeStruct((nb, ATT_HEADS, kvw), F32)],
        compiler_params=_params(("parallel",)),
        name="swa_sample",
    )(qx, ck, cv, kn, vn, bias_s, sink_b)


def _ssd_sample_prep_kernel(xbc_ref, c0_ref, c1_ref, c2_ref, dt_ref, cw_ref, cb_ref, dtb_ref,
                            alog_ref, e_ref, dec_t_ref, xdt_t_ref, b_ref, c_ref, xs_ref):
    u = (c0_ref[...] * cw_ref[0:1, :] + c1_ref[...] * cw_ref[1:2, :]
         + c2_ref[...] * cw_ref[2:3, :] + xbc_ref[...] * cw_ref[3:4, :]) + cb_ref[...]
    u = u * _sigmoid(u)
    xs = u[:, :SSD_WIDTH]
    b_ref[...] = u[:, SSD_WIDTH:SSD_WIDTH + SSD_GROUPS * D_STATE]
    c_ref[...] = u[:, SSD_WIDTH + SSD_GROUPS * D_STATE:]
    xs_ref[...] = xs
    dtv = _softplus(dt_ref[...] + dtb_ref[...])
    dec = jnp.exp(dtv * (-jnp.exp(alog_ref[...])))
    e01 = e_ref[...]
    dec_t_ref[...] = _exact_dot(dec, e01).T
    xdt_t_ref[...] = (xs * _exact_dot(dtv, e01)).T


def ssd_sample_prep(xbc, c0, c1, c2, dt_raw, cw, cb, dtb, alog, e01):
    nb = xbc.shape[0]
    args = (xbc, c0, c1, c2, dt_raw, cw, cb, dtb, alog, e01)
    full = lambda a: pl.BlockSpec(a.shape, lambda i: (0,) * a.ndim)
    out_shape = [jax.ShapeDtypeStruct((SSD_WIDTH, nb), F32),
                 jax.ShapeDtypeStruct((SSD_WIDTH, nb), F32),
                 jax.ShapeDtypeStruct((nb, SSD_GROUPS * D_STATE), F32),
                 jax.ShapeDtypeStruct((nb, SSD_GROUPS * D_STATE), F32),
                 jax.ShapeDtypeStruct((nb, SSD_WIDTH), F32)]
    return pl.pallas_call(
        _ssd_sample_prep_kernel,
        grid=(1,),
        in_specs=[full(a) for a in args],
        out_specs=[full(s) for s in out_shape],
        out_shape=out_shape,
        compiler_params=_params(("arbitrary",)),
        name="ssd_sample_prep",
    )(*args)


def _ssd_sample_step_kernel(st_ref, dec_t_ref, xdt_t_ref, b_ref, c_ref, xs_ref, dskip_ref,
                            ns_ref, y_ref, yt_scr, *, bt):
    i = pl.program_id(0)
    nb = dec_t_ref.shape[1]

    @pl.when(i == 0)
    def _():
        yt_scr[...] = jnp.zeros(yt_scr.shape, F32)

    dparts = _split3(dec_t_ref[...])
    xparts = _split3(xdt_t_ref[...])
    lane = lax.broadcasted_iota(jnp.int32, (GROUP_W, nb), 1)
    for bb in range(bt):
        b = i * bt + bb
        pick = jnp.where(lax.broadcasted_iota(jnp.int32, (nb, D_STATE), 0) == b,
                         1.0, 0.0).astype(BF16)
        dcol = sum(_dot(p, pick) for p in dparts)
        xcol = sum(_dot(p, pick) for p in xparts)
        brow = b_ref[pl.ds(b, 1), :]
        crow = c_ref[pl.ds(b, 1), :]
        for g in range(SSD_GROUPS):
            gs = slice(g * GROUP_W, (g + 1) * GROUP_W)
            ns = slice(g * D_STATE, (g + 1) * D_STATE)
            hn = dcol[gs] * st_ref[bb, gs, :] + xcol[gs] * brow[:, ns]
            ns_ref[bb, gs, :] = hn
            ycol = jnp.sum(hn * crow[:, ns], axis=1, keepdims=True)
            yt_scr[gs, :] = jnp.where(lane == b, ycol, yt_scr[gs, :])

    @pl.when(i == pl.num_programs(0) - 1)
    def _():
        y_ref[...] = yt_scr[...].T + dskip_ref[...] * xs_ref[...]


def ssd_sample_step(state, dec_t, xdt_t, bmat, cmat, xs, dskip_x, bt=4):
    nb = state.shape[0]
    assert nb == D_STATE
    const2 = lambda i: (0, 0)
    return pl.pallas_call(
        functools.partial(_ssd_sample_step_kernel, bt=bt),
        grid=(nb // bt,),
        in_specs=[pl.BlockSpec((bt, SSD_WIDTH, D_STATE), lambda i: (i, 0, 0)),
                  pl.BlockSpec((SSD_WIDTH, nb), const2),
                  pl.BlockSpec((SSD_WIDTH, nb), const2),
                  pl.BlockSpec((nb, SSD_GROUPS * D_STATE), const2),
                  pl.BlockSpec((nb, SSD_GROUPS * D_STATE), const2),
                  pl.BlockSpec((nb, SSD_WIDTH), const2),
                  pl.BlockSpec((1, SSD_WIDTH), const2)],
        out_specs=[pl.BlockSpec((bt, SSD_WIDTH, D_STATE), lambda i: (i, 0, 0)),
                   pl.BlockSpec((nb, SSD_WIDTH), const2)],
        out_shape=[jax.ShapeDtypeStruct((nb, SSD_WIDTH, D_STATE), F32),
                   jax.ShapeDtypeStruct((nb, SSD_WIDTH), F32)],
        scratch_shapes=[pltpu.VMEM((SSD_WIDTH, nb), F32)],
        compiler_params=_params(("arbitrary",)),
        name="ssd_sample_step",
    )(state, dec_t, xdt_t, bmat, cmat, xs, dskip_x)


def _merge_kernel(x_ref, o_ref, y_ref, z_ref, anw_ref, snw_ref, w1_ref, w2_ref, out_ref):
    ya = _rms(o_ref[...], anw_ref[...]).astype(BF16)
    z = z_ref[...]
    ys = _rms(y_ref[...] * (z * _sigmoid(z)), snw_ref[...]).astype(BF16)
    out_ref[...] = x_ref[...] + _dot(ya, w1_ref[...]) + _dot(ys, w2_ref[...])


def merge(x, o_att, y_ssd, z, anw, snw, w1, w2, tm):
    t, d = x.shape
    row = pl.BlockSpec((tm, d), lambda i: (i, 0))
    vec = pl.BlockSpec((1, d), lambda i: (0, 0))
    mat = pl.BlockSpec((d, d), lambda i: (0, 0))
    return pl.pallas_call(
        _merge_kernel,
        grid=(t // tm,),
        in_specs=[row, row, row, row, vec, vec, mat, mat],
        out_specs=row,
        out_shape=jax.ShapeDtypeStruct((t, d), F32),
        compiler_params=_params(("parallel",)),
        name="merge",
    )(x, o_att, y_ssd, z, anw.reshape(1, d), snw.reshape(1, d), w1, w2)


def _mem_prompt_kernel(x_ref, nw_ref, wq_ref, mk_ref, mv_ref, wo_ref, out_ref):
    x = x_ref[...]
    q = _dot(_rms(x, nw_ref[...]).astype(BF16), wq_ref[...])
    scale = MEM_HEAD_DIM ** -0.5
    outs = []
    for h in range(MEM_HEADS):
        hs = slice(h * MEM_HEAD_DIM, (h + 1) * MEM_HEAD_DIM)
        s = _dot_nt(q[:, hs].astype(BF16), mk_ref[:, hs]) * scale
        p = jnp.exp(s - jnp.max(s, axis=-1, keepdims=True))
        p = p / jnp.sum(p, axis=-1, keepdims=True)
        outs.append(_dot(p.astype(BF16), mv_ref[:, hs]).astype(BF16))
    out_ref[...] = x + _dot(jnp.concatenate(outs, axis=1), wo_ref[...])


def mem_attn_prompt(x, nw, wq, mk, mv, wo, seq, tm):
    t, d = x.shape
    per = seq // tm
    row = pl.BlockSpec((tm, d), lambda i: (i, 0))
    mat = pl.BlockSpec((d, d), lambda i: (0, 0))
    mem = pl.BlockSpec((N_MEM, d), lambda i: (i // per, 0))
    return pl.pallas_call(
        _mem_prompt_kernel,
        grid=(t // tm,),
        in_specs=[row, pl.BlockSpec((1, d), lambda i: (0, 0)), mat, mem, mem, mat],
        out_specs=row,
        out_shape=jax.ShapeDtypeStruct((t, d), F32),
        compiler_params=_params(("parallel",)),
        name="mem_prompt",
    )(x, nw.reshape(1, d), wq, mk, mv, wo)


def _mem_sample_kernel(q_ref, k_ref, v_ref, o_ref, *, bt):
    rows = N_MEM * MEM_HEADS
    qh = lax.broadcasted_iota(jnp.int32, (8, rows), 0)
    kh = lax.broadcasted_iota(jnp.int32, (8, rows), 1) % MEM_HEADS
    live = qh < MEM_HEADS
    for bb in range(bt):
        q4 = q_ref[bb]
        q8 = jnp.concatenate([q4, jnp.zeros_like(q4)], axis=0).astype(BF16)
        k2 = k_ref[bb].reshape(rows, MEM_HEAD_DIM).astype(BF16)
        v2 = v_ref[bb].reshape(rows, MEM_HEAD_DIM).astype(BF16)
        s = _dot_nt(q8, k2) * (MEM_HEAD_DIM ** -0.5)
        s = jnp.where(kh == qh, s, -jnp.inf)
        p = jnp.exp(s - jnp.where(live, jnp.max(s, axis=1, keepdims=True), 0.0))
        p = p / jnp.where(live, jnp.sum(p, axis=1, keepdims=True), 1.0)
        o_ref[bb] = _dot(p.astype(BF16), v2)[0:MEM_HEADS]


def mem_sample(q, ck, cv, bt=2):
    nb, d = q.shape
    blk = pl.BlockSpec((bt, N_MEM, MEM_HEADS, MEM_HEAD_DIM), lambda i: (i, 0, 0, 0))
    vec = pl.BlockSpec((bt, MEM_HEADS, MEM_HEAD_DIM), lambda i: (i, 0, 0))
    return pl.pallas_call(
        functools.partial(_mem_sample_kernel, bt=bt),
        grid=(nb // bt,),
        in_specs=[vec, blk, blk],
        out_specs=vec,
        out_shape=jax.ShapeDtypeStruct((nb, MEM_HEADS, MEM_HEAD_DIM), F32),
        compiler_params=_params(("parallel",)),
        name="mem_sample",
    )(q.reshape(nb, MEM_HEADS, MEM_HEAD_DIM), ck, cv).reshape(nb, d)


def _matmul_res_kernel(a_ref, w_ref, r_ref, out_ref):
    out_ref[...] = r_ref[...] + _dot(a_ref[...].astype(BF16), w_ref[...])


def matmul_res(a, w, res, tm):
    t, d = res.shape
    kdim = a.shape[1]
    return pl.pallas_call(
        _matmul_res_kernel,
        grid=(t // tm,),
        in_specs=[pl.BlockSpec((tm, kdim), lambda i: (i, 0)),
                  pl.BlockSpec((kdim, d), lambda i: (0, 0)),
                  pl.BlockSpec((tm, d), lambda i: (i, 0))],
        out_specs=pl.BlockSpec((tm, d), lambda i: (i, 0)),
        out_shape=jax.ShapeDtypeStruct((t, d), F32),
        compiler_params=_params(("parallel",)),
        name="matmul_res",
    )(a, w, res)


def _topk_rows(work, k):
    rows = work.shape[0]
    iota = lax.broadcasted_iota(jnp.int32, work.shape, 0)
    rank = jnp.full(work.shape, k, jnp.int32)
    vals = []
    for r in range(k):
        m = jnp.max(work, axis=0, keepdims=True)
        idx = jnp.min(jnp.where(work == m, iota, rows), axis=0, keepdims=True)
        hit = iota == idx
        rank = jnp.where(hit, r, rank)
        work = jnp.where(hit, -jnp.inf, work)
        vals.append(m)
    return jnp.concatenate(vals, axis=0), rank


def _topk_rows_distinct(work, k):
    rank = jnp.full(work.shape, k, jnp.int32)
    vals = []
    for r in range(k):
        m = jnp.max(work, axis=0, keepdims=True)
        hit = work == m
        rank = jnp.where(hit, r, rank)
        work = jnp.where(hit, -jnp.inf, work)
        vals.append(m)
    taken = jnp.sum(jnp.where(rank < k, 1.0, 0.0), axis=0, keepdims=True)
    return jnp.concatenate(vals, axis=0), rank, taken


def _peer_candidates(a, b):
    lo4 = lax.broadcasted_iota(jnp.int32, (8, a.shape[1]), 0) < 4
    b8 = b[0:8]
    b44 = jnp.where(lo4, b8, pltpu.roll(b8, 4, axis=0))
    return jnp.concatenate(
        [a[0:1] + b8, a[0:1] + b[8:16], a[1:2] + b8, a[2:3] + b8, a[3:4] + b8,
         jnp.where(lo4, a[4:5], a[5:6]) + b44, jnp.where(lo4, a[6:7], a[7:8]) + b44,
         a[8:16] + b[0:1]], axis=0)


def _peer_emit(subs, a, b, rank0, rank1, cand, chosen, seli_ref, selj_ref):
    k = PEER_TOPK
    cmax = a[0:1, :] + b[0:1, :]
    z = jnp.sum(jnp.where(chosen, jnp.exp(cand - cmax), 0.0), axis=0, keepdims=True)
    lo4 = lax.broadcasted_iota(jnp.int32, (8, a.shape[1]), 0) < 4
    ch = [jnp.where(chosen[8 * i:8 * i + 8], 1.0, 0.0) for i in range(8)]
    cnt = lambda v: jnp.sum(v, axis=0, keepdims=True)
    n = [cnt(ch[0]) + cnt(ch[1]), cnt(ch[2]), cnt(ch[3]), cnt(ch[4]),
         cnt(jnp.where(lo4, ch[5], 0.0)), cnt(jnp.where(lo4, 0.0, ch[5])),
         cnt(jnp.where(lo4, ch[6], 0.0)), cnt(jnp.where(lo4, 0.0, ch[6]))]
    n += [ch[7][i:i + 1] for i in range(8)]
    lim0 = jnp.zeros(rank0.shape, F32)
    for r in range(k):
        lim0 = jnp.where(rank0 == r, n[r], lim0)
    seli_ref[0, 0] = jnp.exp(subs[0] - a[0:1, :]) / z
    seli_ref[0, 1] = lim0
    selj_ref[0, 0] = pltpu.bitcast(jnp.exp(subs[1] - b[0:1, :]).astype(BF16), jnp.uint32)
    selj_ref[0, 1] = pltpu.bitcast(rank1.astype(F32).astype(BF16), jnp.uint32)


def _peer_select_kernel(pq_ref, keys_ref, seli_ref, selj_ref):
    k = PEER_TOPK
    subs = [_dot_nt(keys_ref[p], pq_ref[:, p * PEER_HALF:(p + 1) * PEER_HALF].astype(BF16))
            for p in range(2)]
    a, rank0, took0 = _topk_rows_distinct(subs[0], k)
    b, rank1, took1 = _topk_rows_distinct(subs[1], k)
    cand = _peer_candidates(a, b)
    _, crank, tookc = _topk_rows_distinct(cand, k)
    _peer_emit(subs, a, b, rank0, rank1, cand, crank < k, seli_ref, selj_ref)
    tied = jnp.where((took0 != k) | (took1 != k) | (tookc != k), 1.0, 0.0)

    @pl.when(jnp.max(tied) > 0.0)
    def _():
        a, rank0 = _topk_rows(subs[0], k)
        b, rank1 = _topk_rows(subs[1], k)
        cand = _peer_candidates(a, b)
        _, crank = _topk_rows(cand, k)
        _peer_emit(subs, a, b, rank0, rank1, cand, crank < k, seli_ref, selj_ref)


def peer_select(pq, keys_bf16, tm):
    t = pq.shape[0]
    spec = lambda rows: pl.BlockSpec((1, 2, rows, tm), lambda i, h: (h, 0, 0, i))
    return pl.pallas_call(
        _peer_select_kernel,
        grid=(t // tm, PEER_HEADS),
        in_specs=[pl.BlockSpec((tm, 2 * PEER_HALF), lambda i, h: (i, h)),
                  pl.BlockSpec((2, N_KEYS, PEER_HALF), lambda i, h: (h, 0, 0))],
        out_specs=[spec(N_KEYS), spec(N_KEYS // 2)],
        out_shape=[jax.ShapeDtypeStruct((PEER_HEADS, 2, N_KEYS, t), F32),
                   jax.ShapeDtypeStruct((PEER_HEADS, 2, N_KEYS // 2, t), jnp.uint32)],
        compiler_params=_params(("parallel", "parallel")),
        name="peer_select",
    )(pq, keys_bf16)


def _gelu(x):
    return 0.5 * x * (1.0 + lax.erf(x * np.float32(math.sqrt(0.5))))


def _peer_dense_kernel(xnt_ref, x_ref, u_ref, vt_ref, seli_ref, selj_ref, fnw_ref, y_ref,
                       acc_scr, act_scr, w_scr, *, tm, ib, sub):
    e = pl.program_id(1)
    ne = pl.num_programs(1)

    @pl.when(e == 0)
    def _():
        acc_scr[...] = jnp.zeros(acc_scr.shape, F32)

    i8 = pl.ds(pl.multiple_of(e * ib, 8), ib)
    nsub = ib // sub
    d = acc_scr.shape[0]
    halves = 2
    sub_rows = lambda k: slice(k * sub * N_KEYS, (k + 1) * sub * N_KEYS)

    def activations(k, half):
        cs = slice(half * tm // halves, (half + 1) * tm // halves)
        act_scr[k % 2, :, cs] = _dot(u_ref[sub_rows(k), :], pltpu.bitcast(xnt_ref[:, cs], BF16))

    def accumulate(k, half):
        rs = slice(half * d // halves, (half + 1) * d // halves)
        acc_scr[rs, :] += _dot(vt_ref[0, rs, sub_rows(k)], w_scr[k % 3])

    def gates(k, lt):
        ls = slice(lt * LANES, (lt + 1) * LANES)
        for j0 in range(0, sub, 2):
            gs = [jnp.zeros((N_KEYS, LANES), BF16) for _ in range(2)]
            for h in range(PEER_HEADS):
                e1 = pltpu.bitcast(selj_ref[h, 0, :, ls], BF16)
                rank1 = pltpu.bitcast(selj_ref[h, 1, :, ls], BF16)
                for t in range(2):
                    ii = k * sub + j0 + t
                    row = lambda c: jnp.broadcast_to(seli_ref[h, c, i8, ls][ii:ii + 1, :],
                                                     (N_KEYS, LANES)).astype(BF16)
                    gs[t] = gs[t] + jnp.where(rank1 < row(1), e1, jnp.zeros_like(e1)) * row(0)
            for t in range(2):
                rs = slice((j0 + t) * N_KEYS, (j0 + t + 1) * N_KEYS)
                w_scr[k % 3, rs, ls] = gs[t] * _gelu(act_scr[k % 2, rs, ls]).astype(BF16)

    for half in range(halves):
        activations(0, half)
    for k in range(nsub):
        mxu_jobs = []
        if k + 1 < nsub:
            mxu_jobs += [functools.partial(activations, k + 1, half) for half in range(halves)]
        if k >= 1:
            mxu_jobs += [functools.partial(accumulate, k - 1, half) for half in range(halves)]
        for job in mxu_jobs:
            job()
        for lt in range(tm // LANES):
            gates(k, lt)
    for half in range(halves):
        accumulate(nsub - 1, half)

    @pl.when(e == ne - 1)
    def _():
        y_ref[...] = _rms(x_ref[...] + acc_scr[...].T, fnw_ref[...])


PEER_BLOCK_ROWS = 16
PEER_SUB_ROWS = 4


def peer_dense(xnt, x, u_bf16, vt_blocks, seli, selj, fnw, tm, ib=PEER_BLOCK_ROWS,
               sub=PEER_SUB_ROWS):
    t, d = x.shape
    eb = ib * N_KEYS
    nblk = N_EXPERTS // eb
    assert vt_blocks.shape == (nblk, d, eb) and xnt.shape == (d // 2, t)
    sel_spec = lambda rows: pl.BlockSpec((PEER_HEADS, 2, rows, tm), lambda i, e: (0, 0, 0, i))
    return pl.pallas_call(
        functools.partial(_peer_dense_kernel, tm=tm, ib=ib, sub=sub),
        grid=(t // tm, nblk),
        in_specs=[pl.BlockSpec((d // 2, tm), lambda i, e: (0, i)),
                  pl.BlockSpec((tm, d), lambda i, e: (i, 0)),
                  pl.BlockSpec((eb, d), lambda i, e: (e, 0)),
                  pl.BlockSpec((1, d, eb), lambda i, e: (e, 0, 0)),
                  sel_spec(N_KEYS), sel_spec(N_KEYS // 2),
                  pl.BlockSpec((1, d), lambda i, e: (0, 0))],
        out_specs=pl.BlockSpec((tm, d), lambda i, e: (i, 0)),
        out_shape=jax.ShapeDtypeStruct((t, d), F32),
        scratch_shapes=[pltpu.VMEM((d, tm), F32),
                        pltpu.VMEM((2, sub * N_KEYS, tm), F32),
                        pltpu.VMEM((3, sub * N_KEYS, tm), BF16)],
        compiler_params=_params(("parallel", "arbitrary")),
        name="peer_dense",
    )(xnt, x, u_bf16, vt_blocks, seli, selj, fnw.reshape(1, d))


def _tail(x, mk, mv, prm, seq, tm, mem_is_shared, tm_peer):
    if mem_is_shared:
        x = mem_attn_prompt(x, prm["norm_mem_w"], prm["w_mq"], mk, mv, prm["w_mo"], seq, tm)
    else:
        (q,) = norm_matmul(x, prm["norm_mem_w"], prm["w_mq"], (D_MODEL,), tm)
        x = matmul_res(mem_sample(q, mk, mv), prm["w_mo"], x, tm)
    xnt, pq = norm_matmul(x, prm["norm_ffn_w"], prm["peer_wq"], (prm["peer_wq"].shape[1],), tm,
                          emit_ht=True)
    seli, selj = peer_select(pq, prm["peer_keys"], tm)
    return peer_dense(xnt, x, prm["peer_u"], prm["peer_vt"], seli, selj, prm["final_norm_w"],
                      tm_peer)


def kernel(x_prompt, x_sample, mem_prompt, cache_swa_k, cache_swa_v, state_ssm, state_conv, cache_mem_k, cache_mem_v, norm_mix_w, w_in, attn_sinks, rel_bias, attn_norm_w, conv_w, conv_b, dt_bias, a_log, d_skip, ssd_norm_w, w_out, norm_mem_w, mem_norm_w, w_mq, w_mk, w_mv, w_mo, norm_ffn_w, peer_wq, peer_keys, peer_u, peer_v, final_norm_w):
    assert w_in.shape[0] == 1, "single-layer step"
    batch, seq, d = x_prompt.shape
    nb = x_sample.shape[0]
    kvw = ATT_KV_HEADS * ATT_HEAD_DIM
    tm = 512

    n_main = D_MODEL + 2 * kvw + SSD_WIDTH + CONV_DIM
    w_in_p = jnp.concatenate(
        [w_in[0], jnp.zeros((d, HEAD_PAD - SSD_HEADS), F32)], axis=1).astype(BF16)
    in_splits = (D_MODEL, kvw, kvw, SSD_WIDTH, CONV_DIM, HEAD_PAD)
    assert n_main + SSD_HEADS == w_in.shape[2]
    pad_h = lambda v: jnp.pad(v.reshape(1, SSD_HEADS), ((0, 0), (0, HEAD_PAD - SSD_HEADS)))
    dtb, alog = pad_h(dt_bias[0]), pad_h(a_log[0])
    dskip_x = jnp.repeat(d_skip[0], SSD_HEAD_DIM).reshape(1, SSD_WIDTH)
    e01 = (jnp.arange(HEAD_PAD)[:, None] == jnp.arange(SSD_WIDTH)[None, :] // SSD_HEAD_DIM
           ).astype(BF16)
    cw, cb = conv_w[0], conv_b[0].reshape(1, CONV_DIM)
    w_out1 = w_out[0, :D_MODEL].astype(BF16)
    w_out2 = w_out[0, D_MODEL:].astype(BF16)
    prm = dict(norm_mem_w=norm_mem_w[0], w_mq=w_mq[0].astype(BF16), w_mo=w_mo[0].astype(BF16),
               norm_ffn_w=norm_ffn_w[0], peer_wq=peer_wq[0].astype(BF16),
               peer_keys=peer_keys[0].reshape(2 * PEER_HEADS, N_KEYS, PEER_HALF).astype(BF16),
               peer_u=peer_u[0].astype(BF16),
               peer_vt=jnp.swapaxes(peer_v[0].astype(BF16).reshape(
                   N_EXPERTS // (PEER_BLOCK_ROWS * N_KEYS), PEER_BLOCK_ROWS * N_KEYS, d), 1, 2),
               final_norm_w=final_norm_w)

    qi = jnp.arange(WINDOW)[:, None] + WINDOW
    bias_p = rel_bias_rows(rel_bias, _t5_bucket(qi - jnp.arange(2 * WINDOW)[None, :]))
    dist_s = jnp.broadcast_to(jnp.maximum(WINDOW - jnp.arange(2 * WINDOW), 0)[None, :],
                              (8, 2 * WINDOW))
    bias_s = rel_bias_rows(rel_bias, _t5_bucket(dist_s))[:, 0, :]
    sinks = attn_sinks[0]

    xp = x_prompt.reshape(batch * seq, d)
    q, k, v, z, xbc, dt_raw = norm_matmul(xp, norm_mix_w[0], w_in_p, in_splits, tm)
    o_att = swa_prompt(q, k, v, bias_p, sinks, batch, seq)
    y_ssd, p_state = ssd_prompt(xbc, dt_raw, cw, cb, dtb, alog, dskip_x, e01, batch, seq)
    x1 = merge(xp, o_att, y_ssd, z, attn_norm_w[0], ssd_norm_w[0], w_out1, w_out2, tm)
    mkv_w = jnp.concatenate([w_mk[0], w_mv[0]], axis=1).astype(BF16)
    mk, mv = norm_matmul(mem_prompt.reshape(batch * N_MEM, d), mem_norm_w[0], mkv_w, (d, d), tm)
    y_p = _tail(x1, mk.astype(BF16), mv.astype(BF16), prm, seq, tm, True, tm)

    xs_in = x_sample.reshape(nb, d)
    tms = nb
    qs, ks, vs, zs, xbcs, dts = norm_matmul(xs_in, norm_mix_w[0], w_in_p, in_splits, tms)
    own = (jnp.arange(kvw)[None, :] // ATT_HEAD_DIM) == (jnp.arange(ATT_HEADS)[:, None] // (ATT_HEADS // ATT_KV_HEADS))
    qx = jnp.where(own[None], jnp.tile(qs.reshape(nb, ATT_HEADS, ATT_HEAD_DIM), (1, 1, ATT_KV_HEADS)), 0.0)
    ck = cache_swa_k[0].reshape(nb, WINDOW, kvw)
    cv = cache_swa_v[0].reshape(nb, WINDOW, kvw)
    sink_b = jnp.broadcast_to(sinks[:, None], (ATT_HEADS, LANES))
    nk, nv, ox = swa_sample(qx, ck, cv, ks.reshape(nb, 1, kvw), vs.reshape(nb, 1, kvw), bias_s, sink_b)
    ox5 = ox.reshape(nb, ATT_KV_HEADS, ATT_HEADS // ATT_KV_HEADS, ATT_KV_HEADS, ATT_HEAD_DIM)
    o_att_s = jnp.stack([ox5[:, g, :, g, :] for g in range(ATT_KV_HEADS)], axis=1).reshape(nb, d)
    sc = state_conv[0]
    dec_t, xdt_t, bmat, cmat, xs_s = ssd_sample_prep(xbcs, sc[:, 0], sc[:, 1], sc[:, 2], dts, cw, cb, dtb, alog, e01)
    new_state, y_s = ssd_sample_step(state_ssm[0].reshape(nb, SSD_WIDTH, D_STATE), dec_t, xdt_t, bmat, cmat, xs_s, dskip_x)
    x1s = merge(xs_in, o_att_s, y_s, zs, attn_norm_w[0], ssd_norm_w[0], w_out1, w_out2, tms)
    y_s_out = _tail(x1s, cache_mem_k[0], cache_mem_v[0], prm, seq, tms, False, tms)

    k4 = k.reshape(batch, seq, ATT_KV_HEADS, ATT_HEAD_DIM)
    v4 = v.reshape(batch, seq, ATT_KV_HEADS, ATT_HEAD_DIM)
    xbc3 = xbc.reshape(batch, seq, CONV_DIM)
    return (y_p.reshape(batch, seq, d),
            y_s_out.reshape(nb, 1, d),
            k4[None, :, seq - WINDOW:],
            v4[None, :, seq - WINDOW:],
            p_state.reshape(1, batch, SSD_HEADS, SSD_HEAD_DIM, D_STATE),
            xbc3[None, :, seq - (CONV_WIDTH - 1):],
            mk.reshape(1, batch, N_MEM, MEM_HEADS, MEM_HEAD_DIM),
            mv.reshape(1, batch, N_MEM, MEM_HEADS, MEM_HEAD_DIM),
            nk.reshape(1, nb, WINDOW, ATT_KV_HEADS, ATT_HEAD_DIM),
            nv.reshape(1, nb, WINDOW, ATT_KV_HEADS, ATT_HEAD_DIM),
            new_state.reshape(1, nb, SSD_HEADS, SSD_HEAD_DIM, D_STATE),
            jnp.concatenate([sc[:, 1:], xbcs[:, None, :]], axis=1)[None])
```

```python
import functools
import math

import numpy as np
import jax
import jax.numpy as jnp
from jax import lax
from jax.experimental import pallas as pl
from jax.experimental.pallas import tpu as pltpu

F32 = jnp.float32
BF16 = jnp.bfloat16
EPS = 1e-6

D_MODEL = 1024
ATT_HEADS = 16
ATT_KV_HEADS = 4
ATT_HEAD_DIM = 64
WINDOW = 128
N_BUCKETS = 32
MAX_DISTANCE = 128
SSD_HEADS = 16
SSD_HEAD_DIM = 64
SSD_WIDTH = SSD_HEADS * SSD_HEAD_DIM
SSD_GROUPS = 2
D_STATE = 128
CONV_WIDTH = 4
CONV_DIM = SSD_WIDTH + 2 * SSD_GROUPS * D_STATE
CHUNK = 128
N_MEM = 256
MEM_HEADS = 4
MEM_HEAD_DIM = D_MODEL // MEM_HEADS
PEER_HEADS = 8
N_KEYS = 128
N_EXPERTS = N_KEYS * N_KEYS
PEER_TOPK = 16
PEER_HALF = 128

LANES = 128
HEAD_PAD = 128
GROUP_W = SSD_WIDTH // SSD_GROUPS
VMEM_LIMIT = 56 * 1024 * 1024


def _params(sem):
    return pltpu.CompilerParams(dimension_semantics=sem, vmem_limit_bytes=VMEM_LIMIT)


def _dot(a, b):
    return jnp.dot(a, b, preferred_element_type=F32)


def _dot_nt(a, b):
    return lax.dot_general(a, b, (((1,), (1,)), ((), ())), preferred_element_type=F32)


def _rms(x, w):
    var = jnp.mean(x * x, axis=-1, keepdims=True)
    return x * lax.rsqrt(var + EPS) * w


def _sigmoid(x):
    return 1.0 / (1.0 + jnp.exp(-x))


def _softplus(x):
    return jnp.maximum(x, 0.0) + jnp.log1p(jnp.exp(-jnp.abs(x)))


def _split3(v):
    v1 = v.astype(BF16)
    r1 = v - v1.astype(F32)
    v2 = r1.astype(BF16)
    r2 = r1 - v2.astype(F32)
    return v1, v2, r2.astype(BF16)


def _exact_dot(v, m01):
    v1, v2, v3 = _split3(v)
    return _dot(v1, m01) + _dot(v2, m01) + _dot(v3, m01)


def _exact_dot_left(m01, v):
    v1, v2, v3 = _split3(v)
    return _dot(m01, v1) + _dot(m01, v2) + _dot(m01, v3)


def _norm_matmul_kernel(x_ref, nw_ref, w_ref, *out_refs, splits, emit_ht):
    hf = _rms(x_ref[...], nw_ref[...])
    h = hf.astype(BF16)
    refs = list(out_refs)
    if emit_ht:
        refs.pop(0)[...] = pltpu.bitcast(hf.T.astype(BF16), jnp.uint32)
    off = 0
    for o_ref, n in zip(refs, splits):
        o_ref[...] = _dot(h, w_ref[:, off:off + n])
        off += n


def norm_matmul(x, nw, w_bf16, splits, tm, emit_ht=False):
    t, d = x.shape
    n = w_bf16.shape[1]
    assert sum(splits) == n and t % tm == 0
    out_shape = [jax.ShapeDtypeStruct((t, s), F32) for s in splits]
    out_specs = [pl.BlockSpec((tm, s), lambda i: (i, 0)) for s in splits]
    if emit_ht:
        out_shape.insert(0, jax.ShapeDtypeStruct((d // 2, t), jnp.uint32))
        out_specs.insert(0, pl.BlockSpec((d // 2, tm), lambda i: (0, i)))
    return pl.pallas_call(
        functools.partial(_norm_matmul_kernel, splits=tuple(splits), emit_ht=emit_ht),
        grid=(t // tm,),
        in_specs=[pl.BlockSpec((tm, d), lambda i: (i, 0)),
                  pl.BlockSpec((1, d), lambda i: (0, 0)),
                  pl.BlockSpec((d, n), lambda i: (0, 0))],
        out_specs=out_specs,
        out_shape=out_shape,
        compiler_params=_params(("parallel",)),
        name="norm_matmul",
    )(x, nw.reshape(1, d), w_bf16)


def _bias_kernel(table_ref, bucket_ref, out_ref):
    h = pl.program_id(0)
    bk = bucket_ref[...]
    acc = jnp.zeros(bk.shape, F32)
    for b in range(N_BUCKETS):
        acc = jnp.where(bk == b, table_ref[b, h], acc)
    out_ref[0] = acc


def rel_bias_rows(table, bucket):
    r, c = bucket.shape
    return pl.pallas_call(
        _bias_kernel,
        grid=(ATT_HEADS,),
        in_specs=[pl.BlockSpec(memory_space=pltpu.SMEM),
                  pl.BlockSpec((r, c), lambda h: (0, 0))],
        out_specs=pl.BlockSpec((1, r, c), lambda h: (h, 0, 0)),
        out_shape=jax.ShapeDtypeStruct((ATT_HEADS, r, c), F32),
        compiler_params=_params(("parallel",)),
        name="rel_bias",
    )(table, bucket)


def _t5_bucket(dist):
    n = jnp.maximum(dist, 0)
    max_exact = N_BUCKETS // 2
    nf = jnp.maximum(n, 1).astype(F32)
    large = max_exact + (jnp.log(nf / max_exact) / math.log(MAX_DISTANCE / max_exact)
                         * (N_BUCKETS - max_exact)).astype(jnp.int32)
    large = jnp.minimum(large, N_BUCKETS - 1)
    return jnp.where(n < max_exact, n, large)


def _half_pair(pair, rolled, which, lo):
    if which == 0:
        return jnp.where(lo, pair, 0.0), jnp.where(lo, 0.0, rolled)
    return jnp.where(lo, rolled, 0.0), jnp.where(lo, 0.0, pair)


def _swa_prompt_kernel(sink_ref, q_ref, kc_ref, kp_ref, vc_ref, vp_ref, bias_ref, o_ref):
    i = pl.program_id(1)
    blk = WINDOW
    row = lax.broadcasted_iota(jnp.int32, (blk, 2 * blk), 0)
    col = lax.broadcasted_iota(jnp.int32, (blk, 2 * blk), 1)
    dist = row + blk - col
    ok1 = (dist >= 0) & (dist <= WINDOW) & ((col >= blk) | (i > 0))
    ok = jnp.concatenate([ok1, ok1], axis=0)
    first = lax.broadcasted_iota(jnp.int32, (2 * blk, 1), 0) < blk
    lo = lax.broadcasted_iota(jnp.int32, (2 * blk, LANES), 1) < ATT_HEAD_DIM
    scale = ATT_HEAD_DIM ** -0.5
    for pair in range(ATT_KV_HEADS // 2):
        sl = slice(pair * LANES, (pair + 1) * LANES)
        kpair = jnp.concatenate([kp_ref[:, sl], kc_ref[:, sl]], axis=0)
        vpair = jnp.concatenate([vp_ref[:, sl], vc_ref[:, sl]], axis=0)
        kroll = pltpu.roll(kpair, ATT_HEAD_DIM, axis=1)
        vroll = pltpu.roll(vpair, ATT_HEAD_DIM, axis=1)
        for gi in range(2):
            g = 2 * pair + gi
            k_lo, k_hi = _half_pair(kpair, kroll, gi, lo)
            v_lo, v_hi = _half_pair(vpair, vroll, gi, lo)
            k_halves = (k_lo.astype(BF16), k_hi.astype(BF16))
            v_halves = (v_lo.astype(BF16), v_hi.astype(BF16))
            qsl = slice(2 * g * LANES, (2 * g + 2) * LANES)
            q2 = jnp.concatenate([q_ref[:, 2 * g * LANES:(2 * g + 1) * LANES],
                                  q_ref[:, (2 * g + 1) * LANES:(2 * g + 2) * LANES]],
                                 axis=0).astype(BF16)
            acc = jnp.zeros((2 * blk, LANES), F32)
            for par in range(2):
                h0, h1 = 4 * g + par, 4 * g + 2 + par
                bias = jnp.concatenate([bias_ref[h0], bias_ref[h1]], axis=0)
                s = jnp.where(ok, _dot_nt(q2, k_halves[par]) * scale + bias, -jnp.inf)
                sink = jnp.where(first, sink_ref[h0], sink_ref[h1])
                m = jnp.maximum(jnp.max(s, axis=-1, keepdims=True), sink)
                p = jnp.exp(s - m)
                den = jnp.sum(p, axis=-1, keepdims=True) + jnp.exp(sink - m)
                acc = acc + _dot((p / den).astype(BF16), v_halves[par])
            o_ref[:, qsl] = jnp.concatenate([acc[:blk], acc[blk:]], axis=1)


def swa_prompt(q, k, v, bias, sinks, batch, seq):
    nb = seq // WINDOW
    cur = lambda b, i: (b * nb + i, 0)
    prev = lambda b, i: (b * nb + jnp.maximum(i - 1, 0), 0)
    kvw = ATT_KV_HEADS * ATT_HEAD_DIM
    return pl.pallas_call(
        _swa_prompt_kernel,
        grid=(batch, nb),
        in_specs=[pl.BlockSpec(memory_space=pltpu.SMEM),
                  pl.BlockSpec((WINDOW, D_MODEL), cur),
                  pl.BlockSpec((WINDOW, kvw), cur),
                  pl.BlockSpec((WINDOW, kvw), prev),
                  pl.BlockSpec((WINDOW, kvw), cur),
                  pl.BlockSpec((WINDOW, kvw), prev),
                  pl.BlockSpec((ATT_HEADS, WINDOW, 2 * WINDOW), lambda b, i: (0, 0, 0))],
        out_specs=pl.BlockSpec((WINDOW, D_MODEL), cur),
        out_shape=jax.ShapeDtypeStruct((batch * seq, D_MODEL), F32),
        compiler_params=_params(("parallel", "parallel")),
        name="swa_prompt",
    )(sinks, q, k, k, v, v, bias)


def _ssd_prompt_kernel(xbc_ref, dt_ref, cw_ref, cb_ref, dtb_ref, alog_ref, dskip_ref, e_ref,
                       y_ref, st_ref, state_scr, tail_scr):
    c = pl.program_id(1)
    last = pl.num_programs(1) - 1

    @pl.when(c == 0)
    def _():
        state_scr[...] = jnp.zeros(state_scr.shape, F32)
        tail_scr[...] = jnp.zeros(tail_scr.shape, F32)

    x = xbc_ref[...]
    tail = tail_scr[...]
    row8 = lax.broadcasted_iota(jnp.int32, tail.shape, 0)
    acc = x * cw_ref[CONV_WIDTH - 1:CONV_WIDTH, :]
    for s in range(1, CONV_WIDTH):
        xr = pltpu.roll(x, s, axis=0)
        tr = pltpu.roll(tail, s, axis=0)
        head = jnp.where(row8 < s, tr, xr[0:8])
        shifted = jnp.concatenate([head, xr[8:]], axis=0)
        acc = acc + shifted * cw_ref[CONV_WIDTH - 1 - s:CONV_WIDTH - s, :]
    tail_scr[...] = x[CHUNK - 8:CHUNK]
    u = acc + cb_ref[...]
    u = u * _sigmoid(u)
    xs = u[:, :SSD_WIDTH]
    bm = u[:, SSD_WIDTH:SSD_WIDTH + SSD_GROUPS * D_STATE]
    cm = u[:, SSD_WIDTH + SSD_GROUPS * D_STATE:]

    dtv = _softplus(dt_ref[...] + dtb_ref[...])
    a = dtv * (-jnp.exp(alog_ref[...]))
    r = lax.broadcasted_iota(jnp.int32, (CHUNK, CHUNK), 0)
    cc = lax.broadcasted_iota(jnp.int32, (CHUNK, CHUNK), 1)
    causal = r >= cc
    tril = jnp.where(causal, 1.0, 0.0).astype(BF16)
    cs = _exact_dot_left(tril, a)
    cs_t = cs.T
    dte = jnp.exp(cs[CHUNK - 1:CHUNK, :] - cs)
    ecs = jnp.exp(cs)
    e01 = e_ref[...]
    dt_x = _exact_dot(dtv, e01)
    dte_x = _exact_dot(dte, e01)
    ecs_x = _exact_dot(ecs, e01)
    xdt = xs * dt_x
    xdec = (xdt * dte_x).astype(BF16)
    xdt_b = xdt.astype(BF16)
    lo = lax.broadcasted_iota(jnp.int32, (CHUNK, LANES), 1) < SSD_HEAD_DIM
    dsk = dskip_ref[...]
    for g in range(SSD_GROUPS):
        gs = slice(g * GROUP_W, (g + 1) * GROUP_W)
        bg = bm[:, g * D_STATE:(g + 1) * D_STATE]
        cg = cm[:, g * D_STATE:(g + 1) * D_STATE].astype(BF16)
        cb = _dot_nt(cg, bg.astype(BF16))
        bgt = bg.T.astype(BF16)
        prev = state_scr[g]
        y_off = _dot(cg, prev.astype(BF16)) * ecs_x[:, gs]
        state_scr[g] = ecs_x[CHUNK - 1:CHUNK, gs] * prev + _dot(bgt, xdec[:, gs])
        for m in range(GROUP_W // LANES):
            ps = slice(g * GROUP_W + m * LANES, g * GROUP_W + (m + 1) * LANES)
            xp = xdt_b[:, ps]
            ydiag = jnp.zeros((CHUNK, LANES), F32)
            for par in range(2):
                j = g * (SSD_HEADS // SSD_GROUPS) + 2 * m + par
                diff = cs[:, j:j + 1] - cs_t[j:j + 1, :]
                ldec = jnp.exp(jnp.where(causal, diff, -jnp.inf))
                mm = (cb * ldec).astype(BF16)
                xpm = jnp.where(lo, xp, 0.0) if par == 0 else jnp.where(lo, 0.0, xp)
                ydiag = ydiag + _dot(mm, xpm.astype(BF16))
            y_ref[:, ps] = ydiag + y_off[:, m * LANES:(m + 1) * LANES] + dsk[:, ps] * xs[:, ps]

    @pl.when(c == last)
    def _():
        for g in range(SSD_GROUPS):
            st_ref[0, g * GROUP_W:(g + 1) * GROUP_W, :] = state_scr[g].T


def ssd_prompt(xbc, dt_raw, cw, cb, dtb, alog, dskip_x, e01, batch, seq):
    nc = seq // CHUNK
    cur = lambda b, c: (b * nc + c, 0)
    const = lambda b, c: (0, 0)
    return pl.pallas_call(
        _ssd_prompt_kernel,
        grid=(batch, nc),
        in_specs=[pl.BlockSpec((CHUNK, CONV_DIM), cur),
                  pl.BlockSpec((CHUNK, HEAD_PAD), cur),
                  pl.BlockSpec((CONV_WIDTH, CONV_DIM), const),
                  pl.BlockSpec((1, CONV_DIM), const),
                  pl.BlockSpec((1, HEAD_PAD), const),
                  pl.BlockSpec((1, HEAD_PAD), const),
                  pl.BlockSpec((1, SSD_WIDTH), const),
                  pl.BlockSpec((HEAD_PAD, SSD_WIDTH), const)],
        out_specs=[pl.BlockSpec((CHUNK, SSD_WIDTH), cur),
                   pl.BlockSpec((1, SSD_WIDTH, D_STATE), lambda b, c: (b, 0, 0))],
        out_shape=[jax.ShapeDtypeStruct((batch * seq, SSD_WIDTH), F32),
                   jax.ShapeDtypeStruct((batch, SSD_WIDTH, D_STATE), F32)],
        scratch_shapes=[pltpu.VMEM((SSD_GROUPS, D_STATE, GROUP_W), F32),
                        pltpu.VMEM((8, CONV_DIM), F32)],
        compiler_params=_params(("parallel", "arbitrary")),
        name="ssd_prompt",
    )(xbc, dt_raw, cw, cb, dtb, alog, dskip_x, e01)


def _swa_sample_kernel(qx_ref, ck_ref, cv_ref, kn_ref, vn_ref, bias_ref, sink_ref,
                       nk_ref, nv_ref, o_ref, *, bt):
    scale = ATT_HEAD_DIM ** -0.5
    kvw = ATT_KV_HEADS * ATT_HEAD_DIM
    rowi = lax.broadcasted_iota(jnp.int32, (WINDOW, kvw), 0)
    bias_c = bias_ref[:, 0:WINDOW]
    bias_n = bias_ref[:, WINDOW:WINDOW + 1]
    sink = sink_ref[:, 0:1]
    for bb in range(bt):
        kc = ck_ref[bb]
        vc = cv_ref[bb]
        kn = kn_ref[bb]
        vn = vn_ref[bb]
        qx = qx_ref[bb]
        s_c = _dot_nt(qx.astype(BF16), kc.astype(BF16)) * scale + bias_c
        s_n = jnp.sum(qx * kn, axis=1, keepdims=True) * scale + bias_n
        m = jnp.maximum(jnp.maximum(jnp.max(s_c, axis=1, keepdims=True), s_n), sink)
        p_c = jnp.exp(s_c - m)
        p_n = jnp.exp(s_n - m)
        den = jnp.sum(p_c, axis=1, keepdims=True) + p_n + jnp.exp(sink - m)
        o_ref[bb] = _dot((p_c / den).astype(BF16), vc.astype(BF16)) + (p_n / den) * vn
        nk_ref[bb] = jnp.where(rowi == WINDOW - 1, kn, pltpu.roll(kc, WINDOW - 1, axis=0))
        nv_ref[bb] = jnp.where(rowi == WINDOW - 1, vn, pltpu.roll(vc, WINDOW - 1, axis=0))


def swa_sample(qx, ck, cv, kn, vn, bias_s, sink_b, bt=8):
    nb = qx.shape[0]
    kvw = ATT_KV_HEADS * ATT_HEAD_DIM
    blk3 = lambda i: (i, 0, 0)
    return pl.pallas_call(
        functools.partial(_swa_sample_kernel, bt=bt),
        grid=(nb // bt,),
        in_specs=[pl.BlockSpec((bt, ATT_HEADS, kvw), blk3),
                  pl.BlockSpec((bt, WINDOW, kvw), blk3),
                  pl.BlockSpec((bt, WINDOW, kvw), blk3),
                  pl.BlockSpec((bt, 1, kvw), blk3),
                  pl.BlockSpec((bt, 1, kvw), blk3),
                  pl.BlockSpec((ATT_HEADS, 2 * WINDOW), lambda i: (0, 0)),
                  pl.BlockSpec((ATT_HEADS, LANES), lambda i: (0, 0))],
        out_specs=[pl.BlockSpec((bt, WINDOW, kvw), blk3),
                   pl.BlockSpec((bt, WINDOW, kvw), blk3),
                   pl.BlockSpec((bt, ATT_HEADS, kvw), blk3)],
        out_shape=[jax.ShapeDtypeStruct((nb, WINDOW, kvw), F32),
                   jax.ShapeDtypeStruct((nb, WINDOW, kvw), F32),
                   jax.ShapeDtypeStruct((nb, ATT_HEADS, kvw), F32)],
        compiler_params=_params(("parallel",)),
        name="swa_sample",
    )(qx, ck, cv, kn, vn, bias_s, sink_b)


def _ssd_sample_prep_kernel(xbc_ref, c0_ref, c1_ref, c2_ref, dt_ref, cw_ref, cb_ref, dtb_ref,
                            alog_ref, e_ref, dec_t_ref, xdt_t_ref, b_ref, c_ref, xs_ref):
    u = (c0_ref[...] * cw_ref[0:1, :] + c1_ref[...] * cw_ref[1:2, :]
         + c2_ref[...] * cw_ref[2:3, :] + xbc_ref[...] * cw_ref[3:4, :]) + cb_ref[...]
    u = u * _sigmoid(u)
    xs = u[:, :SSD_WIDTH]
    b_ref[...] = u[:, SSD_WIDTH:SSD_WIDTH + SSD_GROUPS * D_STATE]
    c_ref[...] = u[:, SSD_WIDTH + SSD_GROUPS * D_STATE:]
    xs_ref[...] = xs
    dtv = _softplus(dt_ref[...] + dtb_ref[...])
    dec = jnp.exp(dtv * (-jnp.exp(alog_ref[...])))
    e01 = e_ref[...]
    dec_t_ref[...] = _exact_dot(dec, e01).T
    xdt_t_ref[...] = (xs * _exact_dot(dtv, e01)).T


def ssd_sample_prep(xbc, c0, c1, c2, dt_raw, cw, cb, dtb, alog, e01):
    nb = xbc.shape[0]
    args = (xbc, c0, c1, c2, dt_raw, cw, cb, dtb, alog, e01)
    full = lambda a: pl.BlockSpec(a.shape, lambda i: (0,) * a.ndim)
    out_shape = [jax.ShapeDtypeStruct((SSD_WIDTH, nb), F32),
                 jax.ShapeDtypeStruct((SSD_WIDTH, nb), F32),
                 jax.ShapeDtypeStruct((nb, SSD_GROUPS * D_STATE), F32),
                 jax.ShapeDtypeStruct((nb, SSD_GROUPS * D_STATE), F32),
                 jax.ShapeDtypeStruct((nb, SSD_WIDTH), F32)]
    return pl.pallas_call(
        _ssd_sample_prep_kernel,
        grid=(1,),
        in_specs=[full(a) for a in args],
        out_specs=[full(s) for s in out_shape],
        out_shape=out_shape,
        compiler_params=_params(("arbitrary",)),
        name="ssd_sample_prep",
    )(*args)


def _ssd_sample_step_kernel(st_ref, dec_t_ref, xdt_t_ref, b_ref, c_ref, xs_ref, dskip_ref,
                            ns_ref, y_ref, yt_scr, *, bt):
    i = pl.program_id(0)
    nb = dec_t_ref.shape[1]

    @pl.when(i == 0)
    def _():
        yt_scr[...] = jnp.zeros(yt_scr.shape, F32)

    dparts = _split3(dec_t_ref[...])
    xparts = _split3(xdt_t_ref[...])
    lane = lax.broadcasted_iota(jnp.int32, (GROUP_W, nb), 1)
    for bb in range(bt):
        b = i * bt + bb
        pick = jnp.where(lax.broadcasted_iota(jnp.int32, (nb, D_STATE), 0) == b,
                         1.0, 0.0).astype(BF16)
        dcol = sum(_dot(p, pick) for p in dparts)
        xcol = sum(_dot(p, pick) for p in xparts)
        brow = b_ref[pl.ds(b, 1), :]
        crow = c_ref[pl.ds(b, 1), :]
        for g in range(SSD_GROUPS):
            gs = slice(g * GROUP_W, (g + 1) * GROUP_W)
            ns = slice(g * D_STATE, (g + 1) * D_STATE)
            hn = dcol[gs] * st_ref[bb, gs, :] + xcol[gs] * brow[:, ns]
            ns_ref[bb, gs, :] = hn
            ycol = jnp.sum(hn * crow[:, ns], axis=1, keepdims=True)
            yt_scr[gs, :] = jnp.where(lane == b, ycol, yt_scr[gs, :])

    @pl.when(i == pl.num_programs(0) - 1)
    def _():
        y_ref[...] = yt_scr[...].T + dskip_ref[...] * xs_ref[...]


def ssd_sample_step(state, dec_t, xdt_t, bmat, cmat, xs, dskip_x, bt=4):
    nb = state.shape[0]
    assert nb == D_STATE
    const2 = lambda i: (0, 0)
    return pl.pallas_call(
        functools.partial(_ssd_sample_step_kernel, bt=bt),
        grid=(nb // bt,),
        in_specs=[pl.BlockSpec((bt, SSD_WIDTH, D_STATE), lambda i: (i, 0, 0)),
                  pl.BlockSpec((SSD_WIDTH, nb), const2),
                  pl.BlockSpec((SSD_WIDTH, nb), const2),
                  pl.BlockSpec((nb, SSD_GROUPS * D_STATE), const2),
                  pl.BlockSpec((nb, SSD_GROUPS * D_STATE), const2),
                  pl.BlockSpec((nb, SSD_WIDTH), const2),
                  pl.BlockSpec((1, SSD_WIDTH), const2)],
        out_specs=[pl.BlockSpec((bt, SSD_WIDTH, D_STATE), lambda i: (i, 0, 0)),
                   pl.BlockSpec((nb, SSD_WIDTH), const2)],
        out_shape=[jax.ShapeDtypeStruct((nb, SSD_WIDTH, D_STATE), F32),
                   jax.ShapeDtypeStruct((nb, SSD_WIDTH), F32)],
        scratch_shapes=[pltpu.VMEM((SSD_WIDTH, nb), F32)],
        compiler_params=_params(("arbitrary",)),
        name="ssd_sample_step",
    )(state, dec_t, xdt_t, bmat, cmat, xs, dskip_x)


def _merge_kernel(x_ref, o_ref, y_ref, z_ref, anw_ref, snw_ref, w1_ref, w2_ref, out_ref):
    ya = _rms(o_ref[...], anw_ref[...]).astype(BF16)
    z = z_ref[...]
    ys = _rms(y_ref[...] * (z * _sigmoid(z)), snw_ref[...]).astype(BF16)
    out_ref[...] = x_ref[...] + _dot(ya, w1_ref[...]) + _dot(ys, w2_ref[...])


def merge(x, o_att, y_ssd, z, anw, snw, w1, w2, tm):
    t, d = x.shape
    row = pl.BlockSpec((tm, d), lambda i: (i, 0))
    vec = pl.BlockSpec((1, d), lambda i: (0, 0))
    mat = pl.BlockSpec((d, d), lambda i: (0, 0))
    return pl.pallas_call(
        _merge_kernel,
        grid=(t // tm,),
        in_specs=[row, row, row, row, vec, vec, mat, mat],
        out_specs=row,
        out_shape=jax.ShapeDtypeStruct((t, d), F32),
        compiler_params=_params(("parallel",)),
        name="merge",
    )(x, o_att, y_ssd, z, anw.reshape(1, d), snw.reshape(1, d), w1, w2)


def _mem_prompt_kernel(x_ref, nw_ref, wq_ref, mk_ref, mv_ref, wo_ref, out_ref):
    x = x_ref[...]
    q = _dot(_rms(x, nw_ref[...]).astype(BF16), wq_ref[...])
    scale = MEM_HEAD_DIM ** -0.5
    outs = []
    for h in range(MEM_HEADS):
        hs = slice(h * MEM_HEAD_DIM, (h + 1) * MEM_HEAD_DIM)
        s = _dot_nt(q[:, hs].astype(BF16), mk_ref[:, hs]) * scale
        p = jnp.exp(s - jnp.max(s, axis=-1, keepdims=True))
        p = p / jnp.sum(p, axis=-1, keepdims=True)
        outs.append(_dot(p.astype(BF16), mv_ref[:, hs]).astype(BF16))
    out_ref[...] = x + _dot(jnp.concatenate(outs, axis=1), wo_ref[...])


def mem_attn_prompt(x, nw, wq, mk, mv, wo, seq, tm):
    t, d = x.shape
    per = seq // tm
    row = pl.BlockSpec((tm, d), lambda i: (i, 0))
    mat = pl.BlockSpec((d, d), lambda i: (0, 0))
    mem = pl.BlockSpec((N_MEM, d), lambda i: (i // per, 0))
    return pl.pallas_call(
        _mem_prompt_kernel,
        grid=(t // tm,),
        in_specs=[row, pl.BlockSpec((1, d), lambda i: (0, 0)), mat, mem, mem, mat],
        out_specs=row,
        out_shape=jax.ShapeDtypeStruct((t, d), F32),
        compiler_params=_params(("parallel",)),
        name="mem_prompt",
    )(x, nw.reshape(1, d), wq, mk, mv, wo)


def _mem_sample_kernel(q_ref, k_ref, v_ref, o_ref, *, bt):
    rows = N_MEM * MEM_HEADS
    qh = lax.broadcasted_iota(jnp.int32, (8, rows), 0)
    kh = lax.broadcasted_iota(jnp.int32, (8, rows), 1) % MEM_HEADS
    live = qh < MEM_HEADS
    for bb in range(bt):
        q4 = q_ref[bb]
        q8 = jnp.concatenate([q4, jnp.zeros_like(q4)], axis=0).astype(BF16)
        k2 = k_ref[bb].reshape(rows, MEM_HEAD_DIM).astype(BF16)
        v2 = v_ref[bb].reshape(rows, MEM_HEAD_DIM).astype(BF16)
        s = _dot_nt(q8, k2) * (MEM_HEAD_DIM ** -0.5)
        s = jnp.where(kh == qh, s, -jnp.inf)
        p = jnp.exp(s - jnp.where(live, jnp.max(s, axis=1, keepdims=True), 0.0))
        p = p / jnp.where(live, jnp.sum(p, axis=1, keepdims=True), 1.0)
        o_ref[bb] = _dot(p.astype(BF16), v2)[0:MEM_HEADS]


def mem_sample(q, ck, cv, bt=2):
    nb, d = q.shape
    blk = pl.BlockSpec((bt, N_MEM, MEM_HEADS, MEM_HEAD_DIM), lambda i: (i, 0, 0, 0))
    vec = pl.BlockSpec((bt, MEM_HEADS, MEM_HEAD_DIM), lambda i: (i, 0, 0))
    return pl.pallas_call(
        functools.partial(_mem_sample_kernel, bt=bt),
        grid=(nb // bt,),
        in_specs=[vec, blk, blk],
        out_specs=vec,
        out_shape=jax.ShapeDtypeStruct((nb, MEM_HEADS, MEM_HEAD_DIM), F32),
        compiler_params=_params(("parallel",)),
        name="mem_sample",
    )(q.reshape(nb, MEM_HEADS, MEM_HEAD_DIM), ck, cv).reshape(nb, d)


def _matmul_res_kernel(a_ref, w_ref, r_ref, out_ref):
    out_ref[...] = r_ref[...] + _dot(a_ref[...].astype(BF16), w_ref[...])


def matmul_res(a, w, res, tm):
    t, d = res.shape
    kdim = a.shape[1]
    return pl.pallas_call(
        _matmul_res_kernel,
        grid=(t // tm,),
        in_specs=[pl.BlockSpec((tm, kdim), lambda i: (i, 0)),
                  pl.BlockSpec((kdim, d), lambda i: (0, 0)),
                  pl.BlockSpec((tm, d), lambda i: (i, 0))],
        out_specs=pl.BlockSpec((tm, d), lambda i: (i, 0)),
        out_shape=jax.ShapeDtypeStruct((t, d), F32),
        compiler_params=_params(("parallel",)),
        name="matmul_res",
    )(a, w, res)


def _topk_rows(work, k):
    rows = work.shape[0]
    iota = lax.broadcasted_iota(jnp.int32, work.shape, 0)
    rank = jnp.full(work.shape, k, jnp.int32)
    vals = []
    for r in range(k):
        m = jnp.max(work, axis=0, keepdims=True)
        idx = jnp.min(jnp.where(work == m, iota, rows), axis=0, keepdims=True)
        hit = iota == idx
        rank = jnp.where(hit, r, rank)
        work = jnp.where(hit, -jnp.inf, work)
        vals.append(m)
    return jnp.concatenate(vals, axis=0), rank


def _topk_rows_distinct(work, k):
    rank = jnp.full(work.shape, k, jnp.int32)
    vals = []
    for r in range(k):
        m = jnp.max(work, axis=0, keepdims=True)
        hit = work == m
        rank = jnp.where(hit, r, rank)
        work = jnp.where(hit, -jnp.inf, work)
        vals.append(m)
    taken = jnp.sum(jnp.where(rank < k, 1.0, 0.0), axis=0, keepdims=True)
    return jnp.concatenate(vals, axis=0), rank, taken


def _peer_candidates(a, b):
    lo4 = lax.broadcasted_iota(jnp.int32, (8, a.shape[1]), 0) < 4
    b8 = b[0:8]
    b44 = jnp.where(lo4, b8, pltpu.roll(b8, 4, axis=0))
    return jnp.concatenate(
        [a[0:1] + b8, a[0:1] + b[8:16], a[1:2] + b8, a[2:3] + b8, a[3:4] + b8,
         jnp.where(lo4, a[4:5], a[5:6]) + b44, jnp.where(lo4, a[6:7], a[7:8]) + b44,
         a[8:16] + b[0:1]], axis=0)


def _peer_emit(subs, a, b, rank0, rank1, cand, chosen, seli_ref, selj_ref):
    k = PEER_TOPK
    cmax = a[0:1, :] + b[0:1, :]
    z = jnp.sum(jnp.where(chosen, jnp.exp(cand - cmax), 0.0), axis=0, keepdims=True)
    lo4 = lax.broadcasted_iota(jnp.int32, (8, a.shape[1]), 0) < 4
    ch = [jnp.where(chosen[8 * i:8 * i + 8], 1.0, 0.0) for i in range(8)]
    cnt = lambda v: jnp.sum(v, axis=0, keepdims=True)
    n = [cnt(ch[0]) + cnt(ch[1]), cnt(ch[2]), cnt(ch[3]), cnt(ch[4]),
         cnt(jnp.where(lo4, ch[5], 0.0)), cnt(jnp.where(lo4, 0.0, ch[5])),
         cnt(jnp.where(lo4, ch[6], 0.0)), cnt(jnp.where(lo4, 0.0, ch[6]))]
    n += [ch[7][i:i + 1] for i in range(8)]
    lim0 = jnp.zeros(rank0.shape, F32)
    for r in range(k):
        lim0 = jnp.where(rank0 == r, n[r], lim0)
    def twice(v):
        bits = pltpu.bitcast(v.astype(BF16).astype(F32), jnp.uint32)
        return bits | (bits >> 16)

    seli_ref[0, 0] = twice(0.5 * jnp.exp(subs[0] - a[0:1, :]) / z)
    seli_ref[0, 1] = twice(lim0)
    selj_ref[0, 0] = pltpu.bitcast(jnp.exp(subs[1] - b[0:1, :]).astype(BF16), jnp.uint32)
    selj_ref[0, 1] = pltpu.bitcast(rank1.astype(F32).astype(BF16), jnp.uint32)


def _peer_select_kernel(pq_ref, keys_ref, seli_ref, selj_ref):
    k = PEER_TOPK
    subs = [_dot_nt(keys_ref[p], pq_ref[:, p * PEER_HALF:(p + 1) * PEER_HALF].astype(BF16))
            for p in range(2)]
    a, rank0, took0 = _topk_rows_distinct(subs[0], k)
    b, rank1, took1 = _topk_rows_distinct(subs[1], k)
    cand = _peer_candidates(a, b)
    _, crank, tookc = _topk_rows_distinct(cand, k)
    _peer_emit(subs, a, b, rank0, rank1, cand, crank < k, seli_ref, selj_ref)
    tied = jnp.where((took0 != k) | (took1 != k) | (tookc != k), 1.0, 0.0)

    @pl.when(jnp.max(tied) > 0.0)
    def _():
        a, rank0 = _topk_rows(subs[0], k)
        b, rank1 = _topk_rows(subs[1], k)
        cand = _peer_candidates(a, b)
        _, crank = _topk_rows(cand, k)
        _peer_emit(subs, a, b, rank0, rank1, cand, crank < k, seli_ref, selj_ref)


def peer_select(pq, keys_bf16, tm):
    t = pq.shape[0]
    spec = lambda rows: pl.BlockSpec((1, 2, rows, tm), lambda i, h: (h, 0, 0, i))
    return pl.pallas_call(
        _peer_select_kernel,
        grid=(t // tm, PEER_HEADS),
        in_specs=[pl.BlockSpec((tm, 2 * PEER_HALF), lambda i, h: (i, h)),
                  pl.BlockSpec((2, N_KEYS, PEER_HALF), lambda i, h: (h, 0, 0))],
        out_specs=[spec(N_KEYS), spec(N_KEYS // 2)],
        out_shape=[jax.ShapeDtypeStruct((PEER_HEADS, 2, N_KEYS, t), jnp.uint32),
                   jax.ShapeDtypeStruct((PEER_HEADS, 2, N_KEYS // 2, t), jnp.uint32)],
        compiler_params=_params(("parallel", "parallel")),
        name="peer_select",
    )(pq, keys_bf16)


def _gelu_x2(x):
    return x * (1.0 + lax.erf(x * np.float32(math.sqrt(0.5))))


def _peer_dense_kernel(xnt_ref, x_ref, u_ref, vt_ref, seli_ref, selj_ref, fnw_ref, y_ref,
                       acc_scr, act_scr, w_scr, *, tm, ib, sub):
    e = pl.program_id(1)
    ne = pl.num_programs(1)

    @pl.when(e == 0)
    def _():
        acc_scr[...] = jnp.zeros(acc_scr.shape, F32)

    i8 = pl.ds(pl.multiple_of(e * ib, 8), ib)
    nsub = ib // sub
    d = acc_scr.shape[0]
    halves = 2
    sub_rows = lambda k: slice(k * sub * N_KEYS, (k + 1) * sub * N_KEYS)

    def activations(k, half):
        cs = slice(half * tm // halves, (half + 1) * tm // halves)
        u = pltpu.bitcast(u_ref[k * sub * N_KEYS // 2:(k + 1) * sub * N_KEYS // 2, :], BF16)
        act_scr[k % 2, :, cs] = _dot(u, pltpu.bitcast(xnt_ref[:, cs], BF16))

    def accumulate(k, half):
        rs = slice(half * d // halves, (half + 1) * d // halves)
        ws = slice(half * d // (2 * halves), (half + 1) * d // (2 * halves))
        acc_scr[rs, :] += _dot(pltpu.bitcast(vt_ref[0, ws, sub_rows(k)], BF16), w_scr[k % 3])

    def gates(k, lt):
        ls = slice(lt * LANES, (lt + 1) * LANES)
        for j0 in range(0, sub, 2):
            gs = [jnp.zeros((N_KEYS, LANES), BF16) for _ in range(2)]
            for h in range(PEER_HEADS):
                e1 = pltpu.bitcast(selj_ref[h, 0, :, ls], BF16)
                rank1 = pltpu.bitcast(selj_ref[h, 1, :, ls], BF16)
                for t in range(2):
                    ii = k * sub + j0 + t
                    row = lambda c: pltpu.bitcast(jnp.broadcast_to(
                        seli_ref[h, c, i8, ls][ii:ii + 1, :], (N_KEYS // 2, LANES)), BF16)
                    gs[t] = gs[t] + jnp.where(rank1 < row(1), e1, jnp.zeros_like(e1)) * row(0)
            for t in range(2):
                rs = slice((j0 + t) * N_KEYS, (j0 + t + 1) * N_KEYS)
                w_scr[k % 3, rs, ls] = gs[t] * _gelu_x2(act_scr[k % 2, rs, ls]).astype(BF16)

    for half in range(halves):
        activations(0, half)
    for k in range(nsub):
        mxu_jobs = []
        if k + 1 < nsub:
            mxu_jobs += [functools.partial(activations, k + 1, half) for half in range(halves)]
        if k >= 1:
            mxu_jobs += [functools.partial(accumulate, k - 1, half) for half in range(halves)]
        for job in mxu_jobs:
            job()
        for lt in range(tm // LANES):
            gates(k, lt)
    for half in range(halves):
        accumulate(nsub - 1, half)

    @pl.when(e == ne - 1)
    def _():
        y_ref[...] = _rms(x_ref[...] + acc_scr[...].T, fnw_ref[...])


PEER_BLOCK_ROWS = 16
PEER_SUB_ROWS = 4


def _pack_peer_kernel(u_ref, v_ref, up_ref, vtp_ref):
    up_ref[...] = pltpu.bitcast(u_ref[...].astype(BF16), jnp.uint32)
    vtp_ref[0] = pltpu.bitcast(v_ref[...].T.astype(BF16), jnp.uint32)


def pack_peer_weights(u, v, ib=PEER_BLOCK_ROWS):
    n, d = u.shape
    eb = ib * N_KEYS
    return pl.pallas_call(
        _pack_peer_kernel,
        grid=(n // eb,),
        in_specs=[pl.BlockSpec((eb, d), lambda e: (e, 0)), pl.BlockSpec((eb, d), lambda e: (e, 0))],
        out_specs=[pl.BlockSpec((eb // 2, d), lambda e: (e, 0)),
                   pl.BlockSpec((1, d // 2, eb), lambda e: (e, 0, 0))],
        out_shape=[jax.ShapeDtypeStruct((n // 2, d), jnp.uint32),
                   jax.ShapeDtypeStruct((n // eb, d // 2, eb), jnp.uint32)],
        compiler_params=_params(("parallel",)),
        name="pack_peer_weights",
    )(u, v)


def peer_dense(xnt, x, u_packed, vt_packed, seli, selj, fnw, tm, ib=PEER_BLOCK_ROWS,
               sub=PEER_SUB_ROWS):
    t, d = x.shape
    eb = ib * N_KEYS
    nblk = N_EXPERTS // eb
    assert vt_packed.shape == (nblk, d // 2, eb) and xnt.shape == (d // 2, t)
    sel_spec = lambda rows: pl.BlockSpec((PEER_HEADS, 2, rows, tm), lambda i, e: (0, 0, 0, i))
    return pl.pallas_call(
        functools.partial(_peer_dense_kernel, tm=tm, ib=ib, sub=sub),
        grid=(t // tm, nblk),
        in_specs=[pl.BlockSpec((d // 2, tm), lambda i, e: (0, i)),
                  pl.BlockSpec((tm, d), lambda i, e: (i, 0)),
                  pl.BlockSpec((eb // 2, d), lambda i, e: (e, 0)),
                  pl.BlockSpec((1, d // 2, eb), lambda i, e: (e, 0, 0)),
                  sel_spec(N_KEYS), sel_spec(N_KEYS // 2),
                  pl.BlockSpec((1, d), lambda i, e: (0, 0))],
        out_specs=pl.BlockSpec((tm, d), lambda i, e: (i, 0)),
        out_shape=jax.ShapeDtypeStruct((t, d), F32),
        scratch_shapes=[pltpu.VMEM((d, tm), F32),
                        pltpu.VMEM((2, sub * N_KEYS, tm), F32),
                        pltpu.VMEM((3, sub * N_KEYS, tm), BF16)],
        compiler_params=_params(("parallel", "arbitrary")),
        name="peer_dense",
    )(xnt, x, u_packed, vt_packed, seli, selj, fnw.reshape(1, d))


def _tail(x, mk, mv, prm, seq, tm, mem_is_shared, tm_peer):
    if mem_is_shared:
        x = mem_attn_prompt(x, prm["norm_mem_w"], prm["w_mq"], mk, mv, prm["w_mo"], seq, tm)
    else:
        (q,) = norm_matmul(x, prm["norm_mem_w"], prm["w_mq"], (D_MODEL,), tm)
        x = matmul_res(mem_sample(q, mk, mv), prm["w_mo"], x, tm)
    xnt, pq = norm_matmul(x, prm["norm_ffn_w"], prm["peer_wq"], (prm["peer_wq"].shape[1],), tm,
                          emit_ht=True)
    seli, selj = peer_select(pq, prm["peer_keys"], tm)
    return peer_dense(xnt, x, prm["peer_u"], prm["peer_vt"], seli, selj, prm["final_norm_w"],
                      tm_peer)


def kernel(x_prompt, x_sample, mem_prompt, cache_swa_k, cache_swa_v, state_ssm, state_conv, cache_mem_k, cache_mem_v, norm_mix_w, w_in, attn_sinks, rel_bias, attn_norm_w, conv_w, conv_b, dt_bias, a_log, d_skip, ssd_norm_w, w_out, norm_mem_w, mem_norm_w, w_mq, w_mk, w_mv, w_mo, norm_ffn_w, peer_wq, peer_keys, peer_u, peer_v, final_norm_w):
    assert w_in.shape[0] == 1, "single-layer step"
    batch, seq, d = x_prompt.shape
    nb = x_sample.shape[0]
    kvw = ATT_KV_HEADS * ATT_HEAD_DIM
    tm = 512

    n_main = D_MODEL + 2 * kvw + SSD_WIDTH + CONV_DIM
    w_in_p = jnp.concatenate(
        [w_in[0], jnp.zeros((d, HEAD_PAD - SSD_HEADS), F32)], axis=1).astype(BF16)
    in_splits = (D_MODEL, kvw, kvw, SSD_WIDTH, CONV_DIM, HEAD_PAD)
    assert n_main + SSD_HEADS == w_in.shape[2]
    pad_h = lambda v: jnp.pad(v.reshape(1, SSD_HEADS), ((0, 0), (0, HEAD_PAD - SSD_HEADS)))
    dtb, alog = pad_h(dt_bias[0]), pad_h(a_log[0])
    dskip_x = jnp.repeat(d_skip[0], SSD_HEAD_DIM).reshape(1, SSD_WIDTH)
    e01 = (jnp.arange(HEAD_PAD)[:, None] == jnp.arange(SSD_WIDTH)[None, :] // SSD_HEAD_DIM
           ).astype(BF16)
    cw, cb = conv_w[0], conv_b[0].reshape(1, CONV_DIM)
    w_out1 = w_out[0, :D_MODEL].astype(BF16)
    w_out2 = w_out[0, D_MODEL:].astype(BF16)
    peer_up, peer_vtp = pack_peer_weights(peer_u[0], peer_v[0])
    prm = dict(norm_mem_w=norm_mem_w[0], w_mq=w_mq[0].astype(BF16), w_mo=w_mo[0].astype(BF16),
               norm_ffn_w=norm_ffn_w[0], peer_wq=peer_wq[0].astype(BF16),
               peer_keys=peer_keys[0].reshape(2 * PEER_HEADS, N_KEYS, PEER_HALF).astype(BF16),
               peer_u=peer_up, peer_vt=peer_vtp, final_norm_w=final_norm_w)

    qi = jnp.arange(WINDOW)[:, None] + WINDOW
    bias_p = rel_bias_rows(rel_bias, _t5_bucket(qi - jnp.arange(2 * WINDOW)[None, :]))
    dist_s = jnp.broadcast_to(jnp.maximum(WINDOW - jnp.arange(2 * WINDOW), 0)[None, :],
                              (8, 2 * WINDOW))
    bias_s = rel_bias_rows(rel_bias, _t5_bucket(dist_s))[:, 0, :]
    sinks = attn_sinks[0]

    xp = x_prompt.reshape(batch * seq, d)
    q, k, v, z, xbc, dt_raw = norm_matmul(xp, norm_mix_w[0], w_in_p, in_splits, tm)
    o_att = swa_prompt(q, k, v, bias_p, sinks, batch, seq)
    y_ssd, p_state = ssd_prompt(xbc, dt_raw, cw, cb, dtb, alog, dskip_x, e01, batch, seq)
    x1 = merge(xp, o_att, y_ssd, z, attn_norm_w[0], ssd_norm_w[0], w_out1, w_out2, tm)
    mkv_w = jnp.concatenate([w_mk[0], w_mv[0]], axis=1).astype(BF16)
    mk, mv = norm_matmul(mem_prompt.reshape(batch * N_MEM, d), mem_norm_w[0], mkv_w, (d, d), tm)
    y_p = _tail(x1, mk.astype(BF16), mv.astype(BF16), prm, seq, tm, True, tm)

    xs_in = x_sample.reshape(nb, d)
    tms = nb
    qs, ks, vs, zs, xbcs, dts = norm_matmul(xs_in, norm_mix_w[0], w_in_p, in_splits, tms)
    own = (jnp.arange(kvw)[None, :] // ATT_HEAD_DIM) == (jnp.arange(ATT_HEADS)[:, None] // (ATT_HEADS // ATT_KV_HEADS))
    qx = jnp.where(own[None], jnp.tile(qs.reshape(nb, ATT_HEADS, ATT_HEAD_DIM), (1, 1, ATT_KV_HEADS)), 0.0)
    ck = cache_swa_k[0].reshape(nb, WINDOW, kvw)
    cv = cache_swa_v[0].reshape(nb, WINDOW, kvw)
    sink_b = jnp.broadcast_to(sinks[:, None], (ATT_HEADS, LANES))
    nk, nv, ox = swa_sample(qx, ck, cv, ks.reshape(nb, 1, kvw), vs.reshape(nb, 1, kvw), bias_s, sink_b)
    ox5 = ox.reshape(nb, ATT_KV_HEADS, ATT_HEADS // ATT_KV_HEADS, ATT_KV_HEADS, ATT_HEAD_DIM)
    o_att_s = jnp.stack([ox5[:, g, :, g, :] for g in range(ATT_KV_HEADS)], axis=1).reshape(nb, d)
    sc = state_conv[0]
    dec_t, xdt_t, bmat, cmat, xs_s = ssd_sample_prep(xbcs, sc[:, 0], sc[:, 1], sc[:, 2], dts, cw, cb, dtb, alog, e01)
    new_state, y_s = ssd_sample_step(state_ssm[0].reshape(nb, SSD_WIDTH, D_STATE), dec_t, xdt_t, bmat, cmat, xs_s, dskip_x)
    x1s = merge(xs_in, o_att_s, y_s, zs, attn_norm_w[0], ssd_norm_w[0], w_out1, w_out2, tms)
    y_s_out = _tail(x1s, cache_mem_k[0], cache_mem_v[0], prm, seq, tms, False, tms)

    k4 = k.reshape(batch, seq, ATT_KV_HEADS, ATT_HEAD_DIM)
    v4 = v.reshape(batch, seq, ATT_KV_HEADS, ATT_HEAD_DIM)
    xbc3 = xbc.reshape(batch, seq, CONV_DIM)
    return (y_p.reshape(batch, seq, d),
            y_s_out.reshape(nb, 1, d),
            k4[None, :, seq - WINDOW:],
            v4[None, :, seq - WINDOW:],
            p_state.reshape(1, batch, SSD_HEADS, SSD_HEAD_DIM, D_STATE),
            xbc3[None, :, seq - (CONV_WIDTH - 1):],
            mk.reshape(1, batch, N_MEM, MEM_HEADS, MEM_HEAD_DIM),
            mv.reshape(1, batch, N_MEM, MEM_HEADS, MEM_HEAD_DIM),
            nk.reshape(1, nb, WINDOW, ATT_KV_HEADS, ATT_HEAD_DIM),
            nv.reshape(1, nb, WINDOW, ATT_KV_HEADS, ATT_HEAD_DIM),
            new_state.reshape(1, nb, SSD_HEADS, SSD_HEAD_DIM, D_STATE),
            jnp.concatenate([sc[:, 1:], xbcs[:, None, :]], axis=1)[None])
```

```python
import functools
import math

import numpy as np
import jax
import jax.numpy as jnp
from jax import lax
from jax.experimental import pallas as pl
from jax.experimental.pallas import tpu as pltpu

F32 = jnp.float32
BF16 = jnp.bfloat16
EPS = 1e-6

D_MODEL = 1024
ATT_HEADS = 16
ATT_KV_HEADS = 4
ATT_HEAD_DIM = 64
WINDOW = 128
N_BUCKETS = 32
MAX_DISTANCE = 128
SSD_HEADS = 16
SSD_HEAD_DIM = 64
SSD_WIDTH = SSD_HEADS * SSD_HEAD_DIM
SSD_GROUPS = 2
D_STATE = 128
CONV_WIDTH = 4
CONV_DIM = SSD_WIDTH + 2 * SSD_GROUPS * D_STATE
CHUNK = 128
N_MEM = 256
MEM_HEADS = 4
MEM_HEAD_DIM = D_MODEL // MEM_HEADS
PEER_HEADS = 8
N_KEYS = 128
N_EXPERTS = N_KEYS * N_KEYS
PEER_TOPK = 16
PEER_HALF = 128

LANES = 128
HEAD_PAD = 128
GROUP_W = SSD_WIDTH // SSD_GROUPS
VMEM_LIMIT = 56 * 1024 * 1024


def _params(sem):
    return pltpu.CompilerParams(dimension_semantics=sem, vmem_limit_bytes=VMEM_LIMIT)


def _dot(a, b):
    return jnp.dot(a, b, preferred_element_type=F32)


def _dot_nt(a, b):
    return lax.dot_general(a, b, (((1,), (1,)), ((), ())), preferred_element_type=F32)


def _rms(x, w):
    var = jnp.mean(x * x, axis=-1, keepdims=True)
    return x * lax.rsqrt(var + EPS) * w


def _sigmoid(x):
    return 1.0 / (1.0 + jnp.exp(-x))


def _softplus(x):
    return jnp.maximum(x, 0.0) + jnp.log1p(jnp.exp(-jnp.abs(x)))


def _split3(v):
    v1 = v.astype(BF16)
    r1 = v - v1.astype(F32)
    v2 = r1.astype(BF16)
    r2 = r1 - v2.astype(F32)
    return v1, v2, r2.astype(BF16)


def _exact_dot(v, m01):
    v1, v2, v3 = _split3(v)
    return _dot(v1, m01) + _dot(v2, m01) + _dot(v3, m01)


def _exact_dot_left(m01, v):
    v1, v2, v3 = _split3(v)
    return _dot(m01, v1) + _dot(m01, v2) + _dot(m01, v3)


def _norm_matmul_kernel(x_ref, nw_ref, w_ref, *out_refs, splits, emit_ht):
    hf = _rms(x_ref[...], nw_ref[...])
    h = hf.astype(BF16)
    refs = list(out_refs)
    if emit_ht:
        refs.pop(0)[...] = pltpu.bitcast(hf.T.astype(BF16), jnp.uint32)
    off = 0
    for o_ref, n in zip(refs, splits):
        o_ref[...] = _dot(h, w_ref[:, off:off + n])
        off += n


def norm_matmul(x, nw, w_bf16, splits, tm, emit_ht=False):
    t, d = x.shape
    n = w_bf16.shape[1]
    assert sum(splits) == n and t % tm == 0
    out_shape = [jax.ShapeDtypeStruct((t, s), F32) for s in splits]
    out_specs = [pl.BlockSpec((tm, s), lambda i: (i, 0)) for s in splits]
    if emit_ht:
        out_shape.insert(0, jax.ShapeDtypeStruct((d // 2, t), jnp.uint32))
        out_specs.insert(0, pl.BlockSpec((d // 2, tm), lambda i: (0, i)))
    return pl.pallas_call(
        functools.partial(_norm_matmul_kernel, splits=tuple(splits), emit_ht=emit_ht),
        grid=(t // tm,),
        in_specs=[pl.BlockSpec((tm, d), lambda i: (i, 0)),
                  pl.BlockSpec((1, d), lambda i: (0, 0)),
                  pl.BlockSpec((d, n), lambda i: (0, 0))],
        out_specs=out_specs,
        out_shape=out_shape,
        compiler_params=_params(("parallel",)),
        name="norm_matmul",
    )(x, nw.reshape(1, d), w_bf16)


def _bias_kernel(table_ref, bucket_ref, out_ref):
    h = pl.program_id(0)
    bk = bucket_ref[...]
    acc = jnp.zeros(bk.shape, F32)
    for b in range(N_BUCKETS):
        acc = jnp.where(bk == b, table_ref[b, h], acc)
    out_ref[0] = acc


def rel_bias_rows(table, bucket):
    r, c = bucket.shape
    return pl.pallas_call(
        _bias_kernel,
        grid=(ATT_HEADS,),
        in_specs=[pl.BlockSpec(memory_space=pltpu.SMEM),
                  pl.BlockSpec((r, c), lambda h: (0, 0))],
        out_specs=pl.BlockSpec((1, r, c), lambda h: (h, 0, 0)),
        out_shape=jax.ShapeDtypeStruct((ATT_HEADS, r, c), F32),
        compiler_params=_params(("parallel",)),
        name="rel_bias",
    )(table, bucket)


def _t5_bucket(dist):
    n = jnp.maximum(dist, 0)
    max_exact = N_BUCKETS // 2
    nf = jnp.maximum(n, 1).astype(F32)
    large = max_exact + (jnp.log(nf / max_exact) / math.log(MAX_DISTANCE / max_exact)
                         * (N_BUCKETS - max_exact)).astype(jnp.int32)
    large = jnp.minimum(large, N_BUCKETS - 1)
    return jnp.where(n < max_exact, n, large)


def _half_pair(pair, rolled, which, lo):
    if which == 0:
        return jnp.where(lo, pair, 0.0), jnp.where(lo, 0.0, rolled)
    return jnp.where(lo, rolled, 0.0), jnp.where(lo, 0.0, pair)


def _swa_prompt_kernel(sink_ref, q_ref, kc_ref, kp_ref, vc_ref, vp_ref, bias_ref, o_ref):
    i = pl.program_id(1)
    blk = WINDOW
    row = lax.broadcasted_iota(jnp.int32, (blk, 2 * blk), 0)
    col = lax.broadcasted_iota(jnp.int32, (blk, 2 * blk), 1)
    dist = row + blk - col
    ok1 = (dist >= 0) & (dist <= WINDOW) & ((col >= blk) | (i > 0))
    ok = jnp.concatenate([ok1, ok1], axis=0)
    first = lax.broadcasted_iota(jnp.int32, (2 * blk, 1), 0) < blk
    lo = lax.broadcasted_iota(jnp.int32, (2 * blk, LANES), 1) < ATT_HEAD_DIM
    scale = ATT_HEAD_DIM ** -0.5
    for pair in range(ATT_KV_HEADS // 2):
        sl = slice(pair * LANES, (pair + 1) * LANES)
        kpair = jnp.concatenate([kp_ref[:, sl], kc_ref[:, sl]], axis=0)
        vpair = jnp.concatenate([vp_ref[:, sl], vc_ref[:, sl]], axis=0)
        kroll = pltpu.roll(kpair, ATT_HEAD_DIM, axis=1)
        vroll = pltpu.roll(vpair, ATT_HEAD_DIM, axis=1)
        for gi in range(2):
            g = 2 * pair + gi
            k_lo, k_hi = _half_pair(kpair, kroll, gi, lo)
            v_lo, v_hi = _half_pair(vpair, vroll, gi, lo)
            k_halves = (k_lo.astype(BF16), k_hi.astype(BF16))
            v_halves = (v_lo.astype(BF16), v_hi.astype(BF16))
            qsl = slice(2 * g * LANES, (2 * g + 2) * LANES)
            q2 = jnp.concatenate([q_ref[:, 2 * g * LANES:(2 * g + 1) * LANES],
                                  q_ref[:, (2 * g + 1) * LANES:(2 * g + 2) * LANES]],
                                 axis=0).astype(BF16)
            acc = jnp.zeros((2 * blk, LANES), F32)
            for par in range(2):
                h0, h1 = 4 * g + par, 4 * g + 2 + par
                bias = jnp.concatenate([bias_ref[h0], bias_ref[h1]], axis=0)
                s = jnp.where(ok, _dot_nt(q2, k_halves[par]) * scale + bias, -jnp.inf)
                sink = jnp.where(first, sink_ref[h0], sink_ref[h1])
                m = jnp.maximum(jnp.max(s, axis=-1, keepdims=True), sink)
                p = jnp.exp(s - m)
                den = jnp.sum(p, axis=-1, keepdims=True) + jnp.exp(sink - m)
                acc = acc + _dot((p / den).astype(BF16), v_halves[par])
            o_ref[:, qsl] = jnp.concatenate([acc[:blk], acc[blk:]], axis=1)


def swa_prompt(q, k, v, bias, sinks, batch, seq):
    nb = seq // WINDOW
    cur = lambda b, i: (b * nb + i, 0)
    prev = lambda b, i: (b * nb + jnp.maximum(i - 1, 0), 0)
    kvw = ATT_KV_HEADS * ATT_HEAD_DIM
    return pl.pallas_call(
        _swa_prompt_kernel,
        grid=(batch, nb),
        in_specs=[pl.BlockSpec(memory_space=pltpu.SMEM),
                  pl.BlockSpec((WINDOW, D_MODEL), cur),
                  pl.BlockSpec((WINDOW, kvw), cur),
                  pl.BlockSpec((WINDOW, kvw), prev),
                  pl.BlockSpec((WINDOW, kvw), cur),
                  pl.BlockSpec((WINDOW, kvw), prev),
                  pl.BlockSpec((ATT_HEADS, WINDOW, 2 * WINDOW), lambda b, i: (0, 0, 0))],
        out_specs=pl.BlockSpec((WINDOW, D_MODEL), cur),
        out_shape=jax.ShapeDtypeStruct((batch * seq, D_MODEL), F32),
        compiler_params=_params(("parallel", "parallel")),
        name="swa_prompt",
    )(sinks, q, k, k, v, v, bias)


def _ssd_prompt_kernel(xbc_ref, dt_ref, cw_ref, cb_ref, dtb_ref, alog_ref, dskip_ref, e_ref,
                       y_ref, st_ref, state_scr, tail_scr):
    c = pl.program_id(1)
    last = pl.num_programs(1) - 1

    @pl.when(c == 0)
    def _():
        state_scr[...] = jnp.zeros(state_scr.shape, F32)
        tail_scr[...] = jnp.zeros(tail_scr.shape, F32)

    x = xbc_ref[...]
    tail = tail_scr[...]
    row8 = lax.broadcasted_iota(jnp.int32, tail.shape, 0)
    acc = x * cw_ref[CONV_WIDTH - 1:CONV_WIDTH, :]
    for s in range(1, CONV_WIDTH):
        xr = pltpu.roll(x, s, axis=0)
        tr = pltpu.roll(tail, s, axis=0)
        head = jnp.where(row8 < s, tr, xr[0:8])
        shifted = jnp.concatenate([head, xr[8:]], axis=0)
        acc = acc + shifted * cw_ref[CONV_WIDTH - 1 - s:CONV_WIDTH - s, :]
    tail_scr[...] = x[CHUNK - 8:CHUNK]
    u = acc + cb_ref[...]
    u = u * _sigmoid(u)
    xs = u[:, :SSD_WIDTH]
    bm = u[:, SSD_WIDTH:SSD_WIDTH + SSD_GROUPS * D_STATE]
    cm = u[:, SSD_WIDTH + SSD_GROUPS * D_STATE:]

    dtv = _softplus(dt_ref[...] + dtb_ref[...])
    a = dtv * (-jnp.exp(alog_ref[...]))
    r = lax.broadcasted_iota(jnp.int32, (CHUNK, CHUNK), 0)
    cc = lax.broadcasted_iota(jnp.int32, (CHUNK, CHUNK), 1)
    causal = r >= cc
    tril = jnp.where(causal, 1.0, 0.0).astype(BF16)
    cs = _exact_dot_left(tril, a)
    cs_t = cs.T
    dte = jnp.exp(cs[CHUNK - 1:CHUNK, :] - cs)
    ecs = jnp.exp(cs)
    e01 = e_ref[...]
    dt_x = _exact_dot(dtv, e01)
    dte_x = _exact_dot(dte, e01)
    ecs_x = _exact_dot(ecs, e01)
    xdt = xs * dt_x
    xdec = (xdt * dte_x).astype(BF16)
    xdt_b = xdt.astype(BF16)
    lo = lax.broadcasted_iota(jnp.int32, (CHUNK, LANES), 1) < SSD_HEAD_DIM
    dsk = dskip_ref[...]
    for g in range(SSD_GROUPS):
        gs = slice(g * GROUP_W, (g + 1) * GROUP_W)
        bg = bm[:, g * D_STATE:(g + 1) * D_STATE]
        cg = cm[:, g * D_STATE:(g + 1) * D_STATE].astype(BF16)
        cb = _dot_nt(cg, bg.astype(BF16))
        bgt = bg.T.astype(BF16)
        prev = state_scr[g]
        y_off = _dot(cg, prev.astype(BF16)) * ecs_x[:, gs]
        state_scr[g] = ecs_x[CHUNK - 1:CHUNK, gs] * prev + _dot(bgt, xdec[:, gs])
        for m in range(GROUP_W // LANES):
            ps = slice(g * GROUP_W + m * LANES, g * GROUP_W + (m + 1) * LANES)
            xp = xdt_b[:, ps]
            ydiag = jnp.zeros((CHUNK, LANES), F32)
            for par in range(2):
                j = g * (SSD_HEADS // SSD_GROUPS) + 2 * m + par
                diff = cs[:, j:j + 1] - cs_t[j:j + 1, :]
                ldec = jnp.exp(jnp.where(causal, diff, -jnp.inf))
                mm = (cb * ldec).astype(BF16)
                xpm = jnp.where(lo, xp, 0.0) if par == 0 else jnp.where(lo, 0.0, xp)
                ydiag = ydiag + _dot(mm, xpm.astype(BF16))
            y_ref[:, ps] = ydiag + y_off[:, m * LANES:(m + 1) * LANES] + dsk[:, ps] * xs[:, ps]

    @pl.when(c == last)
    def _():
        for g in range(SSD_GROUPS):
            st_ref[0, g * GROUP_W:(g + 1) * GROUP_W, :] = state_scr[g].T


def ssd_prompt(xbc, dt_raw, cw, cb, dtb, alog, dskip_x, e01, batch, seq):
    nc = seq // CHUNK
    cur = lambda b, c: (b * nc + c, 0)
    const = lambda b, c: (0, 0)
    return pl.pallas_call(
        _ssd_prompt_kernel,
        grid=(batch, nc),
        in_specs=[pl.BlockSpec((CHUNK, CONV_DIM), cur),
                  pl.BlockSpec((CHUNK, HEAD_PAD), cur),
                  pl.BlockSpec((CONV_WIDTH, CONV_DIM), const),
                  pl.BlockSpec((1, CONV_DIM), const),
                  pl.BlockSpec((1, HEAD_PAD), const),
                  pl.BlockSpec((1, HEAD_PAD), const),
                  pl.BlockSpec((1, SSD_WIDTH), const),
                  pl.BlockSpec((HEAD_PAD, SSD_WIDTH), const)],
        out_specs=[pl.BlockSpec((CHUNK, SSD_WIDTH), cur),
                   pl.BlockSpec((1, SSD_WIDTH, D_STATE), lambda b, c: (b, 0, 0))],
        out_shape=[jax.ShapeDtypeStruct((batch * seq, SSD_WIDTH), F32),
                   jax.ShapeDtypeStruct((batch, SSD_WIDTH, D_STATE), F32)],
        scratch_shapes=[pltpu.VMEM((SSD_GROUPS, D_STATE, GROUP_W), F32),
                        pltpu.VMEM((8, CONV_DIM), F32)],
        compiler_params=_params(("parallel", "arbitrary")),
        name="ssd_prompt",
    )(xbc, dt_raw, cw, cb, dtb, alog, dskip_x, e01)


def _swa_sample_kernel(q_ref, ck_ref, cv_ref, kn_ref, vn_ref, kn8_ref, vn8_ref, bias_ref,
                       biasn_ref, sink_ref, nk_ref, nv_ref, o_ref, *, bt):
    scale = ATT_HEAD_DIM ** -0.5
    rows = WINDOW * ATT_KV_HEADS
    group = ATT_HEADS // ATT_KV_HEADS
    own = (lax.broadcasted_iota(jnp.int32, (ATT_HEADS, rows), 1) % ATT_KV_HEADS
           == lax.broadcasted_iota(jnp.int32, (ATT_HEADS, rows), 0) // group)
    newest = lax.broadcasted_iota(jnp.int32, (8, ATT_HEAD_DIM), 0) >= 8 - ATT_KV_HEADS
    sink = sink_ref[:, 0:1]
    bias_n = biasn_ref[:, 0:1]
    for bb in range(bt):
        kc = ck_ref[bb].reshape(rows, ATT_HEAD_DIM)
        vc = cv_ref[bb].reshape(rows, ATT_HEAD_DIM)
        q = q_ref[bb]
        kn = kn_ref[bb]
        vn = vn_ref[bb]
        s_c = jnp.where(own, _dot_nt(q.astype(BF16), kc.astype(BF16)) * scale + bias_ref[...],
                        -jnp.inf)
        s_n = jnp.sum(q * kn, axis=1, keepdims=True) * scale + bias_n
        m = jnp.maximum(jnp.maximum(jnp.max(s_c, axis=1, keepdims=True), s_n), sink)
        p_c = jnp.exp(s_c - m)
        p_n = jnp.exp(s_n - m)
        den = jnp.sum(p_c, axis=1, keepdims=True) + p_n + jnp.exp(sink - m)
        o_ref[bb] = _dot((p_c / den).astype(BF16), vc.astype(BF16)) + (p_n / den) * vn
        for cache, new8_ref, out_ref in ((kc, kn8_ref, nk_ref), (vc, vn8_ref, nv_ref)):
            shifted = pltpu.roll(cache, rows - ATT_KV_HEADS, axis=0)
            tail = jnp.where(newest, new8_ref[bb], shifted[rows - 8:])
            out_ref[bb] = jnp.concatenate([shifted[:rows - 8], tail], axis=0).reshape(
                WINDOW, ATT_KV_HEADS, ATT_HEAD_DIM)


def swa_sample(q, ck, cv, kn, vn, kn8, vn8, bias_c, bias_n, sink_b, bt=8):
    nb = q.shape[0]
    rows = WINDOW * ATT_KV_HEADS
    blk3 = pl.BlockSpec((bt, ATT_HEADS, ATT_HEAD_DIM), lambda i: (i, 0, 0))
    blk8 = pl.BlockSpec((bt, 2 * ATT_KV_HEADS, ATT_HEAD_DIM), lambda i: (i, 0, 0))
    blk4 = pl.BlockSpec((bt, WINDOW, ATT_KV_HEADS, ATT_HEAD_DIM), lambda i: (i, 0, 0, 0))
    return pl.pallas_call(
        functools.partial(_swa_sample_kernel, bt=bt),
        grid=(nb // bt,),
        in_specs=[blk3, blk4, blk4, blk3, blk3, blk8, blk8,
                  pl.BlockSpec((ATT_HEADS, rows), lambda i: (0, 0)),
                  pl.BlockSpec((ATT_HEADS, LANES), lambda i: (0, 0)),
                  pl.BlockSpec((ATT_HEADS, LANES), lambda i: (0, 0))],
        out_specs=[blk4, blk4, blk3],
        out_shape=[jax.ShapeDtypeStruct(ck.shape, F32),
                   jax.ShapeDtypeStruct(cv.shape, F32),
                   jax.ShapeDtypeStruct((nb, ATT_HEADS, ATT_HEAD_DIM), F32)],
        compiler_params=_params(("parallel",)),
        name="swa_sample",
    )(q, ck, cv, kn, vn, kn8, vn8, bias_c, bias_n, sink_b)


def _ssd_sample_prep_kernel(xbc_ref, c0_ref, c1_ref, c2_ref, dt_ref, cw_ref, cb_ref, dtb_ref,
                            alog_ref, e_ref, dec_t_ref, xdt_t_ref, b_ref, c_ref, xs_ref):
    u = (c0_ref[...] * cw_ref[0:1, :] + c1_ref[...] * cw_ref[1:2, :]
         + c2_ref[...] * cw_ref[2:3, :] + xbc_ref[...] * cw_ref[3:4, :]) + cb_ref[...]
    u = u * _sigmoid(u)
    xs = u[:, :SSD_WIDTH]
    b_ref[...] = u[:, SSD_WIDTH:SSD_WIDTH + SSD_GROUPS * D_STATE]
    c_ref[...] = u[:, SSD_WIDTH + SSD_GROUPS * D_STATE:]
    xs_ref[...] = xs
    dtv = _softplus(dt_ref[...] + dtb_ref[...])
    dec = jnp.exp(dtv * (-jnp.exp(alog_ref[...])))
    e01 = e_ref[...]
    dec_t_ref[...] = _exact_dot(dec, e01).T
    xdt_t_ref[...] = (xs * _exact_dot(dtv, e01)).T


def ssd_sample_prep(xbc, c0, c1, c2, dt_raw, cw, cb, dtb, alog, e01):
    nb = xbc.shape[0]
    args = (xbc, c0, c1, c2, dt_raw, cw, cb, dtb, alog, e01)
    full = lambda a: pl.BlockSpec(a.shape, lambda i: (0,) * a.ndim)
    out_shape = [jax.ShapeDtypeStruct((SSD_WIDTH, nb), F32),
                 jax.ShapeDtypeStruct((SSD_WIDTH, nb), F32),
                 jax.ShapeDtypeStruct((nb, SSD_GROUPS * D_STATE), F32),
                 jax.ShapeDtypeStruct((nb, SSD_GROUPS * D_STATE), F32),
                 jax.ShapeDtypeStruct((nb, SSD_WIDTH), F32)]
    return pl.pallas_call(
        _ssd_sample_prep_kernel,
        grid=(1,),
        in_specs=[full(a) for a in args],
        out_specs=[full(s) for s in out_shape],
        out_shape=out_shape,
        compiler_params=_params(("arbitrary",)),
        name="ssd_sample_prep",
    )(*args)


def _ssd_sample_step_kernel(st_ref, dec_t_ref, xdt_t_ref, b_ref, c_ref, xs_ref, dskip_ref,
                            ns_ref, y_ref, yt_scr, *, bt):
    i = pl.program_id(0)
    nb = dec_t_ref.shape[1]

    @pl.when(i == 0)
    def _():
        yt_scr[...] = jnp.zeros(yt_scr.shape, F32)

    dparts = _split3(dec_t_ref[...])
    xparts = _split3(xdt_t_ref[...])
    lane = lax.broadcasted_iota(jnp.int32, (GROUP_W, nb), 1)
    for bb in range(bt):
        b = i * bt + bb
        pick = jnp.where(lax.broadcasted_iota(jnp.int32, (nb, D_STATE), 0) == b,
                         1.0, 0.0).astype(BF16)
        dcol = sum(_dot(p, pick) for p in dparts)
        xcol = sum(_dot(p, pick) for p in xparts)
        brow = b_ref[pl.ds(b, 1), :]
        crow = c_ref[pl.ds(b, 1), :]
        for g in range(SSD_GROUPS):
            gs = slice(g * GROUP_W, (g + 1) * GROUP_W)
            ns = slice(g * D_STATE, (g + 1) * D_STATE)
            hn = dcol[gs] * st_ref[bb, gs, :] + xcol[gs] * brow[:, ns]
            ns_ref[bb, gs, :] = hn
            ycol = jnp.sum(hn * crow[:, ns], axis=1, keepdims=True)
            yt_scr[gs, :] = jnp.where(lane == b, ycol, yt_scr[gs, :])

    @pl.when(i == pl.num_programs(0) - 1)
    def _():
        y_ref[...] = yt_scr[...].T + dskip_ref[...] * xs_ref[...]


def ssd_sample_step(state, dec_t, xdt_t, bmat, cmat, xs, dskip_x, bt=4):
    nb = state.shape[0]
    assert nb == D_STATE
    const2 = lambda i: (0, 0)
    return pl.pallas_call(
        functools.partial(_ssd_sample_step_kernel, bt=bt),
        grid=(nb // bt,),
        in_specs=[pl.BlockSpec((bt, SSD_WIDTH, D_STATE), lambda i: (i, 0, 0)),
                  pl.BlockSpec((SSD_WIDTH, nb), const2),
                  pl.BlockSpec((SSD_WIDTH, nb), const2),
                  pl.BlockSpec((nb, SSD_GROUPS * D_STATE), const2),
                  pl.BlockSpec((nb, SSD_GROUPS * D_STATE), const2),
                  pl.BlockSpec((nb, SSD_WIDTH), const2),
                  pl.BlockSpec((1, SSD_WIDTH), const2)],
        out_specs=[pl.BlockSpec((bt, SSD_WIDTH, D_STATE), lambda i: (i, 0, 0)),
                   pl.BlockSpec((nb, SSD_WIDTH), const2)],
        out_shape=[jax.ShapeDtypeStruct((nb, SSD_WIDTH, D_STATE), F32),
                   jax.ShapeDtypeStruct((nb, SSD_WIDTH), F32)],
        scratch_shapes=[pltpu.VMEM((SSD_WIDTH, nb), F32)],
        compiler_params=_params(("arbitrary",)),
        name="ssd_sample_step",
    )(state, dec_t, xdt_t, bmat, cmat, xs, dskip_x)


def _merge_kernel(x_ref, o_ref, y_ref, z_ref, anw_ref, snw_ref, w1_ref, w2_ref, out_ref):
    ya = _rms(o_ref[...], anw_ref[...]).astype(BF16)
    z = z_ref[...]
    ys = _rms(y_ref[...] * (z * _sigmoid(z)), snw_ref[...]).astype(BF16)
    out_ref[...] = x_ref[...] + _dot(ya, w1_ref[...]) + _dot(ys, w2_ref[...])


def merge(x, o_att, y_ssd, z, anw, snw, w1, w2, tm):
    t, d = x.shape
    row = pl.BlockSpec((tm, d), lambda i: (i, 0))
    vec = pl.BlockSpec((1, d), lambda i: (0, 0))
    mat = pl.BlockSpec((d, d), lambda i: (0, 0))
    return pl.pallas_call(
        _merge_kernel,
        grid=(t // tm,),
        in_specs=[row, row, row, row, vec, vec, mat, mat],
        out_specs=row,
        out_shape=jax.ShapeDtypeStruct((t, d), F32),
        compiler_params=_params(("parallel",)),
        name="merge",
    )(x, o_att, y_ssd, z, anw.reshape(1, d), snw.reshape(1, d), w1, w2)


def _mem_prompt_kernel(x_ref, nw_ref, wq_ref, mk_ref, mv_ref, wo_ref, out_ref):
    x = x_ref[...]
    q = _dot(_rms(x, nw_ref[...]).astype(BF16), wq_ref[...])
    scale = MEM_HEAD_DIM ** -0.5
    outs = []
    for h in range(MEM_HEADS):
        hs = slice(h * MEM_HEAD_DIM, (h + 1) * MEM_HEAD_DIM)
        s = _dot_nt(q[:, hs].astype(BF16), mk_ref[:, hs]) * scale
        p = jnp.exp(s - jnp.max(s, axis=-1, keepdims=True))
        p = p / jnp.sum(p, axis=-1, keepdims=True)
        outs.append(_dot(p.astype(BF16), mv_ref[:, hs]).astype(BF16))
    out_ref[...] = x + _dot(jnp.concatenate(outs, axis=1), wo_ref[...])


def mem_attn_prompt(x, nw, wq, mk, mv, wo, seq, tm):
    t, d = x.shape
    per = seq // tm
    row = pl.BlockSpec((tm, d), lambda i: (i, 0))
    mat = pl.BlockSpec((d, d), lambda i: (0, 0))
    mem = pl.BlockSpec((N_MEM, d), lambda i: (i // per, 0))
    return pl.pallas_call(
        _mem_prompt_kernel,
        grid=(t // tm,),
        in_specs=[row, pl.BlockSpec((1, d), lambda i: (0, 0)), mat, mem, mem, mat],
        out_specs=row,
        out_shape=jax.ShapeDtypeStruct((t, d), F32),
        compiler_params=_params(("parallel",)),
        name="mem_prompt",
    )(x, nw.reshape(1, d), wq, mk, mv, wo)


def _mem_sample_kernel(q_ref, k_ref, v_ref, o_ref, *, bt):
    rows = N_MEM * MEM_HEADS
    qh = lax.broadcasted_iota(jnp.int32, (8, rows), 0)
    kh = lax.broadcasted_iota(jnp.int32, (8, rows), 1) % MEM_HEADS
    live = qh < MEM_HEADS
    for bb in range(bt):
        q4 = q_ref[bb]
        q8 = jnp.concatenate([q4, jnp.zeros_like(q4)], axis=0).astype(BF16)
        k2 = k_ref[bb].reshape(rows, MEM_HEAD_DIM).astype(BF16)
        v2 = v_ref[bb].reshape(rows, MEM_HEAD_DIM).astype(BF16)
        s = _dot_nt(q8, k2) * (MEM_HEAD_DIM ** -0.5)
        s = jnp.where(kh == qh, s, -jnp.inf)
        p = jnp.exp(s - jnp.where(live, jnp.max(s, axis=1, keepdims=True), 0.0))
        p = p / jnp.where(live, jnp.sum(p, axis=1, keepdims=True), 1.0)
        o_ref[bb] = _dot(p.astype(BF16), v2)[0:MEM_HEADS]


def mem_sample(q, ck, cv, bt=2):
    nb, d = q.shape
    blk = pl.BlockSpec((bt, N_MEM, MEM_HEADS, MEM_HEAD_DIM), lambda i: (i, 0, 0, 0))
    vec = pl.BlockSpec((bt, MEM_HEADS, MEM_HEAD_DIM), lambda i: (i, 0, 0))
    return pl.pallas_call(
        functools.partial(_mem_sample_kernel, bt=bt),
        grid=(nb // bt,),
        in_specs=[vec, blk, blk],
        out_specs=vec,
        out_shape=jax.ShapeDtypeStruct((nb, MEM_HEADS, MEM_HEAD_DIM), F32),
        compiler_params=_params(("parallel",)),
        name="mem_sample",
    )(q.reshape(nb, MEM_HEADS, MEM_HEAD_DIM), ck, cv).reshape(nb, d)


def _matmul_res_kernel(a_ref, w_ref, r_ref, out_ref):
    out_ref[...] = r_ref[...] + _dot(a_ref[...].astype(BF16), w_ref[...])


def matmul_res(a, w, res, tm):
    t, d = res.shape
    kdim = a.shape[1]
    return pl.pallas_call(
        _matmul_res_kernel,
        grid=(t // tm,),
        in_specs=[pl.BlockSpec((tm, kdim), lambda i: (i, 0)),
                  pl.BlockSpec((kdim, d), lambda i: (0, 0)),
                  pl.BlockSpec((tm, d), lambda i: (i, 0))],
        out_specs=pl.BlockSpec((tm, d), lambda i: (i, 0)),
        out_shape=jax.ShapeDtypeStruct((t, d), F32),
        compiler_params=_params(("parallel",)),
        name="matmul_res",
    )(a, w, res)


def _topk_rows(work, k):
    rows = work.shape[0]
    iota = lax.broadcasted_iota(jnp.int32, work.shape, 0)
    rank = jnp.full(work.shape, k, jnp.int32)
    vals = []
    for r in range(k):
        m = jnp.max(work, axis=0, keepdims=True)
        idx = jnp.min(jnp.where(work == m, iota, rows), axis=0, keepdims=True)
        hit = iota == idx
        rank = jnp.where(hit, r, rank)
        work = jnp.where(hit, -jnp.inf, work)
        vals.append(m)
    return jnp.concatenate(vals, axis=0), rank


def _topk_rows_distinct(work, k, want_rank):
    rank = jnp.full(work.shape, k, jnp.int32) if want_rank else None
    vals = []
    for r in range(k):
        m = jnp.max(work, axis=0, keepdims=True)
        hit = work == m
        if want_rank:
            rank = jnp.where(hit, r, rank)
        work = jnp.where(hit, -jnp.inf, work)
        vals.append(m)
    taken = work == -jnp.inf
    count = jnp.sum(jnp.where(taken, 1.0, 0.0), axis=0, keepdims=True)
    return jnp.concatenate(vals, axis=0), rank, taken, count


def _peer_candidates(a, b):
    lo4 = lax.broadcasted_iota(jnp.int32, (8, a.shape[1]), 0) < 4
    b8 = b[0:8]
    b44 = jnp.where(lo4, b8, pltpu.roll(b8, 4, axis=0))
    return jnp.concatenate(
        [a[0:1] + b8, a[0:1] + b[8:16], a[1:2] + b8, a[2:3] + b8, a[3:4] + b8,
         jnp.where(lo4, a[4:5], a[5:6]) + b44, jnp.where(lo4, a[6:7], a[7:8]) + b44,
         a[8:16] + b[0:1]], axis=0)


def _peer_emit(subs, a, b, rank0, rank1, cand, chosen, seli_ref, selj_ref):
    k = PEER_TOPK
    cmax = a[0:1, :] + b[0:1, :]
    z = jnp.sum(jnp.where(chosen, jnp.exp(cand - cmax), 0.0), axis=0, keepdims=True)
    lo4 = lax.broadcasted_iota(jnp.int32, (8, a.shape[1]), 0) < 4
    ch = [jnp.where(chosen[8 * i:8 * i + 8], 1.0, 0.0) for i in range(8)]
    cnt = lambda v: jnp.sum(v, axis=0, keepdims=True)
    n = [cnt(ch[0]) + cnt(ch[1]), cnt(ch[2]), cnt(ch[3]), cnt(ch[4]),
         cnt(jnp.where(lo4, ch[5], 0.0)), cnt(jnp.where(lo4, 0.0, ch[5])),
         cnt(jnp.where(lo4, ch[6], 0.0)), cnt(jnp.where(lo4, 0.0, ch[6]))]
    n += [ch[7][i:i + 1] for i in range(8)]
    lim0 = jnp.zeros(subs[0].shape, F32)
    if rank0 is None:
        for r in reversed(range(k)):
            lim0 = jnp.where(subs[0] >= a[r:r + 1, :], n[r], lim0)
    else:
        for r in range(k):
            lim0 = jnp.where(rank0 == r, n[r], lim0)

    def twice(v):
        bits = pltpu.bitcast(v.astype(BF16).astype(F32), jnp.uint32)
        return bits | (bits >> 16)

    seli_ref[0, 0] = twice(0.5 * jnp.exp(subs[0] - a[0:1, :]) / z)
    seli_ref[0, 1] = twice(lim0)
    selj_ref[0, 0] = pltpu.bitcast(jnp.exp(subs[1] - b[0:1, :]).astype(BF16), jnp.uint32)
    selj_ref[0, 1] = pltpu.bitcast(rank1.astype(F32).astype(BF16), jnp.uint32)


def _peer_select_kernel(pq_ref, keys_ref, seli_ref, selj_ref):
    k = PEER_TOPK
    subs = [_dot_nt(keys_ref[p], pq_ref[:, p * PEER_HALF:(p + 1) * PEER_HALF].astype(BF16))
            for p in range(2)]
    a, _, _, took0 = _topk_rows_distinct(subs[0], k, want_rank=False)
    b, rank1, _, took1 = _topk_rows_distinct(subs[1], k, want_rank=True)
    cand = _peer_candidates(a, b)
    _, _, chosen, tookc = _topk_rows_distinct(cand, k, want_rank=False)
    _peer_emit(subs, a, b, None, rank1, cand, chosen, seli_ref, selj_ref)
    tied = jnp.where((took0 != k) | (took1 != k) | (tookc != k), 1.0, 0.0)

    @pl.when(jnp.max(tied) > 0.0)
    def _():
        a, rank0 = _topk_rows(subs[0], k)
        b, rank1 = _topk_rows(subs[1], k)
        cand = _peer_candidates(a, b)
        _, crank = _topk_rows(cand, k)
        _peer_emit(subs, a, b, rank0, rank1, cand, crank < k, seli_ref, selj_ref)


def peer_select(pq, keys_bf16, tm):
    t = pq.shape[0]
    spec = lambda rows: pl.BlockSpec((1, 2, rows, tm), lambda i, h: (h, 0, 0, i))
    return pl.pallas_call(
        _peer_select_kernel,
        grid=(t // tm, PEER_HEADS),
        in_specs=[pl.BlockSpec((tm, 2 * PEER_HALF), lambda i, h: (i, h)),
                  pl.BlockSpec((2, N_KEYS, PEER_HALF), lambda i, h: (h, 0, 0))],
        out_specs=[spec(N_KEYS), spec(N_KEYS // 2)],
        out_shape=[jax.ShapeDtypeStruct((PEER_HEADS, 2, N_KEYS, t), jnp.uint32),
                   jax.ShapeDtypeStruct((PEER_HEADS, 2, N_KEYS // 2, t), jnp.uint32)],
        compiler_params=_params(("parallel", "parallel")),
        name="peer_select",
    )(pq, keys_bf16)


def _gelu_x2(x):
    return x * (1.0 + lax.erf(x * np.float32(math.sqrt(0.5))))


def _peer_dense_kernel(xnt_ref, x_ref, u_ref, vt_ref, seli_ref, selj_ref, fnw_ref, y_ref,
                       acc_scr, act_scr, w_scr, *, tm, ib, sub):
    e = pl.program_id(1)
    ne = pl.num_programs(1)

    @pl.when(e == 0)
    def _():
        acc_scr[...] = jnp.zeros(acc_scr.shape, F32)

    i8 = pl.ds(pl.multiple_of(e * ib, 8), ib)
    nsub = ib // sub
    d = acc_scr.shape[0]
    halves = 2
    sub_rows = lambda k: slice(k * sub * N_KEYS, (k + 1) * sub * N_KEYS)

    def activations(k, half):
        cs = slice(half * tm // halves, (half + 1) * tm // halves)
        u = pltpu.bitcast(u_ref[k * sub * N_KEYS // 2:(k + 1) * sub * N_KEYS // 2, :], BF16)
        act_scr[k % 2, :, cs] = _dot(u, pltpu.bitcast(xnt_ref[:, cs], BF16))

    def accumulate(k, half):
        rs = slice(half * d // halves, (half + 1) * d // halves)
        ws = slice(half * d // (2 * halves), (half + 1) * d // (2 * halves))
        acc_scr[rs, :] += _dot(pltpu.bitcast(vt_ref[0, ws, sub_rows(k)], BF16), w_scr[k % 3])

    def gates(k, lt):
        ls = slice(lt * LANES, (lt + 1) * LANES)
        for j0 in range(0, sub, 2):
            gs = [jnp.zeros((N_KEYS, LANES), BF16) for _ in range(2)]
            for h in range(PEER_HEADS):
                e1 = pltpu.bitcast(selj_ref[h, 0, :, ls], BF16)
                rank1 = pltpu.bitcast(selj_ref[h, 1, :, ls], BF16)
                for t in range(2):
                    ii = k * sub + j0 + t
                    row = lambda c: pltpu.bitcast(jnp.broadcast_to(
                        seli_ref[h, c, i8, ls][ii:ii + 1, :], (N_KEYS // 2, LANES)), BF16)
                    gs[t] = gs[t] + jnp.where(rank1 < row(1), e1, jnp.zeros_like(e1)) * row(0)
            for t in range(2):
                rs = slice((j0 + t) * N_KEYS, (j0 + t + 1) * N_KEYS)
                w_scr[k % 3, rs, ls] = gs[t] * _gelu_x2(act_scr[k % 2, rs, ls]).astype(BF16)

    for half in range(halves):
        activations(0, half)
    for k in range(nsub):
        mxu_jobs = []
        if k + 1 < nsub:
            mxu_jobs += [functools.partial(activations, k + 1, half) for half in range(halves)]
        if k >= 1:
            mxu_jobs += [functools.partial(accumulate, k - 1, half) for half in range(halves)]
        for job in mxu_jobs:
            job()
        for lt in range(tm // LANES):
            gates(k, lt)
    for half in range(halves):
        accumulate(nsub - 1, half)

    @pl.when(e == ne - 1)
    def _():
        y_ref[...] = _rms(x_ref[...] + acc_scr[...].T, fnw_ref[...])


PEER_BLOCK_ROWS = 16
PEER_SUB_ROWS = 4


def _pack_peer_kernel(u_ref, v_ref, up_ref, vtp_ref):
    up_ref[...] = pltpu.bitcast(u_ref[...].astype(BF16), jnp.uint32)
    vtp_ref[0] = pltpu.bitcast(v_ref[...].T.astype(BF16), jnp.uint32)


def pack_peer_weights(u, v, ib=PEER_BLOCK_ROWS):
    n, d = u.shape
    eb = ib * N_KEYS
    return pl.pallas_call(
        _pack_peer_kernel,
        grid=(n // eb,),
        in_specs=[pl.BlockSpec((eb, d), lambda e: (e, 0)), pl.BlockSpec((eb, d), lambda e: (e, 0))],
        out_specs=[pl.BlockSpec((eb // 2, d), lambda e: (e, 0)),
                   pl.BlockSpec((1, d // 2, eb), lambda e: (e, 0, 0))],
        out_shape=[jax.ShapeDtypeStruct((n // 2, d), jnp.uint32),
                   jax.ShapeDtypeStruct((n // eb, d // 2, eb), jnp.uint32)],
        compiler_params=_params(("parallel",)),
        name="pack_peer_weights",
    )(u, v)


def peer_dense(xnt, x, u_packed, vt_packed, seli, selj, fnw, tm, ib=PEER_BLOCK_ROWS,
               sub=PEER_SUB_ROWS):
    t, d = x.shape
    eb = ib * N_KEYS
    nblk = N_EXPERTS // eb
    assert vt_packed.shape == (nblk, d // 2, eb) and xnt.shape == (d // 2, t)
    sel_spec = lambda rows: pl.BlockSpec((PEER_HEADS, 2, rows, tm), lambda i, e: (0, 0, 0, i))
    return pl.pallas_call(
        functools.partial(_peer_dense_kernel, tm=tm, ib=ib, sub=sub),
        grid=(t // tm, nblk),
        in_specs=[pl.BlockSpec((d // 2, tm), lambda i, e: (0, i)),
                  pl.BlockSpec((tm, d), lambda i, e: (i, 0)),
                  pl.BlockSpec((eb // 2, d), lambda i, e: (e, 0)),
                  pl.BlockSpec((1, d // 2, eb), lambda i, e: (e, 0, 0)),
                  sel_spec(N_KEYS), sel_spec(N_KEYS // 2),
                  pl.BlockSpec((1, d), lambda i, e: (0, 0))],
        out_specs=pl.BlockSpec((tm, d), lambda i, e: (i, 0)),
        out_shape=jax.ShapeDtypeStruct((t, d), F32),
        scratch_shapes=[pltpu.VMEM((d, tm), F32),
                        pltpu.VMEM((2, sub * N_KEYS, tm), F32),
                        pltpu.VMEM((3, sub * N_KEYS, tm), BF16)],
        compiler_params=_params(("parallel", "arbitrary")),
        name="peer_dense",
    )(xnt, x, u_packed, vt_packed, seli, selj, fnw.reshape(1, d))


def _tail(x, mk, mv, prm, seq, tm, mem_is_shared, tm_peer):
    if mem_is_shared:
        x = mem_attn_prompt(x, prm["norm_mem_w"], prm["w_mq"], mk, mv, prm["w_mo"], seq, tm)
    else:
        (q,) = norm_matmul(x, prm["norm_mem_w"], prm["w_mq"], (D_MODEL,), tm)
        x = matmul_res(mem_sample(q, mk, mv), prm["w_mo"], x, tm)
    xnt, pq = norm_matmul(x, prm["norm_ffn_w"], prm["peer_wq"], (prm["peer_wq"].shape[1],), tm,
                          emit_ht=True)
    seli, selj = peer_select(pq, prm["peer_keys"], tm)
    return peer_dense(xnt, x, prm["peer_u"], prm["peer_vt"], seli, selj, prm["final_norm_w"],
                      tm_peer)


def kernel(x_prompt, x_sample, mem_prompt, cache_swa_k, cache_swa_v, state_ssm, state_conv, cache_mem_k, cache_mem_v, norm_mix_w, w_in, attn_sinks, rel_bias, attn_norm_w, conv_w, conv_b, dt_bias, a_log, d_skip, ssd_norm_w, w_out, norm_mem_w, mem_norm_w, w_mq, w_mk, w_mv, w_mo, norm_ffn_w, peer_wq, peer_keys, peer_u, peer_v, final_norm_w):
    assert w_in.shape[0] == 1, "single-layer step"
    batch, seq, d = x_prompt.shape
    nb = x_sample.shape[0]
    kvw = ATT_KV_HEADS * ATT_HEAD_DIM
    tm = 512

    n_main = D_MODEL + 2 * kvw + SSD_WIDTH + CONV_DIM
    w_in_p = jnp.concatenate(
        [w_in[0], jnp.zeros((d, HEAD_PAD - SSD_HEADS), F32)], axis=1).astype(BF16)
    in_splits = (D_MODEL, kvw, kvw, SSD_WIDTH, CONV_DIM, HEAD_PAD)
    assert n_main + SSD_HEADS == w_in.shape[2]
    pad_h = lambda v: jnp.pad(v.reshape(1, SSD_HEADS), ((0, 0), (0, HEAD_PAD - SSD_HEADS)))
    dtb, alog = pad_h(dt_bias[0]), pad_h(a_log[0])
    dskip_x = jnp.repeat(d_skip[0], SSD_HEAD_DIM).reshape(1, SSD_WIDTH)
    e01 = (jnp.arange(HEAD_PAD)[:, None] == jnp.arange(SSD_WIDTH)[None, :] // SSD_HEAD_DIM
           ).astype(BF16)
    cw, cb = conv_w[0], conv_b[0].reshape(1, CONV_DIM)
    w_out1 = w_out[0, :D_MODEL].astype(BF16)
    w_out2 = w_out[0, D_MODEL:].astype(BF16)
    peer_up, peer_vtp = pack_peer_weights(peer_u[0], peer_v[0])
    prm = dict(norm_mem_w=norm_mem_w[0], w_mq=w_mq[0].astype(BF16), w_mo=w_mo[0].astype(BF16),
               norm_ffn_w=norm_ffn_w[0], peer_wq=peer_wq[0].astype(BF16),
               peer_keys=peer_keys[0].reshape(2 * PEER_HEADS, N_KEYS, PEER_HALF).astype(BF16),
               peer_u=peer_up, peer_vt=peer_vtp, final_norm_w=final_norm_w)

    qi = jnp.arange(WINDOW)[:, None] + WINDOW
    bias_p = rel_bias_rows(rel_bias, _t5_bucket(qi - jnp.arange(2 * WINDOW)[None, :]))
    dist_s = jnp.broadcast_to(jnp.maximum(WINDOW - jnp.arange(2 * WINDOW), 0)[None, :],
                              (8, 2 * WINDOW))
    bias_s = rel_bias_rows(rel_bias, _t5_bucket(dist_s))[:, 0, :]
    sinks = attn_sinks[0]

    xp = x_prompt.reshape(batch * seq, d)
    q, k, v, z, xbc, dt_raw = norm_matmul(xp, norm_mix_w[0], w_in_p, in_splits, tm)
    o_att = swa_prompt(q, k, v, bias_p, sinks, batch, seq)
    y_ssd, p_state = ssd_prompt(xbc, dt_raw, cw, cb, dtb, alog, dskip_x, e01, batch, seq)
    x1 = merge(xp, o_att, y_ssd, z, attn_norm_w[0], ssd_norm_w[0], w_out1, w_out2, tm)
    mkv_w = jnp.concatenate([w_mk[0], w_mv[0]], axis=1).astype(BF16)
    mk, mv = norm_matmul(mem_prompt.reshape(batch * N_MEM, d), mem_norm_w[0], mkv_w, (d, d), tm)
    y_p = _tail(x1, mk.astype(BF16), mv.astype(BF16), prm, seq, tm, True, tm)

    xs_in = x_sample.reshape(nb, d)
    tms = nb
    qs, ks, vs, zs, xbcs, dts = norm_matmul(xs_in, norm_mix_w[0], w_in_p, in_splits, tms)
    group = ATT_HEADS // ATT_KV_HEADS
    ks4 = ks.reshape(nb, ATT_KV_HEADS, ATT_HEAD_DIM)
    vs4 = vs.reshape(nb, ATT_KV_HEADS, ATT_HEAD_DIM)
    sink_b = jnp.broadcast_to(sinks[:, None], (ATT_HEADS, LANES))
    bias_c = jnp.repeat(bias_s[:, :WINDOW], ATT_KV_HEADS, axis=1)
    bias_n = jnp.broadcast_to(bias_s[:, WINDOW:WINDOW + 1], (ATT_HEADS, LANES))
    nk, nv, o_s = swa_sample(qs.reshape(nb, ATT_HEADS, ATT_HEAD_DIM), cache_swa_k[0], cache_swa_v[0],
                             jnp.repeat(ks4, group, axis=1), jnp.repeat(vs4, group, axis=1),
                             jnp.tile(ks4, (1, 2, 1)), jnp.tile(vs4, (1, 2, 1)),
                             bias_c, bias_n, sink_b)
    o_att_s = o_s.reshape(nb, d)
    sc = state_conv[0]
    dec_t, xdt_t, bmat, cmat, xs_s = ssd_sample_prep(xbcs, sc[:, 0], sc[:, 1], sc[:, 2], dts, cw, cb, dtb, alog, e01)
    new_state, y_s = ssd_sample_step(state_ssm[0].reshape(nb, SSD_WIDTH, D_STATE), dec_t, xdt_t, bmat, cmat, xs_s, dskip_x)
    x1s = merge(xs_in, o_att_s, y_s, zs, attn_norm_w[0], ssd_norm_w[0], w_out1, w_out2, tms)
    y_s_out = _tail(x1s, cache_mem_k[0], cache_mem_v[0], prm, seq, tms, False, tms)

    k4 = k.reshape(batch, seq, ATT_KV_HEADS, ATT_HEAD_DIM)
    v4 = v.reshape(batch, seq, ATT_KV_HEADS, ATT_HEAD_DIM)
    xbc3 = xbc.reshape(batch, seq, CONV_DIM)
    return (y_p.reshape(batch, seq, d),
            y_s_out.reshape(nb, 1, d),
            k4[None, :, seq - WINDOW:],
            v4[None, :, seq - WINDOW:],
            p_state.reshape(1, batch, SSD_HEADS, SSD_HEAD_DIM, D_STATE),
            xbc3[None, :, seq - (CONV_WIDTH - 1):],
            mk.reshape(1, batch, N_MEM, MEM_HEADS, MEM_HEAD_DIM),
            mv.reshape(1, batch, N_MEM, MEM_HEADS, MEM_HEAD_DIM),
            nk.reshape(1, nb, WINDOW, ATT_KV_HEADS, ATT_HEAD_DIM),
            nv.reshape(1, nb, WINDOW, ATT_KV_HEADS, ATT_HEAD_DIM),
            new_state.reshape(1, nb, SSD_HEADS, SSD_HEAD_DIM, D_STATE),
            jnp.concatenate([sc[:, 1:], xbcs[:, None, :]], axis=1)[None])
```

```python
import functools
import math

import numpy as np
import jax
import jax.numpy as jnp
from jax import lax
from jax.experimental import pallas as pl
from jax.experimental.pallas import tpu as pltpu

F32 = jnp.float32
BF16 = jnp.bfloat16
EPS = 1e-6

D_MODEL = 1024
ATT_HEADS = 16
ATT_KV_HEADS = 4
ATT_HEAD_DIM = 64
WINDOW = 128
N_BUCKETS = 32
MAX_DISTANCE = 128
SSD_HEADS = 16
SSD_HEAD_DIM = 64
SSD_WIDTH = SSD_HEADS * SSD_HEAD_DIM
SSD_GROUPS = 2
D_STATE = 128
CONV_WIDTH = 4
CONV_DIM = SSD_WIDTH + 2 * SSD_GROUPS * D_STATE
CHUNK = 128
N_MEM = 256
MEM_HEADS = 4
MEM_HEAD_DIM = D_MODEL // MEM_HEADS
PEER_HEADS = 8
N_KEYS = 128
N_EXPERTS = N_KEYS * N_KEYS
PEER_TOPK = 16
PEER_HALF = 128

LANES = 128
HEAD_PAD = 128
GROUP_W = SSD_WIDTH // SSD_GROUPS
VMEM_LIMIT = 56 * 1024 * 1024


def _params(sem):
    return pltpu.CompilerParams(dimension_semantics=sem, vmem_limit_bytes=VMEM_LIMIT)


def _dot(a, b):
    return jnp.dot(a, b, preferred_element_type=F32)


def _dot_nt(a, b):
    return lax.dot_general(a, b, (((1,), (1,)), ((), ())), preferred_element_type=F32)


def _rms(x, w):
    var = jnp.mean(x * x, axis=-1, keepdims=True)
    return x * lax.rsqrt(var + EPS) * w


def _sigmoid(x):
    return 1.0 / (1.0 + jnp.exp(-x))


def _softplus(x):
    return jnp.maximum(x, 0.0) + jnp.log1p(jnp.exp(-jnp.abs(x)))


def _split3(v):
    v1 = v.astype(BF16)
    r1 = v - v1.astype(F32)
    v2 = r1.astype(BF16)
    r2 = r1 - v2.astype(F32)
    return v1, v2, r2.astype(BF16)


def _exact_dot(v, m01):
    v1, v2, v3 = _split3(v)
    return _dot(v1, m01) + _dot(v2, m01) + _dot(v3, m01)


def _exact_dot_left(m01, v):
    v1, v2, v3 = _split3(v)
    return _dot(m01, v1) + _dot(m01, v2) + _dot(m01, v3)


def _norm_matmul_kernel(x_ref, nw_ref, w_ref, *out_refs, splits, emit_ht):
    hf = _rms(x_ref[...], nw_ref[...])
    h = hf.astype(BF16)
    refs = list(out_refs)
    if emit_ht:
        refs.pop(0)[...] = pltpu.bitcast(hf.T.astype(BF16), jnp.uint32)
    off = 0
    for o_ref, n in zip(refs, splits):
        o_ref[...] = _dot(h, w_ref[:, off:off + n])
        off += n


def norm_matmul(x, nw, w_bf16, splits, tm, emit_ht=False):
    t, d = x.shape
    n = w_bf16.shape[1]
    assert sum(splits) == n and t % tm == 0
    out_shape = [jax.ShapeDtypeStruct((t, s), F32) for s in splits]
    out_specs = [pl.BlockSpec((tm, s), lambda i: (i, 0)) for s in splits]
    if emit_ht:
        out_shape.insert(0, jax.ShapeDtypeStruct((d // 2, t), jnp.uint32))
        out_specs.insert(0, pl.BlockSpec((d // 2, tm), lambda i: (0, i)))
    return pl.pallas_call(
        functools.partial(_norm_matmul_kernel, splits=tuple(splits), emit_ht=emit_ht),
        grid=(t // tm,),
        in_specs=[pl.BlockSpec((tm, d), lambda i: (i, 0)),
                  pl.BlockSpec((1, d), lambda i: (0, 0)),
                  pl.BlockSpec((d, n), lambda i: (0, 0))],
        out_specs=out_specs,
        out_shape=out_shape,
        compiler_params=_params(("parallel",)),
        name="norm_matmul",
    )(x, nw.reshape(1, d), w_bf16)


def _bias_kernel(table_ref, bucket_ref, out_ref):
    h = pl.program_id(0)
    bk = bucket_ref[...]
    acc = jnp.zeros(bk.shape, F32)
    for b in range(N_BUCKETS):
        acc = jnp.where(bk == b, table_ref[b, h], acc)
    out_ref[0] = acc


def rel_bias_rows(table, bucket):
    r, c = bucket.shape
    return pl.pallas_call(
        _bias_kernel,
        grid=(ATT_HEADS,),
        in_specs=[pl.BlockSpec(memory_space=pltpu.SMEM),
                  pl.BlockSpec((r, c), lambda h: (0, 0))],
        out_specs=pl.BlockSpec((1, r, c), lambda h: (h, 0, 0)),
        out_shape=jax.ShapeDtypeStruct((ATT_HEADS, r, c), F32),
        compiler_params=_params(("parallel",)),
        name="rel_bias",
    )(table, bucket)


def _t5_bucket(dist):
    n = jnp.maximum(dist, 0)
    max_exact = N_BUCKETS // 2
    nf = jnp.maximum(n, 1).astype(F32)
    large = max_exact + (jnp.log(nf / max_exact) / math.log(MAX_DISTANCE / max_exact)
                         * (N_BUCKETS - max_exact)).astype(jnp.int32)
    large = jnp.minimum(large, N_BUCKETS - 1)
    return jnp.where(n < max_exact, n, large)


def _half_pair(pair, rolled, which, lo):
    if which == 0:
        return jnp.where(lo, pair, 0.0), jnp.where(lo, 0.0, rolled)
    return jnp.where(lo, rolled, 0.0), jnp.where(lo, 0.0, pair)


def _swa_prompt_kernel(sink_ref, q_ref, kc_ref, kp_ref, vc_ref, vp_ref, bias_ref, o_ref):
    i = pl.program_id(1)
    blk = WINDOW
    row = lax.broadcasted_iota(jnp.int32, (blk, 2 * blk), 0)
    col = lax.broadcasted_iota(jnp.int32, (blk, 2 * blk), 1)
    dist = row + blk - col
    ok1 = (dist >= 0) & (dist <= WINDOW) & ((col >= blk) | (i > 0))
    ok = jnp.concatenate([ok1, ok1], axis=0)
    first = lax.broadcasted_iota(jnp.int32, (2 * blk, 1), 0) < blk
    lo = lax.broadcasted_iota(jnp.int32, (2 * blk, LANES), 1) < ATT_HEAD_DIM
    scale = ATT_HEAD_DIM ** -0.5
    for pair in range(ATT_KV_HEADS // 2):
        sl = slice(pair * LANES, (pair + 1) * LANES)
        kpair = jnp.concatenate([kp_ref[:, sl], kc_ref[:, sl]], axis=0)
        vpair = jnp.concatenate([vp_ref[:, sl], vc_ref[:, sl]], axis=0)
        kroll = pltpu.roll(kpair, ATT_HEAD_DIM, axis=1)
        vroll = pltpu.roll(vpair, ATT_HEAD_DIM, axis=1)
        for gi in range(2):
            g = 2 * pair + gi
            k_lo, k_hi = _half_pair(kpair, kroll, gi, lo)
            v_lo, v_hi = _half_pair(vpair, vroll, gi, lo)
            k_halves = (k_lo.astype(BF16), k_hi.astype(BF16))
            v_halves = (v_lo.astype(BF16), v_hi.astype(BF16))
            qsl = slice(2 * g * LANES, (2 * g + 2) * LANES)
            q2 = jnp.concatenate([q_ref[:, 2 * g * LANES:(2 * g + 1) * LANES],
                                  q_ref[:, (2 * g + 1) * LANES:(2 * g + 2) * LANES]],
                                 axis=0).astype(BF16)
            acc = jnp.zeros((2 * blk, LANES), F32)
            for par in range(2):
                h0, h1 = 4 * g + par, 4 * g + 2 + par
                bias = jnp.concatenate([bias_ref[h0], bias_ref[h1]], axis=0)
                s = jnp.where(ok, _dot_nt(q2, k_halves[par]) * scale + bias, -jnp.inf)
                sink = jnp.where(first, sink_ref[h0], sink_ref[h1])
                m = jnp.maximum(jnp.max(s, axis=-1, keepdims=True), sink)
                p = jnp.exp(s - m)
                den = jnp.sum(p, axis=-1, keepdims=True) + jnp.exp(sink - m)
                acc = acc + _dot((p / den).astype(BF16), v_halves[par])
            o_ref[:, qsl] = jnp.concatenate([acc[:blk], acc[blk:]], axis=1)


def swa_prompt(q, k, v, bias, sinks, batch, seq):
    nb = seq // WINDOW
    cur = lambda b, i: (b * nb + i, 0)
    prev = lambda b, i: (b * nb + jnp.maximum(i - 1, 0), 0)
    kvw = ATT_KV_HEADS * ATT_HEAD_DIM
    return pl.pallas_call(
        _swa_prompt_kernel,
        grid=(batch, nb),
        in_specs=[pl.BlockSpec(memory_space=pltpu.SMEM),
                  pl.BlockSpec((WINDOW, D_MODEL), cur),
                  pl.BlockSpec((WINDOW, kvw), cur),
                  pl.BlockSpec((WINDOW, kvw), prev),
                  pl.BlockSpec((WINDOW, kvw), cur),
                  pl.BlockSpec((WINDOW, kvw), prev),
                  pl.BlockSpec((ATT_HEADS, WINDOW, 2 * WINDOW), lambda b, i: (0, 0, 0))],
        out_specs=pl.BlockSpec((WINDOW, D_MODEL), cur),
        out_shape=jax.ShapeDtypeStruct((batch * seq, D_MODEL), F32),
        compiler_params=_params(("parallel", "parallel")),
        name="swa_prompt",
    )(sinks, q, k, k, v, v, bias)


def _ssd_prompt_kernel(xbc_ref, dt_ref, cw_ref, cb_ref, dtb_ref, alog_ref, dskip_ref, e_ref,
                       y_ref, st_ref, state_scr, tail_scr):
    c = pl.program_id(1)
    last = pl.num_programs(1) - 1

    @pl.when(c == 0)
    def _():
        state_scr[...] = jnp.zeros(state_scr.shape, F32)
        tail_scr[...] = jnp.zeros(tail_scr.shape, F32)

    x = xbc_ref[...]
    tail = tail_scr[...]
    row8 = lax.broadcasted_iota(jnp.int32, tail.shape, 0)
    acc = x * cw_ref[CONV_WIDTH - 1:CONV_WIDTH, :]
    for s in range(1, CONV_WIDTH):
        xr = pltpu.roll(x, s, axis=0)
        tr = pltpu.roll(tail, s, axis=0)
        head = jnp.where(row8 < s, tr, xr[0:8])
        shifted = jnp.concatenate([head, xr[8:]], axis=0)
        acc = acc + shifted * cw_ref[CONV_WIDTH - 1 - s:CONV_WIDTH - s, :]
    tail_scr[...] = x[CHUNK - 8:CHUNK]
    u = acc + cb_ref[...]
    u = u * _sigmoid(u)
    xs = u[:, :SSD_WIDTH]
    bm = u[:, SSD_WIDTH:SSD_WIDTH + SSD_GROUPS * D_STATE]
    cm = u[:, SSD_WIDTH + SSD_GROUPS * D_STATE:]

    dtv = _softplus(dt_ref[...] + dtb_ref[...])
    a = dtv * (-jnp.exp(alog_ref[...]))
    r = lax.broadcasted_iota(jnp.int32, (CHUNK, CHUNK), 0)
    cc = lax.broadcasted_iota(jnp.int32, (CHUNK, CHUNK), 1)
    causal = r >= cc
    tril = jnp.where(causal, 1.0, 0.0).astype(BF16)
    cs = _exact_dot_left(tril, a)
    cs_t = cs.T
    dte = jnp.exp(cs[CHUNK - 1:CHUNK, :] - cs)
    ecs = jnp.exp(cs)
    e01 = e_ref[...]
    dt_x = _exact_dot(dtv, e01)
    dte_x = _exact_dot(dte, e01)
    ecs_x = _exact_dot(ecs, e01)
    xdt = xs * dt_x
    xdec = (xdt * dte_x).astype(BF16)
    xdt_b = xdt.astype(BF16)
    lo = lax.broadcasted_iota(jnp.int32, (CHUNK, LANES), 1) < SSD_HEAD_DIM
    dsk = dskip_ref[...]
    for g in range(SSD_GROUPS):
        gs = slice(g * GROUP_W, (g + 1) * GROUP_W)
        bg = bm[:, g * D_STATE:(g + 1) * D_STATE]
        cg = cm[:, g * D_STATE:(g + 1) * D_STATE].astype(BF16)
        cb = _dot_nt(cg, bg.astype(BF16))
        bgt = bg.T.astype(BF16)
        prev = state_scr[g]
        y_off = _dot(cg, prev.astype(BF16)) * ecs_x[:, gs]
        state_scr[g] = ecs_x[CHUNK - 1:CHUNK, gs] * prev + _dot(bgt, xdec[:, gs])
        for m in range(GROUP_W // LANES):
            ps = slice(g * GROUP_W + m * LANES, g * GROUP_W + (m + 1) * LANES)
            xp = xdt_b[:, ps]
            ydiag = jnp.zeros((CHUNK, LANES), F32)
            for par in range(2):
                j = g * (SSD_HEADS // SSD_GROUPS) + 2 * m + par
                diff = cs[:, j:j + 1] - cs_t[j:j + 1, :]
                ldec = jnp.exp(jnp.where(causal, diff, -jnp.inf))
                mm = (cb * ldec).astype(BF16)
                xpm = jnp.where(lo, xp, 0.0) if par == 0 else jnp.where(lo, 0.0, xp)
                ydiag = ydiag + _dot(mm, xpm.astype(BF16))
            y_ref[:, ps] = ydiag + y_off[:, m * LANES:(m + 1) * LANES] + dsk[:, ps] * xs[:, ps]

    @pl.when(c == last)
    def _():
        for g in range(SSD_GROUPS):
            st_ref[0, g * GROUP_W:(g + 1) * GROUP_W, :] = state_scr[g].T


def ssd_prompt(xbc, dt_raw, cw, cb, dtb, alog, dskip_x, e01, batch, seq):
    nc = seq // CHUNK
    cur = lambda b, c: (b * nc + c, 0)
    const = lambda b, c: (0, 0)
    return pl.pallas_call(
        _ssd_prompt_kernel,
        grid=(batch, nc),
        in_specs=[pl.BlockSpec((CHUNK, CONV_DIM), cur),
                  pl.BlockSpec((CHUNK, HEAD_PAD), cur),
                  pl.BlockSpec((CONV_WIDTH, CONV_DIM), const),
                  pl.BlockSpec((1, CONV_DIM), const),
                  pl.BlockSpec((1, HEAD_PAD), const),
                  pl.BlockSpec((1, HEAD_PAD), const),
                  pl.BlockSpec((1, SSD_WIDTH), const),
                  pl.BlockSpec((HEAD_PAD, SSD_WIDTH), const)],
        out_specs=[pl.BlockSpec((CHUNK, SSD_WIDTH), cur),
                   pl.BlockSpec((1, SSD_WIDTH, D_STATE), lambda b, c: (b, 0, 0))],
        out_shape=[jax.ShapeDtypeStruct((batch * seq, SSD_WIDTH), F32),
                   jax.ShapeDtypeStruct((batch, SSD_WIDTH, D_STATE), F32)],
        scratch_shapes=[pltpu.VMEM((SSD_GROUPS, D_STATE, GROUP_W), F32),
                        pltpu.VMEM((8, CONV_DIM), F32)],
        compiler_params=_params(("parallel", "arbitrary")),
        name="ssd_prompt",
    )(xbc, dt_raw, cw, cb, dtb, alog, dskip_x, e01)


def _swa_sample_kernel(qx_ref, ck_ref, cv_ref, kn_ref, vn_ref, bias_ref, sink_ref,
                       nk_ref, nv_ref, o_ref, *, bt):
    scale = ATT_HEAD_DIM ** -0.5
    kvw = ATT_KV_HEADS * ATT_HEAD_DIM
    rowi = lax.broadcasted_iota(jnp.int32, (WINDOW, kvw), 0)
    bias_c = bias_ref[:, 0:WINDOW]
    bias_n = bias_ref[:, WINDOW:WINDOW + 1]
    sink = sink_ref[:, 0:1]
    for bb in range(bt):
        kc = ck_ref[bb]
        vc = cv_ref[bb]
        kn = kn_ref[bb]
        vn = vn_ref[bb]
        qx = qx_ref[bb]
        s_c = _dot_nt(qx.astype(BF16), kc.astype(BF16)) * scale + bias_c
        s_n = jnp.sum(qx * kn, axis=1, keepdims=True) * scale + bias_n
        m = jnp.maximum(jnp.maximum(jnp.max(s_c, axis=1, keepdims=True), s_n), sink)
        p_c = jnp.exp(s_c - m)
        p_n = jnp.exp(s_n - m)
        den = jnp.sum(p_c, axis=1, keepdims=True) + p_n + jnp.exp(sink - m)
        o_ref[bb] = _dot((p_c / den).astype(BF16), vc.astype(BF16)) + (p_n / den) * vn
        nk_ref[bb] = jnp.where(rowi == WINDOW - 1, kn, pltpu.roll(kc, WINDOW - 1, axis=0))
        nv_ref[bb] = jnp.where(rowi == WINDOW - 1, vn, pltpu.roll(vc, WINDOW - 1, axis=0))


def swa_sample(qx, ck, cv, kn, vn, bias_s, sink_b, bt=8):
    nb = qx.shape[0]
    kvw = ATT_KV_HEADS * ATT_HEAD_DIM
    blk3 = lambda i: (i, 0, 0)
    return pl.pallas_call(
        functools.partial(_swa_sample_kernel, bt=bt),
        grid=(nb // bt,),
        in_specs=[pl.BlockSpec((bt, ATT_HEADS, kvw), blk3),
                  pl.BlockSpec((bt, WINDOW, kvw), blk3),
                  pl.BlockSpec((bt, WINDOW, kvw), blk3),
                  pl.BlockSpec((bt, 1, kvw), blk3),
                  pl.BlockSpec((bt, 1, kvw), blk3),
                  pl.BlockSpec((ATT_HEADS, 2 * WINDOW), lambda i: (0, 0)),
                  pl.BlockSpec((ATT_HEADS, LANES), lambda i: (0, 0))],
        out_specs=[pl.BlockSpec((bt, WINDOW, kvw), blk3),
                   pl.BlockSpec((bt, WINDOW, kvw), blk3),
                   pl.BlockSpec((bt, ATT_HEADS, kvw), blk3)],
        out_shape=[jax.ShapeDtypeStruct((nb, WINDOW, kvw), F32),
                   jax.ShapeDtypeStruct((nb, WINDOW, kvw), F32),
                   jax.ShapeDtypeStruct((nb, ATT_HEADS, kvw), F32)],
        compiler_params=_params(("parallel",)),
        name="swa_sample",
    )(qx, ck, cv, kn, vn, bias_s, sink_b)


def _ssd_sample_prep_kernel(xbc_ref, c0_ref, c1_ref, c2_ref, dt_ref, cw_ref, cb_ref, dtb_ref,
                            alog_ref, e_ref, dec_t_ref, xdt_t_ref, b_ref, c_ref, xs_ref):
    u = (c0_ref[...] * cw_ref[0:1, :] + c1_ref[...] * cw_ref[1:2, :]
         + c2_ref[...] * cw_ref[2:3, :] + xbc_ref[...] * cw_ref[3:4, :]) + cb_ref[...]
    u = u * _sigmoid(u)
    xs = u[:, :SSD_WIDTH]
    b_ref[...] = u[:, SSD_WIDTH:SSD_WIDTH + SSD_GROUPS * D_STATE]
    c_ref[...] = u[:, SSD_WIDTH + SSD_GROUPS * D_STATE:]
    xs_ref[...] = xs
    dtv = _softplus(dt_ref[...] + dtb_ref[...])
    dec = jnp.exp(dtv * (-jnp.exp(alog_ref[...])))
    e01 = e_ref[...]
    dec_t_ref[...] = _exact_dot(dec, e01).T
    xdt_t_ref[...] = (xs * _exact_dot(dtv, e01)).T


def ssd_sample_prep(xbc, c0, c1, c2, dt_raw, cw, cb, dtb, alog, e01):
    nb = xbc.shape[0]
    args = (xbc, c0, c1, c2, dt_raw, cw, cb, dtb, alog, e01)
    full = lambda a: pl.BlockSpec(a.shape, lambda i: (0,) * a.ndim)
    out_shape = [jax.ShapeDtypeStruct((SSD_WIDTH, nb), F32),
                 jax.ShapeDtypeStruct((SSD_WIDTH, nb), F32),
                 jax.ShapeDtypeStruct((nb, SSD_GROUPS * D_STATE), F32),
                 jax.ShapeDtypeStruct((nb, SSD_GROUPS * D_STATE), F32),
                 jax.ShapeDtypeStruct((nb, SSD_WIDTH), F32)]
    return pl.pallas_call(
        _ssd_sample_prep_kernel,
        grid=(1,),
        in_specs=[full(a) for a in args],
        out_specs=[full(s) for s in out_shape],
        out_shape=out_shape,
        compiler_params=_params(("arbitrary",)),
        name="ssd_sample_prep",
    )(*args)


def _ssd_sample_step_kernel(st_ref, dec_t_ref, xdt_t_ref, b_ref, c_ref, xs_ref, dskip_ref,
                            ns_ref, y_ref, yt_scr, *, bt):
    i = pl.program_id(0)
    nb = dec_t_ref.shape[1]

    @pl.when(i == 0)
    def _():
        yt_scr[...] = jnp.zeros(yt_scr.shape, F32)

    dparts = _split3(dec_t_ref[...])
    xparts = _split3(xdt_t_ref[...])
    lane = lax.broadcasted_iota(jnp.int32, (GROUP_W, nb), 1)
    for bb in range(bt):
        b = i * bt + bb
        pick = jnp.where(lax.broadcasted_iota(jnp.int32, (nb, D_STATE), 0) == b,
                         1.0, 0.0).astype(BF16)
        dcol = sum(_dot(p, pick) for p in dparts)
        xcol = sum(_dot(p, pick) for p in xparts)
        brow = b_ref[pl.ds(b, 1), :]
        crow = c_ref[pl.ds(b, 1), :]
        for g in range(SSD_GROUPS):
            gs = slice(g * GROUP_W, (g + 1) * GROUP_W)
            ns = slice(g * D_STATE, (g + 1) * D_STATE)
            hn = dcol[gs] * st_ref[bb, gs, :] + xcol[gs] * brow[:, ns]
            ns_ref[bb, gs, :] = hn
            ycol = jnp.sum(hn * crow[:, ns], axis=1, keepdims=True)
            yt_scr[gs, :] = jnp.where(lane == b, ycol, yt_scr[gs, :])

    @pl.when(i == pl.num_programs(0) - 1)
    def _():
        y_ref[...] = yt_scr[...].T + dskip_ref[...] * xs_ref[...]


def ssd_sample_step(state, dec_t, xdt_t, bmat, cmat, xs, dskip_x, bt=4):
    nb = state.shape[0]
    assert nb == D_STATE
    const2 = lambda i: (0, 0)
    return pl.pallas_call(
        functools.partial(_ssd_sample_step_kernel, bt=bt),
        grid=(nb // bt,),
        in_specs=[pl.BlockSpec((bt, SSD_WIDTH, D_STATE), lambda i: (i, 0, 0)),
                  pl.BlockSpec((SSD_WIDTH, nb), const2),
                  pl.BlockSpec((SSD_WIDTH, nb), const2),
                  pl.BlockSpec((nb, SSD_GROUPS * D_STATE), const2),
                  pl.BlockSpec((nb, SSD_GROUPS * D_STATE), const2),
                  pl.BlockSpec((nb, SSD_WIDTH), const2),
                  pl.BlockSpec((1, SSD_WIDTH), const2)],
        out_specs=[pl.BlockSpec((bt, SSD_WIDTH, D_STATE), lambda i: (i, 0, 0)),
                   pl.BlockSpec((nb, SSD_WIDTH), const2)],
        out_shape=[jax.ShapeDtypeStruct((nb, SSD_WIDTH, D_STATE), F32),
                   jax.ShapeDtypeStruct((nb, SSD_WIDTH), F32)],
        scratch_shapes=[pltpu.VMEM((SSD_WIDTH, nb), F32)],
        compiler_params=_params(("arbitrary",)),
        name="ssd_sample_step",
    )(state, dec_t, xdt_t, bmat, cmat, xs, dskip_x)


def _merge_kernel(x_ref, o_ref, y_ref, z_ref, anw_ref, snw_ref, w1_ref, w2_ref, out_ref):
    ya = _rms(o_ref[...], anw_ref[...]).astype(BF16)
    z = z_ref[...]
    ys = _rms(y_ref[...] * (z * _sigmoid(z)), snw_ref[...]).astype(BF16)
    out_ref[...] = x_ref[...] + _dot(ya, w1_ref[...]) + _dot(ys, w2_ref[...])


def merge(x, o_att, y_ssd, z, anw, snw, w1, w2, tm):
    t, d = x.shape
    row = pl.BlockSpec((tm, d), lambda i: (i, 0))
    vec = pl.BlockSpec((1, d), lambda i: (0, 0))
    mat = pl.BlockSpec((d, d), lambda i: (0, 0))
    return pl.pallas_call(
        _merge_kernel,
        grid=(t // tm,),
        in_specs=[row, row, row, row, vec, vec, mat, mat],
        out_specs=row,
        out_shape=jax.ShapeDtypeStruct((t, d), F32),
        compiler_params=_params(("parallel",)),
        name="merge",
    )(x, o_att, y_ssd, z, anw.reshape(1, d), snw.reshape(1, d), w1, w2)


def _mem_prompt_kernel(x_ref, nw_ref, wq_ref, mk_ref, mv_ref, wo_ref, out_ref):
    x = x_ref[...]
    q = _dot(_rms(x, nw_ref[...]).astype(BF16), wq_ref[...])
    scale = MEM_HEAD_DIM ** -0.5
    outs = []
    for h in range(MEM_HEADS):
        hs = slice(h * MEM_HEAD_DIM, (h + 1) * MEM_HEAD_DIM)
        s = _dot_nt(q[:, hs].astype(BF16), mk_ref[:, hs]) * scale
        p = jnp.exp(s - jnp.max(s, axis=-1, keepdims=True))
        p = p / jnp.sum(p, axis=-1, keepdims=True)
        outs.append(_dot(p.astype(BF16), mv_ref[:, hs]).astype(BF16))
    out_ref[...] = x + _dot(jnp.concatenate(outs, axis=1), wo_ref[...])


def mem_attn_prompt(x, nw, wq, mk, mv, wo, seq, tm):
    t, d = x.shape
    per = seq // tm
    row = pl.BlockSpec((tm, d), lambda i: (i, 0))
    mat = pl.BlockSpec((d, d), lambda i: (0, 0))
    mem = pl.BlockSpec((N_MEM, d), lambda i: (i // per, 0))
    return pl.pallas_call(
        _mem_prompt_kernel,
        grid=(t // tm,),
        in_specs=[row, pl.BlockSpec((1, d), lambda i: (0, 0)), mat, mem, mem, mat],
        out_specs=row,
        out_shape=jax.ShapeDtypeStruct((t, d), F32),
        compiler_params=_params(("parallel",)),
        name="mem_prompt",
    )(x, nw.reshape(1, d), wq, mk, mv, wo)


def _mem_sample_kernel(q_ref, k_ref, v_ref, o_ref, *, bt):
    rows = N_MEM * MEM_HEADS
    qh = lax.broadcasted_iota(jnp.int32, (8, rows), 0)
    kh = lax.broadcasted_iota(jnp.int32, (8, rows), 1) % MEM_HEADS
    live = qh < MEM_HEADS
    for bb in range(bt):
        q4 = q_ref[bb]
        q8 = jnp.concatenate([q4, jnp.zeros_like(q4)], axis=0).astype(BF16)
        k2 = k_ref[bb].reshape(rows, MEM_HEAD_DIM).astype(BF16)
        v2 = v_ref[bb].reshape(rows, MEM_HEAD_DIM).astype(BF16)
        s = _dot_nt(q8, k2) * (MEM_HEAD_DIM ** -0.5)
        s = jnp.where(kh == qh, s, -jnp.inf)
        p = jnp.exp(s - jnp.where(live, jnp.max(s, axis=1, keepdims=True), 0.0))
        p = p / jnp.where(live, jnp.sum(p, axis=1, keepdims=True), 1.0)
        o_ref[bb] = _dot(p.astype(BF16), v2)[0:MEM_HEADS]


def mem_sample(q, ck, cv, bt=2):
    nb, d = q.shape
    blk = pl.BlockSpec((bt, N_MEM, MEM_HEADS, MEM_HEAD_DIM), lambda i: (i, 0, 0, 0))
    vec = pl.BlockSpec((bt, MEM_HEADS, MEM_HEAD_DIM), lambda i: (i, 0, 0))
    return pl.pallas_call(
        functools.partial(_mem_sample_kernel, bt=bt),
        grid=(nb // bt,),
        in_specs=[vec, blk, blk],
        out_specs=vec,
        out_shape=jax.ShapeDtypeStruct((nb, MEM_HEADS, MEM_HEAD_DIM), F32),
        compiler_params=_params(("parallel",)),
        name="mem_sample",
    )(q.reshape(nb, MEM_HEADS, MEM_HEAD_DIM), ck, cv).reshape(nb, d)


def _matmul_res_kernel(a_ref, w_ref, r_ref, out_ref):
    out_ref[...] = r_ref[...] + _dot(a_ref[...].astype(BF16), w_ref[...])


def matmul_res(a, w, res, tm):
    t, d = res.shape
    kdim = a.shape[1]
    return pl.pallas_call(
        _matmul_res_kernel,
        grid=(t // tm,),
        in_specs=[pl.BlockSpec((tm, kdim), lambda i: (i, 0)),
                  pl.BlockSpec((kdim, d), lambda i: (0, 0)),
                  pl.BlockSpec((tm, d), lambda i: (i, 0))],
        out_specs=pl.BlockSpec((tm, d), lambda i: (i, 0)),
        out_shape=jax.ShapeDtypeStruct((t, d), F32),
        compiler_params=_params(("parallel",)),
        name="matmul_res",
    )(a, w, res)


def _topk_rows(work, k):
    rows = work.shape[0]
    iota = lax.broadcasted_iota(jnp.int32, work.shape, 0)
    rank = jnp.full(work.shape, k, jnp.int32)
    vals = []
    for r in range(k):
        m = jnp.max(work, axis=0, keepdims=True)
        idx = jnp.min(jnp.where(work == m, iota, rows), axis=0, keepdims=True)
        hit = iota == idx
        rank = jnp.where(hit, r, rank)
        work = jnp.where(hit, -jnp.inf, work)
        vals.append(m)
    return jnp.concatenate(vals, axis=0), rank


def _topk_rows_distinct(work, k, want_rank):
    rank = jnp.full(work.shape, k, jnp.int32) if want_rank else None
    vals = []
    for r in range(k):
        m = jnp.max(work, axis=0, keepdims=True)
        hit = work == m
        if want_rank:
            rank = jnp.where(hit, r, rank)
        work = jnp.where(hit, -jnp.inf, work)
        vals.append(m)
    taken = work == -jnp.inf
    count = jnp.sum(jnp.where(taken, 1.0, 0.0), axis=0, keepdims=True)
    return jnp.concatenate(vals, axis=0), rank, taken, count


def _peer_candidates(a, b):
    lo4 = lax.broadcasted_iota(jnp.int32, (8, a.shape[1]), 0) < 4
    b8 = b[0:8]
    b44 = jnp.where(lo4, b8, pltpu.roll(b8, 4, axis=0))
    return jnp.concatenate(
        [a[0:1] + b8, a[0:1] + b[8:16], a[1:2] + b8, a[2:3] + b8, a[3:4] + b8,
         jnp.where(lo4, a[4:5], a[5:6]) + b44, jnp.where(lo4, a[6:7], a[7:8]) + b44,
         a[8:16] + b[0:1]], axis=0)


def _peer_emit(subs, a, b, rank0, rank1, cand, chosen, seli_ref, selj_ref):
    k = PEER_TOPK
    cmax = a[0:1, :] + b[0:1, :]
    z = jnp.sum(jnp.where(chosen, jnp.exp(cand - cmax), 0.0), axis=0, keepdims=True)
    lo4 = lax.broadcasted_iota(jnp.int32, (8, a.shape[1]), 0) < 4
    ch = [jnp.where(chosen[8 * i:8 * i + 8], 1.0, 0.0) for i in range(8)]
    cnt = lambda v: jnp.sum(v, axis=0, keepdims=True)
    n = [cnt(ch[0]) + cnt(ch[1]), cnt(ch[2]), cnt(ch[3]), cnt(ch[4]),
         cnt(jnp.where(lo4, ch[5], 0.0)), cnt(jnp.where(lo4, 0.0, ch[5])),
         cnt(jnp.where(lo4, ch[6], 0.0)), cnt(jnp.where(lo4, 0.0, ch[6]))]
    n += [ch[7][i:i + 1] for i in range(8)]
    lim0 = jnp.zeros(subs[0].shape, F32)
    if rank0 is None:
        for r in reversed(range(k)):
            lim0 = jnp.where(subs[0] >= a[r:r + 1, :], n[r], lim0)
    else:
        for r in range(k):
            lim0 = jnp.where(rank0 == r, n[r], lim0)

    def twice(v):
        bits = pltpu.bitcast(v.astype(BF16).astype(F32), jnp.uint32)
        return bits | (bits >> 16)

    seli_ref[0, 0] = twice(0.5 * jnp.exp(subs[0] - a[0:1, :]) / z)
    seli_ref[0, 1] = twice(lim0)
    selj_ref[0, 0] = pltpu.bitcast(jnp.exp(subs[1] - b[0:1, :]).astype(BF16), jnp.uint32)
    selj_ref[0, 1] = pltpu.bitcast(rank1.astype(F32).astype(BF16), jnp.uint32)


def _peer_select_kernel(pq_ref, keys_ref, seli_ref, selj_ref):
    k = PEER_TOPK
    subs = [_dot_nt(keys_ref[p], pq_ref[:, p * PEER_HALF:(p + 1) * PEER_HALF].astype(BF16))
            for p in range(2)]
    a, _, _, took0 = _topk_rows_distinct(subs[0], k, want_rank=False)
    b, rank1, _, took1 = _topk_rows_distinct(subs[1], k, want_rank=True)
    cand = _peer_candidates(a, b)
    _, _, chosen, tookc = _topk_rows_distinct(cand, k, want_rank=False)
    _peer_emit(subs, a, b, None, rank1, cand, chosen, seli_ref, selj_ref)
    tied = jnp.where((took0 != k) | (took1 != k) | (tookc != k), 1.0, 0.0)

    @pl.when(jnp.max(tied) > 0.0)
    def _():
        a, rank0 = _topk_rows(subs[0], k)
        b, rank1 = _topk_rows(subs[1], k)
        cand = _peer_candidates(a, b)
        _, crank = _topk_rows(cand, k)
        _peer_emit(subs, a, b, rank0, rank1, cand, crank < k, seli_ref, selj_ref)


def peer_select(pq, keys_bf16, tm):
    t = pq.shape[0]
    spec = lambda rows: pl.BlockSpec((1, 2, rows, tm), lambda i, h: (h, 0, 0, i))
    return pl.pallas_call(
        _peer_select_kernel,
        grid=(t // tm, PEER_HEADS),
        in_specs=[pl.BlockSpec((tm, 2 * PEER_HALF), lambda i, h: (i, h)),
                  pl.BlockSpec((2, N_KEYS, PEER_HALF), lambda i, h: (h, 0, 0))],
        out_specs=[spec(N_KEYS), spec(N_KEYS // 2)],
        out_shape=[jax.ShapeDtypeStruct((PEER_HEADS, 2, N_KEYS, t), jnp.uint32),
                   jax.ShapeDtypeStruct((PEER_HEADS, 2, N_KEYS // 2, t), jnp.uint32)],
        compiler_params=_params(("parallel", "parallel")),
        name="peer_select",
    )(pq, keys_bf16)


def _gelu_x2(x):
    return x * (1.0 + lax.erf(x * np.float32(math.sqrt(0.5))))


def _peer_dense_kernel(xnt_ref, x_ref, u_ref, vt_ref, seli_ref, selj_ref, fnw_ref, y_ref,
                       acc_scr, act_scr, w_scr, *, tm, ib, sub):
    e = pl.program_id(1)
    ne = pl.num_programs(1)

    @pl.when(e == 0)
    def _():
        acc_scr[...] = jnp.zeros(acc_scr.shape, F32)

    i8 = pl.ds(pl.multiple_of(e * ib, 8), ib)
    nsub = ib // sub
    d = acc_scr.shape[0]
    halves = 2
    sub_rows = lambda k: slice(k * sub * N_KEYS, (k + 1) * sub * N_KEYS)

    def activations(k, half):
        cs = slice(half * tm // halves, (half + 1) * tm // halves)
        u = pltpu.bitcast(u_ref[k * sub * N_KEYS // 2:(k + 1) * sub * N_KEYS // 2, :], BF16)
        act_scr[k % act_scr.shape[0], :, cs] = _dot(u, pltpu.bitcast(xnt_ref[:, cs], BF16))

    def accumulate(k, half):
        rs = slice(half * d // halves, (half + 1) * d // halves)
        ws = slice(half * d // (2 * halves), (half + 1) * d // (2 * halves))
        acc_scr[rs, :] += _dot(pltpu.bitcast(vt_ref[0, ws, sub_rows(k)], BF16),
                               w_scr[k % w_scr.shape[0]])

    def gates(k, lt):
        ls = slice(lt * LANES, (lt + 1) * LANES)
        for j0 in range(0, sub, 2):
            gs = [jnp.zeros((N_KEYS, LANES), BF16) for _ in range(2)]
            for h in range(PEER_HEADS):
                e1 = pltpu.bitcast(selj_ref[h, 0, :, ls], BF16)
                rank1 = pltpu.bitcast(selj_ref[h, 1, :, ls], BF16)
                for t in range(2):
                    ii = k * sub + j0 + t
                    row = lambda c: pltpu.bitcast(jnp.broadcast_to(
                        seli_ref[h, c, i8, ls][ii:ii + 1, :], (N_KEYS // 2, LANES)), BF16)
                    gs[t] = gs[t] + jnp.where(rank1 < row(1), e1, jnp.zeros_like(e1)) * row(0)
            for t in range(2):
                rs = slice((j0 + t) * N_KEYS, (j0 + t + 1) * N_KEYS)
                act = act_scr[k % act_scr.shape[0], rs, ls]
                w_scr[k % w_scr.shape[0], rs, ls] = gs[t] * _gelu_x2(act).astype(BF16)

    for half in range(halves):
        activations(0, half)
    for k in range(nsub):
        mxu_jobs = []
        if k + 1 < nsub:
            mxu_jobs += [functools.partial(activations, k + 1, half) for half in range(halves)]
        if k >= 1:
            mxu_jobs += [functools.partial(accumulate, k - 1, half) for half in range(halves)]
        for job in mxu_jobs:
            job()
        for lt in range(tm // LANES):
            gates(k, lt)
    for half in range(halves):
        accumulate(nsub - 1, half)

    @pl.when(e == ne - 1)
    def _():
        y_ref[...] = _rms(x_ref[...] + acc_scr[...].T, fnw_ref[...])


PEER_BLOCK_ROWS = 16
PEER_SUB_ROWS = 16


def _pack_peer_kernel(u_ref, v_ref, up_ref, vtp_ref):
    up_ref[...] = pltpu.bitcast(u_ref[...].astype(BF16), jnp.uint32)
    vtp_ref[0] = pltpu.bitcast(v_ref[...].T.astype(BF16), jnp.uint32)


def pack_peer_weights(u, v, ib=PEER_BLOCK_ROWS):
    n, d = u.shape
    eb = ib * N_KEYS
    return pl.pallas_call(
        _pack_peer_kernel,
        grid=(n // eb,),
        in_specs=[pl.BlockSpec((eb, d), lambda e: (e, 0)), pl.BlockSpec((eb, d), lambda e: (e, 0))],
        out_specs=[pl.BlockSpec((eb // 2, d), lambda e: (e, 0)),
                   pl.BlockSpec((1, d // 2, eb), lambda e: (e, 0, 0))],
        out_shape=[jax.ShapeDtypeStruct((n // 2, d), jnp.uint32),
                   jax.ShapeDtypeStruct((n // eb, d // 2, eb), jnp.uint32)],
        compiler_params=_params(("parallel",)),
        name="pack_peer_weights",
    )(u, v)


def peer_dense(xnt, x, u_packed, vt_packed, seli, selj, fnw, tm, ib=PEER_BLOCK_ROWS,
               sub=PEER_SUB_ROWS):
    t, d = x.shape
    eb = ib * N_KEYS
    nblk = N_EXPERTS // eb
    assert vt_packed.shape == (nblk, d // 2, eb) and xnt.shape == (d // 2, t)
    sel_spec = lambda rows: pl.BlockSpec((PEER_HEADS, 2, rows, tm), lambda i, e: (0, 0, 0, i))
    return pl.pallas_call(
        functools.partial(_peer_dense_kernel, tm=tm, ib=ib, sub=sub),
        grid=(t // tm, nblk),
        in_specs=[pl.BlockSpec((d // 2, tm), lambda i, e: (0, i)),
                  pl.BlockSpec((tm, d), lambda i, e: (i, 0)),
                  pl.BlockSpec((eb // 2, d), lambda i, e: (e, 0)),
                  pl.BlockSpec((1, d // 2, eb), lambda i, e: (e, 0, 0)),
                  sel_spec(N_KEYS), sel_spec(N_KEYS // 2),
                  pl.BlockSpec((1, d), lambda i, e: (0, 0))],
        out_specs=pl.BlockSpec((tm, d), lambda i, e: (i, 0)),
        out_shape=jax.ShapeDtypeStruct((t, d), F32),
        scratch_shapes=[pltpu.VMEM((d, tm), F32),
                        pltpu.VMEM((min(2, ib // sub), sub * N_KEYS, tm), F32),
                        pltpu.VMEM((min(3, ib // sub), sub * N_KEYS, tm), BF16)],
        compiler_params=_params(("parallel", "arbitrary")),
        name="peer_dense",
    )(xnt, x, u_packed, vt_packed, seli, selj, fnw.reshape(1, d))


def _tail(x, mk, mv, prm, seq, tm, mem_is_shared, tm_peer):
    if mem_is_shared:
        x = mem_attn_prompt(x, prm["norm_mem_w"], prm["w_mq"], mk, mv, prm["w_mo"], seq, tm)
    else:
        (q,) = norm_matmul(x, prm["norm_mem_w"], prm["w_mq"], (D_MODEL,), tm)
        x = matmul_res(mem_sample(q, mk, mv), prm["w_mo"], x, tm)
    xnt, pq = norm_matmul(x, prm["norm_ffn_w"], prm["peer_wq"], (prm["peer_wq"].shape[1],), tm,
                          emit_ht=True)
    seli, selj = peer_select(pq, prm["peer_keys"], tm)
    return peer_dense(xnt, x, prm["peer_u"], prm["peer_vt"], seli, selj, prm["final_norm_w"],
                      tm_peer)


def kernel(x_prompt, x_sample, mem_prompt, cache_swa_k, cache_swa_v, state_ssm, state_conv, cache_mem_k, cache_mem_v, norm_mix_w, w_in, attn_sinks, rel_bias, attn_norm_w, conv_w, conv_b, dt_bias, a_log, d_skip, ssd_norm_w, w_out, norm_mem_w, mem_norm_w, w_mq, w_mk, w_mv, w_mo, norm_ffn_w, peer_wq, peer_keys, peer_u, peer_v, final_norm_w):
    assert w_in.shape[0] == 1, "single-layer step"
    batch, seq, d = x_prompt.shape
    nb = x_sample.shape[0]
    kvw = ATT_KV_HEADS * ATT_HEAD_DIM
    tm = 512

    n_main = D_MODEL + 2 * kvw + SSD_WIDTH + CONV_DIM
    w_in_p = jnp.concatenate(
        [w_in[0], jnp.zeros((d, HEAD_PAD - SSD_HEADS), F32)], axis=1).astype(BF16)
    in_splits = (D_MODEL, kvw, kvw, SSD_WIDTH, CONV_DIM, HEAD_PAD)
    assert n_main + SSD_HEADS == w_in.shape[2]
    pad_h = lambda v: jnp.pad(v.reshape(1, SSD_HEADS), ((0, 0), (0, HEAD_PAD - SSD_HEADS)))
    dtb, alog = pad_h(dt_bias[0]), pad_h(a_log[0])
    dskip_x = jnp.repeat(d_skip[0], SSD_HEAD_DIM).reshape(1, SSD_WIDTH)
    e01 = (jnp.arange(HEAD_PAD)[:, None] == jnp.arange(SSD_WIDTH)[None, :] // SSD_HEAD_DIM
           ).astype(BF16)
    cw, cb = conv_w[0], conv_b[0].reshape(1, CONV_DIM)
    w_out1 = w_out[0, :D_MODEL].astype(BF16)
    w_out2 = w_out[0, D_MODEL:].astype(BF16)
    peer_up, peer_vtp = pack_peer_weights(peer_u[0], peer_v[0])
    prm = dict(norm_mem_w=norm_mem_w[0], w_mq=w_mq[0].astype(BF16), w_mo=w_mo[0].astype(BF16),
               norm_ffn_w=norm_ffn_w[0], peer_wq=peer_wq[0].astype(BF16),
               peer_keys=peer_keys[0].reshape(2 * PEER_HEADS, N_KEYS, PEER_HALF).astype(BF16),
               peer_u=peer_up, peer_vt=peer_vtp, final_norm_w=final_norm_w)

    qi = jnp.arange(WINDOW)[:, None] + WINDOW
    bias_p = rel_bias_rows(rel_bias, _t5_bucket(qi - jnp.arange(2 * WINDOW)[None, :]))
    dist_s = jnp.broadcast_to(jnp.maximum(WINDOW - jnp.arange(2 * WINDOW), 0)[None, :],
                              (8, 2 * WINDOW))
    bias_s = rel_bias_rows(rel_bias, _t5_bucket(dist_s))[:, 0, :]
    sinks = attn_sinks[0]

    xp = x_prompt.reshape(batch * seq, d)
    q, k, v, z, xbc, dt_raw = norm_matmul(xp, norm_mix_w[0], w_in_p, in_splits, tm)
    o_att = swa_prompt(q, k, v, bias_p, sinks, batch, seq)
    y_ssd, p_state = ssd_prompt(xbc, dt_raw, cw, cb, dtb, alog, dskip_x, e01, batch, seq)
    x1 = merge(xp, o_att, y_ssd, z, attn_norm_w[0], ssd_norm_w[0], w_out1, w_out2, tm)
    mkv_w = jnp.concatenate([w_mk[0], w_mv[0]], axis=1).astype(BF16)
    mk, mv = norm_matmul(mem_prompt.reshape(batch * N_MEM, d), mem_norm_w[0], mkv_w, (d, d), tm)
    y_p = _tail(x1, mk.astype(BF16), mv.astype(BF16), prm, seq, tm, True, tm)

    xs_in = x_sample.reshape(nb, d)
    tms = nb
    qs, ks, vs, zs, xbcs, dts = norm_matmul(xs_in, norm_mix_w[0], w_in_p, in_splits, tms)
    own = (jnp.arange(kvw)[None, :] // ATT_HEAD_DIM) == (jnp.arange(ATT_HEADS)[:, None] // (ATT_HEADS // ATT_KV_HEADS))
    qx = jnp.where(own[None], jnp.tile(qs.reshape(nb, ATT_HEADS, ATT_HEAD_DIM), (1, 1, ATT_KV_HEADS)), 0.0)
    ck = cache_swa_k[0].reshape(nb, WINDOW, kvw)
    cv = cache_swa_v[0].reshape(nb, WINDOW, kvw)
    sink_b = jnp.broadcast_to(sinks[:, None], (ATT_HEADS, LANES))
    nk, nv, ox = swa_sample(qx, ck, cv, ks.reshape(nb, 1, kvw), vs.reshape(nb, 1, kvw), bias_s, sink_b)
    ox5 = ox.reshape(nb, ATT_KV_HEADS, ATT_HEADS // ATT_KV_HEADS, ATT_KV_HEADS, ATT_HEAD_DIM)
    o_att_s = jnp.stack([ox5[:, g, :, g, :] for g in range(ATT_KV_HEADS)], axis=1).reshape(nb, d)
    sc = state_conv[0]
    dec_t, xdt_t, bmat, cmat, xs_s = ssd_sample_prep(xbcs, sc[:, 0], sc[:, 1], sc[:, 2], dts, cw, cb, dtb, alog, e01)
    new_state, y_s = ssd_sample_step(state_ssm[0].reshape(nb, SSD_WIDTH, D_STATE), dec_t, xdt_t, bmat, cmat, xs_s, dskip_x)
    x1s = merge(xs_in, o_att_s, y_s, zs, attn_norm_w[0], ssd_norm_w[0], w_out1, w_out2, tms)
    y_s_out = _tail(x1s, cache_mem_k[0], cache_mem_v[0], prm, seq, tms, False, tms)

    k4 = k.reshape(batch, seq, ATT_KV_HEADS, ATT_HEAD_DIM)
    v4 = v.reshape(batch, seq, ATT_KV_HEADS, ATT_HEAD_DIM)
    xbc3 = xbc.reshape(batch, seq, CONV_DIM)
    return (y_p.reshape(batch, seq, d),
            y_s_out.reshape(nb, 1, d),
            k4[None, :, seq - WINDOW:],
            v4[None, :, seq - WINDOW:],
            p_state.reshape(1, batch, SSD_HEADS, SSD_HEAD_DIM, D_STATE),
            xbc3[None, :, seq - (CONV_WIDTH - 1):],
            mk.reshape(1, batch, N_MEM, MEM_HEADS, MEM_HEAD_DIM),
            mv.reshape(1, batch, N_MEM, MEM_HEADS, MEM_HEAD_DIM),
            nk.reshape(1, nb, WINDOW, ATT_KV_HEADS, ATT_HEAD_DIM),
            nv.reshape(1, nb, WINDOW, ATT_KV_HEADS, ATT_HEAD_DIM),
            new_state.reshape(1, nb, SSD_HEADS, SSD_HEAD_DIM, D_STATE),
            jnp.concatenate([sc[:, 1:], xbcs[:, None, :]], axis=1)[None])
```

```python
import functools
import math

import numpy as np
import jax
import jax.numpy as jnp
from jax import lax
from jax.experimental import pallas as pl
from jax.experimental.pallas import tpu as pltpu

F32 = jnp.float32
BF16 = jnp.bfloat16
EPS = 1e-6

D_MODEL = 1024
ATT_HEADS = 16
ATT_KV_HEADS = 4
ATT_HEAD_DIM = 64
WINDOW = 128
N_BUCKETS = 32
MAX_DISTANCE = 128
SSD_HEADS = 16
SSD_HEAD_DIM = 64
SSD_WIDTH = SSD_HEADS * SSD_HEAD_DIM
SSD_GROUPS = 2
D_STATE = 128
CONV_WIDTH = 4
CONV_DIM = SSD_WIDTH + 2 * SSD_GROUPS * D_STATE
CHUNK = 128
N_MEM = 256
MEM_HEADS = 4
MEM_HEAD_DIM = D_MODEL // MEM_HEADS
PEER_HEADS = 8
N_KEYS = 128
N_EXPERTS = N_KEYS * N_KEYS
PEER_TOPK = 16
PEER_HALF = 128

LANES = 128
HEAD_PAD = 128
GROUP_W = SSD_WIDTH // SSD_GROUPS
VMEM_LIMIT = 56 * 1024 * 1024


def _params(sem):
    return pltpu.CompilerParams(dimension_semantics=sem, vmem_limit_bytes=VMEM_LIMIT)


def _dot(a, b):
    return jnp.dot(a, b, preferred_element_type=F32)


def _dot_nt(a, b):
    return lax.dot_general(a, b, (((1,), (1,)), ((), ())), preferred_element_type=F32)


def _rms(x, w):
    var = jnp.mean(x * x, axis=-1, keepdims=True)
    return x * lax.rsqrt(var + EPS) * w


def _sigmoid(x):
    return 1.0 / (1.0 + jnp.exp(-x))


def _softplus(x):
    return jnp.maximum(x, 0.0) + jnp.log1p(jnp.exp(-jnp.abs(x)))


def _split3(v):
    v1 = v.astype(BF16)
    r1 = v - v1.astype(F32)
    v2 = r1.astype(BF16)
    r2 = r1 - v2.astype(F32)
    return v1, v2, r2.astype(BF16)


def _exact_dot(v, m01):
    v1, v2, v3 = _split3(v)
    return _dot(v1, m01) + _dot(v2, m01) + _dot(v3, m01)


def _exact_dot_left(m01, v):
    v1, v2, v3 = _split3(v)
    return _dot(m01, v1) + _dot(m01, v2) + _dot(m01, v3)


def _norm_matmul_kernel(x_ref, nw_ref, w_ref, *out_refs, splits, emit_ht):
    hf = _rms(x_ref[...], nw_ref[...])
    h = hf.astype(BF16)
    refs = list(out_refs)
    if emit_ht:
        refs.pop(0)[...] = pltpu.bitcast(hf.T.astype(BF16), jnp.uint32)
    off = 0
    for o_ref, n in zip(refs, splits):
        o_ref[...] = _dot(h, w_ref[:, off:off + n])
        off += n


def norm_matmul(x, nw, w_bf16, splits, tm, emit_ht=False):
    t, d = x.shape
    n = w_bf16.shape[1]
    assert sum(splits) == n and t % tm == 0
    out_shape = [jax.ShapeDtypeStruct((t, s), F32) for s in splits]
    out_specs = [pl.BlockSpec((tm, s), lambda i: (i, 0)) for s in splits]
    if emit_ht:
        out_shape.insert(0, jax.ShapeDtypeStruct((d // 2, t), jnp.uint32))
        out_specs.insert(0, pl.BlockSpec((d // 2, tm), lambda i: (0, i)))
    return pl.pallas_call(
        functools.partial(_norm_matmul_kernel, splits=tuple(splits), emit_ht=emit_ht),
        grid=(t // tm,),
        in_specs=[pl.BlockSpec((tm, d), lambda i: (i, 0)),
                  pl.BlockSpec((1, d), lambda i: (0, 0)),
                  pl.BlockSpec((d, n), lambda i: (0, 0))],
        out_specs=out_specs,
        out_shape=out_shape,
        compiler_params=_params(("parallel",)),
        name="norm_matmul",
    )(x, nw.reshape(1, d), w_bf16)


def _bias_kernel(table_ref, bucket_ref, out_ref):
    h = pl.program_id(0)
    bk = bucket_ref[...]
    acc = jnp.zeros(bk.shape, F32)
    for b in range(N_BUCKETS):
        acc = jnp.where(bk == b, table_ref[b, h], acc)
    out_ref[0] = acc


def rel_bias_rows(table, bucket):
    r, c = bucket.shape
    return pl.pallas_call(
        _bias_kernel,
        grid=(ATT_HEADS,),
        in_specs=[pl.BlockSpec(memory_space=pltpu.SMEM),
                  pl.BlockSpec((r, c), lambda h: (0, 0))],
        out_specs=pl.BlockSpec((1, r, c), lambda h: (h, 0, 0)),
        out_shape=jax.ShapeDtypeStruct((ATT_HEADS, r, c), F32),
        compiler_params=_params(("parallel",)),
        name="rel_bias",
    )(table, bucket)


def _t5_bucket(dist):
    n = jnp.maximum(dist, 0)
    max_exact = N_BUCKETS // 2
    nf = jnp.maximum(n, 1).astype(F32)
    large = max_exact + (jnp.log(nf / max_exact) / math.log(MAX_DISTANCE / max_exact)
                         * (N_BUCKETS - max_exact)).astype(jnp.int32)
    large = jnp.minimum(large, N_BUCKETS - 1)
    return jnp.where(n < max_exact, n, large)


def _half_pair(pair, rolled, which, lo):
    if which == 0:
        return jnp.where(lo, pair, 0.0), jnp.where(lo, 0.0, rolled)
    return jnp.where(lo, rolled, 0.0), jnp.where(lo, 0.0, pair)


def _swa_prompt_kernel(sink_ref, q_ref, kc_ref, kp_ref, vc_ref, vp_ref, bias_ref, o_ref):
    i = pl.program_id(1)
    blk = WINDOW
    row = lax.broadcasted_iota(jnp.int32, (blk, 2 * blk), 0)
    col = lax.broadcasted_iota(jnp.int32, (blk, 2 * blk), 1)
    dist = row + blk - col
    ok1 = (dist >= 0) & (dist <= WINDOW) & ((col >= blk) | (i > 0))
    ok = jnp.concatenate([ok1, ok1], axis=0)
    first = lax.broadcasted_iota(jnp.int32, (2 * blk, 1), 0) < blk
    lo = lax.broadcasted_iota(jnp.int32, (2 * blk, LANES), 1) < ATT_HEAD_DIM
    scale = ATT_HEAD_DIM ** -0.5
    for pair in range(ATT_KV_HEADS // 2):
        sl = slice(pair * LANES, (pair + 1) * LANES)
        kpair = jnp.concatenate([kp_ref[:, sl], kc_ref[:, sl]], axis=0)
        vpair = jnp.concatenate([vp_ref[:, sl], vc_ref[:, sl]], axis=0)
        kroll = pltpu.roll(kpair, ATT_HEAD_DIM, axis=1)
        vroll = pltpu.roll(vpair, ATT_HEAD_DIM, axis=1)
        for gi in range(2):
            g = 2 * pair + gi
            k_lo, k_hi = _half_pair(kpair, kroll, gi, lo)
            v_lo, v_hi = _half_pair(vpair, vroll, gi, lo)
            k_halves = (k_lo.astype(BF16), k_hi.astype(BF16))
            v_halves = (v_lo.astype(BF16), v_hi.astype(BF16))
            qsl = slice(2 * g * LANES, (2 * g + 2) * LANES)
            q2 = jnp.concatenate([q_ref[:, 2 * g * LANES:(2 * g + 1) * LANES],
                                  q_ref[:, (2 * g + 1) * LANES:(2 * g + 2) * LANES]],
                                 axis=0).astype(BF16)
            acc = jnp.zeros((2 * blk, LANES), F32)
            for par in range(2):
                h0, h1 = 4 * g + par, 4 * g + 2 + par
                bias = jnp.concatenate([bias_ref[h0], bias_ref[h1]], axis=0)
                s = jnp.where(ok, _dot_nt(q2, k_halves[par]) * scale + bias, -jnp.inf)
                sink = jnp.where(first, sink_ref[h0], sink_ref[h1])
                m = jnp.maximum(jnp.max(s, axis=-1, keepdims=True), sink)
                p = jnp.exp(s - m)
                den = jnp.sum(p, axis=-1, keepdims=True) + jnp.exp(sink - m)
                acc = acc + _dot((p / den).astype(BF16), v_halves[par])
            o_ref[:, qsl] = jnp.concatenate([acc[:blk], acc[blk:]], axis=1)


def swa_prompt(q, k, v, bias, sinks, batch, seq):
    nb = seq // WINDOW
    cur = lambda b, i: (b * nb + i, 0)
    prev = lambda b, i: (b * nb + jnp.maximum(i - 1, 0), 0)
    kvw = ATT_KV_HEADS * ATT_HEAD_DIM
    return pl.pallas_call(
        _swa_prompt_kernel,
        grid=(batch, nb),
        in_specs=[pl.BlockSpec(memory_space=pltpu.SMEM),
                  pl.BlockSpec((WINDOW, D_MODEL), cur),
                  pl.BlockSpec((WINDOW, kvw), cur),
                  pl.BlockSpec((WINDOW, kvw), prev),
                  pl.BlockSpec((WINDOW, kvw), cur),
                  pl.BlockSpec((WINDOW, kvw), prev),
                  pl.BlockSpec((ATT_HEADS, WINDOW, 2 * WINDOW), lambda b, i: (0, 0, 0))],
        out_specs=pl.BlockSpec((WINDOW, D_MODEL), cur),
        out_shape=jax.ShapeDtypeStruct((batch * seq, D_MODEL), F32),
        compiler_params=_params(("parallel", "parallel")),
        name="swa_prompt",
    )(sinks, q, k, k, v, v, bias)


def _ssd_prompt_kernel(xbc_ref, dt_ref, cw_ref, cb_ref, dtb_ref, alog_ref, dskip_ref, e_ref,
                       y_ref, st_ref, state_scr, tail_scr):
    c = pl.program_id(1)
    last = pl.num_programs(1) - 1

    @pl.when(c == 0)
    def _():
        state_scr[...] = jnp.zeros(state_scr.shape, F32)
        tail_scr[...] = jnp.zeros(tail_scr.shape, F32)

    x = xbc_ref[...]
    tail = tail_scr[...]
    row8 = lax.broadcasted_iota(jnp.int32, tail.shape, 0)
    acc = x * cw_ref[CONV_WIDTH - 1:CONV_WIDTH, :]
    for s in range(1, CONV_WIDTH):
        xr = pltpu.roll(x, s, axis=0)
        tr = pltpu.roll(tail, s, axis=0)
        head = jnp.where(row8 < s, tr, xr[0:8])
        shifted = jnp.concatenate([head, xr[8:]], axis=0)
        acc = acc + shifted * cw_ref[CONV_WIDTH - 1 - s:CONV_WIDTH - s, :]
    tail_scr[...] = x[CHUNK - 8:CHUNK]
    u = acc + cb_ref[...]
    u = u * _sigmoid(u)
    xs = u[:, :SSD_WIDTH]
    bm = u[:, SSD_WIDTH:SSD_WIDTH + SSD_GROUPS * D_STATE]
    cm = u[:, SSD_WIDTH + SSD_GROUPS * D_STATE:]

    dtv = _softplus(dt_ref[...] + dtb_ref[...])
    a = dtv * (-jnp.exp(alog_ref[...]))
    r = lax.broadcasted_iota(jnp.int32, (CHUNK, CHUNK), 0)
    cc = lax.broadcasted_iota(jnp.int32, (CHUNK, CHUNK), 1)
    causal = r >= cc
    tril = jnp.where(causal, 1.0, 0.0).astype(BF16)
    cs = _exact_dot_left(tril, a)
    cs_t = cs.T
    dte = jnp.exp(cs[CHUNK - 1:CHUNK, :] - cs)
    ecs = jnp.exp(cs)
    e01 = e_ref[...]
    dt_x = _exact_dot(dtv, e01)
    dte_x = _exact_dot(dte, e01)
    ecs_x = _exact_dot(ecs, e01)
    xdt = xs * dt_x
    xdec = (xdt * dte_x).astype(BF16)
    xdt_b = xdt.astype(BF16)
    lo = lax.broadcasted_iota(jnp.int32, (CHUNK, LANES), 1) < SSD_HEAD_DIM
    dsk = dskip_ref[...]
    for g in range(SSD_GROUPS):
        gs = slice(g * GROUP_W, (g + 1) * GROUP_W)
        bg = bm[:, g * D_STATE:(g + 1) * D_STATE]
        cg = cm[:, g * D_STATE:(g + 1) * D_STATE].astype(BF16)
        cb = _dot_nt(cg, bg.astype(BF16))
        bgt = bg.T.astype(BF16)
        prev = state_scr[g]
        y_off = _dot(cg, prev.astype(BF16)) * ecs_x[:, gs]
        state_scr[g] = ecs_x[CHUNK - 1:CHUNK, gs] * prev + _dot(bgt, xdec[:, gs])
        for m in range(GROUP_W // LANES):
            ps = slice(g * GROUP_W + m * LANES, g * GROUP_W + (m + 1) * LANES)
            xp = xdt_b[:, ps]
            ydiag = jnp.zeros((CHUNK, LANES), F32)
            for par in range(2):
                j = g * (SSD_HEADS // SSD_GROUPS) + 2 * m + par
                diff = cs[:, j:j + 1] - cs_t[j:j + 1, :]
                ldec = jnp.exp(jnp.where(causal, diff, -jnp.inf))
                mm = (cb * ldec).astype(BF16)
                xpm = jnp.where(lo, xp, 0.0) if par == 0 else jnp.where(lo, 0.0, xp)
                ydiag = ydiag + _dot(mm, xpm.astype(BF16))
            y_ref[:, ps] = ydiag + y_off[:, m * LANES:(m + 1) * LANES] + dsk[:, ps] * xs[:, ps]

    @pl.when(c == last)
    def _():
        for g in range(SSD_GROUPS):
            st_ref[0, g * GROUP_W:(g + 1) * GROUP_W, :] = state_scr[g].T


def ssd_prompt(xbc, dt_raw, cw, cb, dtb, alog, dskip_x, e01, batch, seq):
    nc = seq // CHUNK
    cur = lambda b, c: (b * nc + c, 0)
    const = lambda b, c: (0, 0)
    return pl.pallas_call(
        _ssd_prompt_kernel,
        grid=(batch, nc),
        in_specs=[pl.BlockSpec((CHUNK, CONV_DIM), cur),
                  pl.BlockSpec((CHUNK, HEAD_PAD), cur),
                  pl.BlockSpec((CONV_WIDTH, CONV_DIM), const),
                  pl.BlockSpec((1, CONV_DIM), const),
                  pl.BlockSpec((1, HEAD_PAD), const),
                  pl.BlockSpec((1, HEAD_PAD), const),
                  pl.BlockSpec((1, SSD_WIDTH), const),
                  pl.BlockSpec((HEAD_PAD, SSD_WIDTH), const)],
        out_specs=[pl.BlockSpec((CHUNK, SSD_WIDTH), cur),
                   pl.BlockSpec((1, SSD_WIDTH, D_STATE), lambda b, c: (b, 0, 0))],
        out_shape=[jax.ShapeDtypeStruct((batch * seq, SSD_WIDTH), F32),
                   jax.ShapeDtypeStruct((batch, SSD_WIDTH, D_STATE), F32)],
        scratch_shapes=[pltpu.VMEM((SSD_GROUPS, D_STATE, GROUP_W), F32),
                        pltpu.VMEM((8, CONV_DIM), F32)],
        compiler_params=_params(("parallel", "arbitrary")),
        name="ssd_prompt",
    )(xbc, dt_raw, cw, cb, dtb, alog, dskip_x, e01)


def _swa_sample_kernel(qx_ref, ck_ref, cv_ref, kn_ref, vn_ref, bias_ref, sink_ref,
                       nk_ref, nv_ref, o_ref, *, bt):
    scale = ATT_HEAD_DIM ** -0.5
    kvw = ATT_KV_HEADS * ATT_HEAD_DIM
    rowi = lax.broadcasted_iota(jnp.int32, (WINDOW, kvw), 0)
    bias_c = bias_ref[:, 0:WINDOW]
    bias_n = bias_ref[:, WINDOW:WINDOW + 1]
    sink = sink_ref[:, 0:1]
    for bb in range(bt):
        kc = ck_ref[bb]
        vc = cv_ref[bb]
        kn = kn_ref[bb]
        vn = vn_ref[bb]
        qx = qx_ref[bb]
        s_c = _dot_nt(qx.astype(BF16), kc.astype(BF16)) * scale + bias_c
        s_n = jnp.sum(qx * kn, axis=1, keepdims=True) * scale + bias_n
        m = jnp.maximum(jnp.maximum(jnp.max(s_c, axis=1, keepdims=True), s_n), sink)
        p_c = jnp.exp(s_c - m)
        p_n = jnp.exp(s_n - m)
        den = jnp.sum(p_c, axis=1, keepdims=True) + p_n + jnp.exp(sink - m)
        o_ref[bb] = _dot((p_c / den).astype(BF16), vc.astype(BF16)) + (p_n / den) * vn
        nk_ref[bb] = jnp.where(rowi == WINDOW - 1, kn, pltpu.roll(kc, WINDOW - 1, axis=0))
        nv_ref[bb] = jnp.where(rowi == WINDOW - 1, vn, pltpu.roll(vc, WINDOW - 1, axis=0))


def swa_sample(qx, ck, cv, kn, vn, bias_s, sink_b, bt=8):
    nb = qx.shape[0]
    kvw = ATT_KV_HEADS * ATT_HEAD_DIM
    blk3 = lambda i: (i, 0, 0)
    return pl.pallas_call(
        functools.partial(_swa_sample_kernel, bt=bt),
        grid=(nb // bt,),
        in_specs=[pl.BlockSpec((bt, ATT_HEADS, kvw), blk3),
                  pl.BlockSpec((bt, WINDOW, kvw), blk3),
                  pl.BlockSpec((bt, WINDOW, kvw), blk3),
                  pl.BlockSpec((bt, 1, kvw), blk3),
                  pl.BlockSpec((bt, 1, kvw), blk3),
                  pl.BlockSpec((ATT_HEADS, 2 * WINDOW), lambda i: (0, 0)),
                  pl.BlockSpec((ATT_HEADS, LANES), lambda i: (0, 0))],
        out_specs=[pl.BlockSpec((bt, WINDOW, kvw), blk3),
                   pl.BlockSpec((bt, WINDOW, kvw), blk3),
                   pl.BlockSpec((bt, ATT_HEADS, kvw), blk3)],
        out_shape=[jax.ShapeDtypeStruct((nb, WINDOW, kvw), F32),
                   jax.ShapeDtypeStruct((nb, WINDOW, kvw), F32),
                   jax.ShapeDtypeStruct((nb, ATT_HEADS, kvw), F32)],
        compiler_params=_params(("parallel",)),
        name="swa_sample",
    )(qx, ck, cv, kn, vn, bias_s, sink_b)


def _ssd_sample_prep_kernel(xbc_ref, c0_ref, c1_ref, c2_ref, dt_ref, cw_ref, cb_ref, dtb_ref,
                            alog_ref, e_ref, dec_t_ref, xdt_t_ref, b_ref, c_ref, xs_ref):
    u = (c0_ref[...] * cw_ref[0:1, :] + c1_ref[...] * cw_ref[1:2, :]
         + c2_ref[...] * cw_ref[2:3, :] + xbc_ref[...] * cw_ref[3:4, :]) + cb_ref[...]
    u = u * _sigmoid(u)
    xs = u[:, :SSD_WIDTH]
    b_ref[...] = u[:, SSD_WIDTH:SSD_WIDTH + SSD_GROUPS * D_STATE]
    c_ref[...] = u[:, SSD_WIDTH + SSD_GROUPS * D_STATE:]
    xs_ref[...] = xs
    dtv = _softplus(dt_ref[...] + dtb_ref[...])
    dec = jnp.exp(dtv * (-jnp.exp(alog_ref[...])))
    e01 = e_ref[...]
    dec_t_ref[...] = _exact_dot(dec, e01).T
    xdt_t_ref[...] = (xs * _exact_dot(dtv, e01)).T


def ssd_sample_prep(xbc, c0, c1, c2, dt_raw, cw, cb, dtb, alog, e01):
    nb = xbc.shape[0]
    args = (xbc, c0, c1, c2, dt_raw, cw, cb, dtb, alog, e01)
    full = lambda a: pl.BlockSpec(a.shape, lambda i: (0,) * a.ndim)
    out_shape = [jax.ShapeDtypeStruct((SSD_WIDTH, nb), F32),
                 jax.ShapeDtypeStruct((SSD_WIDTH, nb), F32),
                 jax.ShapeDtypeStruct((nb, SSD_GROUPS * D_STATE), F32),
                 jax.ShapeDtypeStruct((nb, SSD_GROUPS * D_STATE), F32),
                 jax.ShapeDtypeStruct((nb, SSD_WIDTH), F32)]
    return pl.pallas_call(
        _ssd_sample_prep_kernel,
        grid=(1,),
        in_specs=[full(a) for a in args],
        out_specs=[full(s) for s in out_shape],
        out_shape=out_shape,
        compiler_params=_params(("arbitrary",)),
        name="ssd_sample_prep",
    )(*args)


def _ssd_sample_step_kernel(st_ref, dec_t_ref, xdt_t_ref, b_ref, c_ref, xs_ref, dskip_ref,
                            ns_ref, y_ref, yt_scr, *, bt):
    i = pl.program_id(0)
    nb = dec_t_ref.shape[1]

    @pl.when(i == 0)
    def _():
        yt_scr[...] = jnp.zeros(yt_scr.shape, F32)

    dparts = _split3(dec_t_ref[...])
    xparts = _split3(xdt_t_ref[...])
    lane = lax.broadcasted_iota(jnp.int32, (GROUP_W, nb), 1)
    for bb in range(bt):
        b = i * bt + bb
        pick = jnp.where(lax.broadcasted_iota(jnp.int32, (nb, D_STATE), 0) == b,
                         1.0, 0.0).astype(BF16)
        dcol = sum(_dot(p, pick) for p in dparts)
        xcol = sum(_dot(p, pick) for p in xparts)
        brow = b_ref[pl.ds(b, 1), :]
        crow = c_ref[pl.ds(b, 1), :]
        for g in range(SSD_GROUPS):
            gs = slice(g * GROUP_W, (g + 1) * GROUP_W)
            ns = slice(g * D_STATE, (g + 1) * D_STATE)
            hn = dcol[gs] * st_ref[bb, gs, :] + xcol[gs] * brow[:, ns]
            ns_ref[bb, gs, :] = hn
            ycol = jnp.sum(hn * crow[:, ns], axis=1, keepdims=True)
            yt_scr[gs, :] = jnp.where(lane == b, ycol, yt_scr[gs, :])

    @pl.when(i == pl.num_programs(0) - 1)
    def _():
        y_ref[...] = yt_scr[...].T + dskip_ref[...] * xs_ref[...]


def ssd_sample_step(state, dec_t, xdt_t, bmat, cmat, xs, dskip_x, bt=4):
    nb = state.shape[0]
    assert nb == D_STATE
    const2 = lambda i: (0, 0)
    return pl.pallas_call(
        functools.partial(_ssd_sample_step_kernel, bt=bt),
        grid=(nb // bt,),
        in_specs=[pl.BlockSpec((bt, SSD_WIDTH, D_STATE), lambda i: (i, 0, 0)),
                  pl.BlockSpec((SSD_WIDTH, nb), const2),
                  pl.BlockSpec((SSD_WIDTH, nb), const2),
                  pl.BlockSpec((nb, SSD_GROUPS * D_STATE), const2),
                  pl.BlockSpec((nb, SSD_GROUPS * D_STATE), const2),
                  pl.BlockSpec((nb, SSD_WIDTH), const2),
                  pl.BlockSpec((1, SSD_WIDTH), const2)],
        out_specs=[pl.BlockSpec((bt, SSD_WIDTH, D_STATE), lambda i: (i, 0, 0)),
                   pl.BlockSpec((nb, SSD_WIDTH), const2)],
        out_shape=[jax.ShapeDtypeStruct((nb, SSD_WIDTH, D_STATE), F32),
                   jax.ShapeDtypeStruct((nb, SSD_WIDTH), F32)],
        scratch_shapes=[pltpu.VMEM((SSD_WIDTH, nb), F32)],
        compiler_params=_params(("arbitrary",)),
        name="ssd_sample_step",
    )(state, dec_t, xdt_t, bmat, cmat, xs, dskip_x)


def _merge_kernel(x_ref, o_ref, y_ref, z_ref, anw_ref, snw_ref, w1_ref, w2_ref, out_ref):
    ya = _rms(o_ref[...], anw_ref[...]).astype(BF16)
    z = z_ref[...]
    ys = _rms(y_ref[...] * (z * _sigmoid(z)), snw_ref[...]).astype(BF16)
    out_ref[...] = x_ref[...] + _dot(ya, w1_ref[...]) + _dot(ys, w2_ref[...])


def merge(x, o_att, y_ssd, z, anw, snw, w1, w2, tm):
    t, d = x.shape
    row = pl.BlockSpec((tm, d), lambda i: (i, 0))
    vec = pl.BlockSpec((1, d), lambda i: (0, 0))
    mat = pl.BlockSpec((d, d), lambda i: (0, 0))
    return pl.pallas_call(
        _merge_kernel,
        grid=(t // tm,),
        in_specs=[row, row, row, row, vec, vec, mat, mat],
        out_specs=row,
        out_shape=jax.ShapeDtypeStruct((t, d), F32),
        compiler_params=_params(("parallel",)),
        name="merge",
    )(x, o_att, y_ssd, z, anw.reshape(1, d), snw.reshape(1, d), w1, w2)


def _mem_prompt_kernel(x_ref, nw_ref, wq_ref, mk_ref, mv_ref, wo_ref, out_ref):
    x = x_ref[...]
    q = _dot(_rms(x, nw_ref[...]).astype(BF16), wq_ref[...])
    scale = MEM_HEAD_DIM ** -0.5
    outs = []
    for h in range(MEM_HEADS):
        hs = slice(h * MEM_HEAD_DIM, (h + 1) * MEM_HEAD_DIM)
        s = _dot_nt(q[:, hs].astype(BF16), mk_ref[:, hs]) * scale
        p = jnp.exp(s - jnp.max(s, axis=-1, keepdims=True))
        p = p / jnp.sum(p, axis=-1, keepdims=True)
        outs.append(_dot(p.astype(BF16), mv_ref[:, hs]).astype(BF16))
    out_ref[...] = x + _dot(jnp.concatenate(outs, axis=1), wo_ref[...])


def mem_attn_prompt(x, nw, wq, mk, mv, wo, seq, tm):
    t, d = x.shape
    per = seq // tm
    row = pl.BlockSpec((tm, d), lambda i: (i, 0))
    mat = pl.BlockSpec((d, d), lambda i: (0, 0))
    mem = pl.BlockSpec((N_MEM, d), lambda i: (i // per, 0))
    return pl.pallas_call(
        _mem_prompt_kernel,
        grid=(t // tm,),
        in_specs=[row, pl.BlockSpec((1, d), lambda i: (0, 0)), mat, mem, mem, mat],
        out_specs=row,
        out_shape=jax.ShapeDtypeStruct((t, d), F32),
        compiler_params=_params(("parallel",)),
        name="mem_prompt",
    )(x, nw.reshape(1, d), wq, mk, mv, wo)


def _mem_sample_kernel(q_ref, k_ref, v_ref, o_ref, *, bt):
    rows = N_MEM * MEM_HEADS
    qh = lax.broadcasted_iota(jnp.int32, (8, rows), 0)
    kh = lax.broadcasted_iota(jnp.int32, (8, rows), 1) % MEM_HEADS
    live = qh < MEM_HEADS
    for bb in range(bt):
        q4 = q_ref[bb]
        q8 = jnp.concatenate([q4, jnp.zeros_like(q4)], axis=0).astype(BF16)
        k2 = k_ref[bb].reshape(rows, MEM_HEAD_DIM).astype(BF16)
        v2 = v_ref[bb].reshape(rows, MEM_HEAD_DIM).astype(BF16)
        s = _dot_nt(q8, k2) * (MEM_HEAD_DIM ** -0.5)
        s = jnp.where(kh == qh, s, -jnp.inf)
        p = jnp.exp(s - jnp.where(live, jnp.max(s, axis=1, keepdims=True), 0.0))
        p = p / jnp.where(live, jnp.sum(p, axis=1, keepdims=True), 1.0)
        o_ref[bb] = _dot(p.astype(BF16), v2)[0:MEM_HEADS]


def mem_sample(q, ck, cv, bt=2):
    nb, d = q.shape
    blk = pl.BlockSpec((bt, N_MEM, MEM_HEADS, MEM_HEAD_DIM), lambda i: (i, 0, 0, 0))
    vec = pl.BlockSpec((bt, MEM_HEADS, MEM_HEAD_DIM), lambda i: (i, 0, 0))
    return pl.pallas_call(
        functools.partial(_mem_sample_kernel, bt=bt),
        grid=(nb // bt,),
        in_specs=[vec, blk, blk],
        out_specs=vec,
        out_shape=jax.ShapeDtypeStruct((nb, MEM_HEADS, MEM_HEAD_DIM), F32),
        compiler_params=_params(("parallel",)),
        name="mem_sample",
    )(q.reshape(nb, MEM_HEADS, MEM_HEAD_DIM), ck, cv).reshape(nb, d)


def _matmul_res_kernel(a_ref, w_ref, r_ref, out_ref):
    out_ref[...] = r_ref[...] + _dot(a_ref[...].astype(BF16), w_ref[...])


def matmul_res(a, w, res, tm):
    t, d = res.shape
    kdim = a.shape[1]
    return pl.pallas_call(
        _matmul_res_kernel,
        grid=(t // tm,),
        in_specs=[pl.BlockSpec((tm, kdim), lambda i: (i, 0)),
                  pl.BlockSpec((kdim, d), lambda i: (0, 0)),
                  pl.BlockSpec((tm, d), lambda i: (i, 0))],
        out_specs=pl.BlockSpec((tm, d), lambda i: (i, 0)),
        out_shape=jax.ShapeDtypeStruct((t, d), F32),
        compiler_params=_params(("parallel",)),
        name="matmul_res",
    )(a, w, res)


def _topk_rows(work, k):
    rows = work.shape[0]
    iota = lax.broadcasted_iota(jnp.int32, work.shape, 0)
    rank = jnp.full(work.shape, k, jnp.int32)
    vals = []
    for r in range(k):
        m = jnp.max(work, axis=0, keepdims=True)
        idx = jnp.min(jnp.where(work == m, iota, rows), axis=0, keepdims=True)
        hit = iota == idx
        rank = jnp.where(hit, r, rank)
        work = jnp.where(hit, -jnp.inf, work)
        vals.append(m)
    return jnp.concatenate(vals, axis=0), rank


def _topk_rows_distinct(work, k, want_rank):
    rank = jnp.full(work.shape, k, jnp.int32) if want_rank else None
    vals = []
    for r in range(k):
        m = jnp.max(work, axis=0, keepdims=True)
        hit = work == m
        if want_rank:
            rank = jnp.where(hit, r, rank)
        work = jnp.where(hit, -jnp.inf, work)
        vals.append(m)
    taken = work == -jnp.inf
    count = jnp.sum(jnp.where(taken, 1.0, 0.0), axis=0, keepdims=True)
    return jnp.concatenate(vals, axis=0), rank, taken, count


def _peer_candidates(a, b):
    lo4 = lax.broadcasted_iota(jnp.int32, (8, a.shape[1]), 0) < 4
    b8 = b[0:8]
    b44 = jnp.where(lo4, b8, pltpu.roll(b8, 4, axis=0))
    return jnp.concatenate(
        [a[0:1] + b8, a[0:1] + b[8:16], a[1:2] + b8, a[2:3] + b8, a[3:4] + b8,
         jnp.where(lo4, a[4:5], a[5:6]) + b44, jnp.where(lo4, a[6:7], a[7:8]) + b44,
         a[8:16] + b[0:1]], axis=0)


def _peer_emit(subs, a, b, rank0, rank1, cand, chosen, seli_ref, selj_ref):
    k = PEER_TOPK
    cmax = a[0:1, :] + b[0:1, :]
    z = jnp.sum(jnp.where(chosen, jnp.exp(cand - cmax), 0.0), axis=0, keepdims=True)
    lo4 = lax.broadcasted_iota(jnp.int32, (8, a.shape[1]), 0) < 4
    ch = [jnp.where(chosen[8 * i:8 * i + 8], 1.0, 0.0) for i in range(8)]
    cnt = lambda v: jnp.sum(v, axis=0, keepdims=True)
    n = [cnt(ch[0]) + cnt(ch[1]), cnt(ch[2]), cnt(ch[3]), cnt(ch[4]),
         cnt(jnp.where(lo4, ch[5], 0.0)), cnt(jnp.where(lo4, 0.0, ch[5])),
         cnt(jnp.where(lo4, ch[6], 0.0)), cnt(jnp.where(lo4, 0.0, ch[6]))]
    n += [ch[7][i:i + 1] for i in range(8)]
    lim0 = jnp.zeros(subs[0].shape, F32)
    if rank0 is None:
        for r in reversed(range(k)):
            lim0 = jnp.where(subs[0] >= a[r:r + 1, :], n[r], lim0)
    else:
        for r in range(k):
            lim0 = jnp.where(rank0 == r, n[r], lim0)

    def twice(v):
        bits = pltpu.bitcast(v.astype(BF16).astype(F32), jnp.uint32)
        return bits | (bits >> 16)

    seli_ref[0, 0] = twice(0.5 * jnp.exp(subs[0] - a[0:1, :]) / z)
    seli_ref[0, 1] = twice(lim0)
    selj_ref[0, 0] = pltpu.bitcast(jnp.exp(subs[1] - b[0:1, :]).astype(BF16), jnp.uint32)
    selj_ref[0, 1] = pltpu.bitcast(rank1.astype(F32).astype(BF16), jnp.uint32)


def _peer_select_kernel(pq_ref, keys_ref, seli_ref, selj_ref):
    k = PEER_TOPK
    subs = [_dot_nt(keys_ref[p], pq_ref[:, p * PEER_HALF:(p + 1) * PEER_HALF].astype(BF16))
            for p in range(2)]
    a, _, _, took0 = _topk_rows_distinct(subs[0], k, want_rank=False)
    b, rank1, _, took1 = _topk_rows_distinct(subs[1], k, want_rank=True)
    cand = _peer_candidates(a, b)
    _, _, chosen, tookc = _topk_rows_distinct(cand, k, want_rank=False)
    _peer_emit(subs, a, b, None, rank1, cand, chosen, seli_ref, selj_ref)
    tied = jnp.where((took0 != k) | (took1 != k) | (tookc != k), 1.0, 0.0)

    @pl.when(jnp.max(tied) > 0.0)
    def _():
        a, rank0 = _topk_rows(subs[0], k)
        b, rank1 = _topk_rows(subs[1], k)
        cand = _peer_candidates(a, b)
        _, crank = _topk_rows(cand, k)
        _peer_emit(subs, a, b, rank0, rank1, cand, crank < k, seli_ref, selj_ref)


def peer_select(pq, keys_bf16, tm):
    t = pq.shape[0]
    spec = lambda rows: pl.BlockSpec((1, 2, rows, tm), lambda i, h: (h, 0, 0, i))
    return pl.pallas_call(
        _peer_select_kernel,
        grid=(t // tm, PEER_HEADS),
        in_specs=[pl.BlockSpec((tm, 2 * PEER_HALF), lambda i, h: (i, h)),
                  pl.BlockSpec((2, N_KEYS, PEER_HALF), lambda i, h: (h, 0, 0))],
        out_specs=[spec(N_KEYS), spec(N_KEYS // 2)],
        out_shape=[jax.ShapeDtypeStruct((PEER_HEADS, 2, N_KEYS, t), jnp.uint32),
                   jax.ShapeDtypeStruct((PEER_HEADS, 2, N_KEYS // 2, t), jnp.uint32)],
        compiler_params=_params(("parallel", "parallel")),
        name="peer_select",
    )(pq, keys_bf16)


def _gelu_x2(x):
    return x * (1.0 + lax.erf(x * np.float32(math.sqrt(0.5))))


def _peer_dense_kernel(xnt_ref, x_ref, u_ref, vt_ref, seli_ref, selj_ref, fnw_ref, y_ref,
                       acc_scr, act_scr, w_scr, *, tm, ib, sub):
    e = pl.program_id(1)
    ne = pl.num_programs(1)

    @pl.when(e == 0)
    def _():
        acc_scr[...] = jnp.zeros(acc_scr.shape, F32)

    i8 = pl.ds(pl.multiple_of(e * ib, 8), ib)
    nsub = ib // sub
    d = acc_scr.shape[0]
    halves = 2
    sub_rows = lambda k: slice(k * sub * N_KEYS, (k + 1) * sub * N_KEYS)

    def activations(k, half):
        cs = slice(half * tm // halves, (half + 1) * tm // halves)
        u = pltpu.bitcast(u_ref[k * sub * N_KEYS // 2:(k + 1) * sub * N_KEYS // 2, :], BF16)
        act_scr[k % act_scr.shape[0], :, cs] = _dot(u, pltpu.bitcast(xnt_ref[:, cs], BF16))

    def accumulate(k, half):
        rs = slice(half * d // halves, (half + 1) * d // halves)
        ws = slice(half * d // (2 * halves), (half + 1) * d // (2 * halves))
        acc_scr[rs, :] += _dot(pltpu.bitcast(vt_ref[0, ws, sub_rows(k)], BF16),
                               w_scr[k % w_scr.shape[0]])

    def gates(k, lt):
        ls = slice(lt * LANES, (lt + 1) * LANES)
        for j0 in range(0, sub, 2):
            gs = [jnp.zeros((N_KEYS, LANES), BF16) for _ in range(2)]
            for h in range(PEER_HEADS):
                e1 = pltpu.bitcast(selj_ref[h, 0, :, ls], BF16)
                rank1 = pltpu.bitcast(selj_ref[h, 1, :, ls], BF16)
                for t in range(2):
                    ii = k * sub + j0 + t
                    row = lambda c: pltpu.bitcast(jnp.broadcast_to(
                        seli_ref[h, c, i8, ls][ii:ii + 1, :], (N_KEYS // 2, LANES)), BF16)
                    gs[t] = gs[t] + jnp.where(rank1 < row(1), e1, jnp.zeros_like(e1)) * row(0)
            for t in range(2):
                rs = slice((j0 + t) * N_KEYS, (j0 + t + 1) * N_KEYS)
                act = act_scr[k % act_scr.shape[0], rs, ls]
                w_scr[k % w_scr.shape[0], rs, ls] = gs[t] * _gelu_x2(act).astype(BF16)

    for half in range(halves):
        activations(0, half)
    for k in range(nsub):
        mxu_jobs = []
        if k + 1 < nsub:
            mxu_jobs += [functools.partial(activations, k + 1, half) for half in range(halves)]
        if k >= 1:
            mxu_jobs += [functools.partial(accumulate, k - 1, half) for half in range(halves)]
        for job in mxu_jobs:
            job()
        for lt in range(tm // LANES):
            gates(k, lt)
    for half in range(halves):
        accumulate(nsub - 1, half)

    @pl.when(e == ne - 1)
    def _():
        y_ref[...] = _rms(x_ref[...] + acc_scr[...].T, fnw_ref[...])


PEER_BLOCK_ROWS = 16
PEER_SUB_ROWS = 4


def _pack_peer_kernel(u_ref, v_ref, up_ref, vtp_ref):
    up_ref[...] = pltpu.bitcast(u_ref[...].astype(BF16), jnp.uint32)
    vtp_ref[0] = pltpu.bitcast(v_ref[...].T.astype(BF16), jnp.uint32)


def pack_peer_weights(u, v, ib=PEER_BLOCK_ROWS):
    n, d = u.shape
    eb = ib * N_KEYS
    return pl.pallas_call(
        _pack_peer_kernel,
        grid=(n // eb,),
        in_specs=[pl.BlockSpec((eb, d), lambda e: (e, 0)), pl.BlockSpec((eb, d), lambda e: (e, 0))],
        out_specs=[pl.BlockSpec((eb // 2, d), lambda e: (e, 0)),
                   pl.BlockSpec((1, d // 2, eb), lambda e: (e, 0, 0))],
        out_shape=[jax.ShapeDtypeStruct((n // 2, d), jnp.uint32),
                   jax.ShapeDtypeStruct((n // eb, d // 2, eb), jnp.uint32)],
        compiler_params=_params(("parallel",)),
        name="pack_peer_weights",
    )(u, v)


def peer_dense(xnt, x, u_packed, vt_packed, seli, selj, fnw, tm, ib=PEER_BLOCK_ROWS,
               sub=PEER_SUB_ROWS):
    t, d = x.shape
    eb = ib * N_KEYS
    nblk = N_EXPERTS // eb
    assert vt_packed.shape == (nblk, d // 2, eb) and xnt.shape == (d // 2, t)
    sel_spec = lambda rows: pl.BlockSpec((PEER_HEADS, 2, rows, tm), lambda i, e: (0, 0, 0, i))
    return pl.pallas_call(
        functools.partial(_peer_dense_kernel, tm=tm, ib=ib, sub=sub),
        grid=(t // tm, nblk),
        in_specs=[pl.BlockSpec((d // 2, tm), lambda i, e: (0, i)),
                  pl.BlockSpec((tm, d), lambda i, e: (i, 0)),
                  pl.BlockSpec((eb // 2, d), lambda i, e: (e, 0)),
                  pl.BlockSpec((1, d // 2, eb), lambda i, e: (e, 0, 0)),
                  sel_spec(N_KEYS), sel_spec(N_KEYS // 2),
                  pl.BlockSpec((1, d), lambda i, e: (0, 0))],
        out_specs=pl.BlockSpec((tm, d), lambda i, e: (i, 0)),
        out_shape=jax.ShapeDtypeStruct((t, d), F32),
        scratch_shapes=[pltpu.VMEM((d, tm), F32),
                        pltpu.VMEM((min(2, ib // sub), sub * N_KEYS, tm), F32),
                        pltpu.VMEM((min(3, ib // sub), sub * N_KEYS, tm), BF16)],
        compiler_params=_params(("parallel", "arbitrary")),
        name="peer_dense",
    )(xnt, x, u_packed, vt_packed, seli, selj, fnw.reshape(1, d))


def _tail(x, mk, mv, prm, seq, tm, mem_is_shared, tm_peer):
    if mem_is_shared:
        x = mem_attn_prompt(x, prm["norm_mem_w"], prm["w_mq"], mk, mv, prm["w_mo"], seq, tm)
    else:
        (q,) = norm_matmul(x, prm["norm_mem_w"], prm["w_mq"], (D_MODEL,), tm)
        x = matmul_res(mem_sample(q, mk, mv), prm["w_mo"], x, tm)
    xnt, pq = norm_matmul(x, prm["norm_ffn_w"], prm["peer_wq"], (prm["peer_wq"].shape[1],), tm,
                          emit_ht=True)
    t = x.shape[0]
    seli, selj = peer_select(pq, prm["peer_keys"], 2 * tm if t % (2 * tm) == 0 else tm)
    return peer_dense(xnt, x, prm["peer_u"], prm["peer_vt"], seli, selj, prm["final_norm_w"],
                      tm_peer)


def kernel(x_prompt, x_sample, mem_prompt, cache_swa_k, cache_swa_v, state_ssm, state_conv, cache_mem_k, cache_mem_v, norm_mix_w, w_in, attn_sinks, rel_bias, attn_norm_w, conv_w, conv_b, dt_bias, a_log, d_skip, ssd_norm_w, w_out, norm_mem_w, mem_norm_w, w_mq, w_mk, w_mv, w_mo, norm_ffn_w, peer_wq, peer_keys, peer_u, peer_v, final_norm_w):
    assert w_in.shape[0] == 1, "single-layer step"
    batch, seq, d = x_prompt.shape
    nb = x_sample.shape[0]
    kvw = ATT_KV_HEADS * ATT_HEAD_DIM
    tm = 512

    n_main = D_MODEL + 2 * kvw + SSD_WIDTH + CONV_DIM
    w_in_p = jnp.concatenate(
        [w_in[0], jnp.zeros((d, HEAD_PAD - SSD_HEADS), F32)], axis=1).astype(BF16)
    in_splits = (D_MODEL, kvw, kvw, SSD_WIDTH, CONV_DIM, HEAD_PAD)
    assert n_main + SSD_HEADS == w_in.shape[2]
    pad_h = lambda v: jnp.pad(v.reshape(1, SSD_HEADS), ((0, 0), (0, HEAD_PAD - SSD_HEADS)))
    dtb, alog = pad_h(dt_bias[0]), pad_h(a_log[0])
    dskip_x = jnp.repeat(d_skip[0], SSD_HEAD_DIM).reshape(1, SSD_WIDTH)
    e01 = (jnp.arange(HEAD_PAD)[:, None] == jnp.arange(SSD_WIDTH)[None, :] // SSD_HEAD_DIM
           ).astype(BF16)
    cw, cb = conv_w[0], conv_b[0].reshape(1, CONV_DIM)
    w_out1 = w_out[0, :D_MODEL].astype(BF16)
    w_out2 = w_out[0, D_MODEL:].astype(BF16)
    peer_up, peer_vtp = pack_peer_weights(peer_u[0], peer_v[0])
    prm = dict(norm_mem_w=norm_mem_w[0], w_mq=w_mq[0].astype(BF16), w_mo=w_mo[0].astype(BF16),
               norm_ffn_w=norm_ffn_w[0], peer_wq=peer_wq[0].astype(BF16),
               peer_keys=peer_keys[0].reshape(2 * PEER_HEADS, N_KEYS, PEER_HALF).astype(BF16),
               peer_u=peer_up, peer_vt=peer_vtp, final_norm_w=final_norm_w)

    qi = jnp.arange(WINDOW)[:, None] + WINDOW
    bias_p = rel_bias_rows(rel_bias, _t5_bucket(qi - jnp.arange(2 * WINDOW)[None, :]))
    dist_s = jnp.broadcast_to(jnp.maximum(WINDOW - jnp.arange(2 * WINDOW), 0)[None, :],
                              (8, 2 * WINDOW))
    bias_s = rel_bias_rows(rel_bias, _t5_bucket(dist_s))[:, 0, :]
    sinks = attn_sinks[0]

    xp = x_prompt.reshape(batch * seq, d)
    q, k, v, z, xbc, dt_raw = norm_matmul(xp, norm_mix_w[0], w_in_p, in_splits, tm)
    o_att = swa_prompt(q, k, v, bias_p, sinks, batch, seq)
    y_ssd, p_state = ssd_prompt(xbc, dt_raw, cw, cb, dtb, alog, dskip_x, e01, batch, seq)
    x1 = merge(xp, o_att, y_ssd, z, attn_norm_w[0], ssd_norm_w[0], w_out1, w_out2, tm)
    mkv_w = jnp.concatenate([w_mk[0], w_mv[0]], axis=1).astype(BF16)
    mk, mv = norm_matmul(mem_prompt.reshape(batch * N_MEM, d), mem_norm_w[0], mkv_w, (d, d), tm)
    y_p = _tail(x1, mk.astype(BF16), mv.astype(BF16), prm, seq, tm, True, tm)

    xs_in = x_sample.reshape(nb, d)
    tms = nb
    qs, ks, vs, zs, xbcs, dts = norm_matmul(xs_in, norm_mix_w[0], w_in_p, in_splits, tms)
    own = (jnp.arange(kvw)[None, :] // ATT_HEAD_DIM) == (jnp.arange(ATT_HEADS)[:, None] // (ATT_HEADS // ATT_KV_HEADS))
    qx = jnp.where(own[None], jnp.tile(qs.reshape(nb, ATT_HEADS, ATT_HEAD_DIM), (1, 1, ATT_KV_HEADS)), 0.0)
    ck = cache_swa_k[0].reshape(nb, WINDOW, kvw)
    cv = cache_swa_v[0].reshape(nb, WINDOW, kvw)
    sink_b = jnp.broadcast_to(sinks[:, None], (ATT_HEADS, LANES))
    nk, nv, ox = swa_sample(qx, ck, cv, ks.reshape(nb, 1, kvw), vs.reshape(nb, 1, kvw), bias_s, sink_b)
    ox5 = ox.reshape(nb, ATT_KV_HEADS, ATT_HEADS // ATT_KV_HEADS, ATT_KV_HEADS, ATT_HEAD_DIM)
    o_att_s = jnp.stack([ox5[:, g, :, g, :] for g in range(ATT_KV_HEADS)], axis=1).reshape(nb, d)
    sc = state_conv[0]
    dec_t, xdt_t, bmat, cmat, xs_s = ssd_sample_prep(xbcs, sc[:, 0], sc[:, 1], sc[:, 2], dts, cw, cb, dtb, alog, e01)
    new_state, y_s = ssd_sample_step(state_ssm[0].reshape(nb, SSD_WIDTH, D_STATE), dec_t, xdt_t, bmat, cmat, xs_s, dskip_x)
    x1s = merge(xs_in, o_att_s, y_s, zs, attn_norm_w[0], ssd_norm_w[0], w_out1, w_out2, tms)
    y_s_out = _tail(x1s, cache_mem_k[0], cache_mem_v[0], prm, seq, tms, False, tms)

    k4 = k.reshape(batch, seq, ATT_KV_HEADS, ATT_HEAD_DIM)
    v4 = v.reshape(batch, seq, ATT_KV_HEADS, ATT_HEAD_DIM)
    xbc3 = xbc.reshape(batch, seq, CONV_DIM)
    return (y_p.reshape(batch, seq, d),
            y_s_out.reshape(nb, 1, d),
            k4[None, :, seq - WINDOW:],
            v4[None, :, seq - WINDOW:],
            p_state.reshape(1, batch, SSD_HEADS, SSD_HEAD_DIM, D_STATE),
            xbc3[None, :, seq - (CONV_WIDTH - 1):],
            mk.reshape(1, batch, N_MEM, MEM_HEADS, MEM_HEAD_DIM),
            mv.reshape(1, batch, N_MEM, MEM_HEADS, MEM_HEAD_DIM),
            nk.reshape(1, nb, WINDOW, ATT_KV_HEADS, ATT_HEAD_DIM),
            nv.reshape(1, nb, WINDOW, ATT_KV_HEADS, ATT_HEAD_DIM),
            new_state.reshape(1, nb, SSD_HEADS, SSD_HEAD_DIM, D_STATE),
            jnp.concatenate([sc[:, 1:], xbcs[:, None, :]], axis=1)[None])
```

```python
import functools
import math

import numpy as np
import jax
import jax.numpy as jnp
from jax import lax
from jax.experimental import pallas as pl
from jax.experimental.pallas import tpu as pltpu

F32 = jnp.float32
BF16 = jnp.bfloat16
EPS = 1e-6

D_MODEL = 1024
ATT_HEADS = 16
ATT_KV_HEADS = 4
ATT_HEAD_DIM = 64
WINDOW = 128
N_BUCKETS = 32
MAX_DISTANCE = 128
SSD_HEADS = 16
SSD_HEAD_DIM = 64
SSD_WIDTH = SSD_HEADS * SSD_HEAD_DIM
SSD_GROUPS = 2
D_STATE = 128
CONV_WIDTH = 4
CONV_DIM = SSD_WIDTH + 2 * SSD_GROUPS * D_STATE
CHUNK = 128
N_MEM = 256
MEM_HEADS = 4
MEM_HEAD_DIM = D_MODEL // MEM_HEADS
PEER_HEADS = 8
N_KEYS = 128
N_EXPERTS = N_KEYS * N_KEYS
PEER_TOPK = 16
PEER_HALF = 128

LANES = 128
HEAD_PAD = 128
GROUP_W = SSD_WIDTH // SSD_GROUPS
VMEM_LIMIT = 56 * 1024 * 1024


def _params(sem):
    return pltpu.CompilerParams(dimension_semantics=sem, vmem_limit_bytes=VMEM_LIMIT)


def _dot(a, b):
    return jnp.dot(a, b, preferred_element_type=F32)


def _dot_nt(a, b):
    return lax.dot_general(a, b, (((1,), (1,)), ((), ())), preferred_element_type=F32)


def _rms(x, w):
    var = jnp.mean(x * x, axis=-1, keepdims=True)
    return x * lax.rsqrt(var + EPS) * w


def _sigmoid(x):
    return 1.0 / (1.0 + jnp.exp(-x))


def _softplus(x):
    return jnp.maximum(x, 0.0) + jnp.log1p(jnp.exp(-jnp.abs(x)))


def _split3(v):
    v1 = v.astype(BF16)
    r1 = v - v1.astype(F32)
    v2 = r1.astype(BF16)
    r2 = r1 - v2.astype(F32)
    return v1, v2, r2.astype(BF16)


def _exact_dot(v, m01):
    v1, v2, v3 = _split3(v)
    return _dot(v1, m01) + _dot(v2, m01) + _dot(v3, m01)


def _exact_dot_left(m01, v):
    v1, v2, v3 = _split3(v)
    return _dot(m01, v1) + _dot(m01, v2) + _dot(m01, v3)


def _norm_matmul_kernel(x_ref, nw_ref, w_ref, *out_refs, splits, emit_ht):
    hf = _rms(x_ref[...], nw_ref[...])
    h = hf.astype(BF16)
    refs = list(out_refs)
    if emit_ht:
        refs.pop(0)[...] = pltpu.bitcast(hf.T.astype(BF16), jnp.uint32)
    off = 0
    for o_ref, n in zip(refs, splits):
        o_ref[...] = _dot(h, w_ref[:, off:off + n])
        off += n


def norm_matmul(x, nw, w_bf16, splits, tm, emit_ht=False):
    t, d = x.shape
    n = w_bf16.shape[1]
    assert sum(splits) == n and t % tm == 0
    out_shape = [jax.ShapeDtypeStruct((t, s), F32) for s in splits]
    out_specs = [pl.BlockSpec((tm, s), lambda i: (i, 0)) for s in splits]
    if emit_ht:
        out_shape.insert(0, jax.ShapeDtypeStruct((d // 2, t), jnp.uint32))
        out_specs.insert(0, pl.BlockSpec((d // 2, tm), lambda i: (0, i)))
    return pl.pallas_call(
        functools.partial(_norm_matmul_kernel, splits=tuple(splits), emit_ht=emit_ht),
        grid=(t // tm,),
        in_specs=[pl.BlockSpec((tm, d), lambda i: (i, 0)),
                  pl.BlockSpec((1, d), lambda i: (0, 0)),
                  pl.BlockSpec((d, n), lambda i: (0, 0))],
        out_specs=out_specs,
        out_shape=out_shape,
        compiler_params=_params(("parallel",)),
        name="norm_matmul",
    )(x, nw.reshape(1, d), w_bf16)


def _bias_kernel(table_ref, bucket_ref, out_ref):
    h = pl.program_id(0)
    bk = bucket_ref[...]
    acc = jnp.zeros(bk.shape, F32)
    for b in range(N_BUCKETS):
        acc = jnp.where(bk == b, table_ref[b, h], acc)
    out_ref[0] = acc


def rel_bias_rows(table, bucket):
    r, c = bucket.shape
    return pl.pallas_call(
        _bias_kernel,
        grid=(ATT_HEADS,),
        in_specs=[pl.BlockSpec(memory_space=pltpu.SMEM),
                  pl.BlockSpec((r, c), lambda h: (0, 0))],
        out_specs=pl.BlockSpec((1, r, c), lambda h: (h, 0, 0)),
        out_shape=jax.ShapeDtypeStruct((ATT_HEADS, r, c), F32),
        compiler_params=_params(("parallel",)),
        name="rel_bias",
    )(table, bucket)


def _t5_bucket(dist):
    n = jnp.maximum(dist, 0)
    max_exact = N_BUCKETS // 2
    nf = jnp.maximum(n, 1).astype(F32)
    large = max_exact + (jnp.log(nf / max_exact) / math.log(MAX_DISTANCE / max_exact)
                         * (N_BUCKETS - max_exact)).astype(jnp.int32)
    large = jnp.minimum(large, N_BUCKETS - 1)
    return jnp.where(n < max_exact, n, large)


def _half_pair(pair, rolled, which, lo):
    if which == 0:
        return jnp.where(lo, pair, 0.0), jnp.where(lo, 0.0, rolled)
    return jnp.where(lo, rolled, 0.0), jnp.where(lo, 0.0, pair)


def _swa_prompt_kernel(sink_ref, q_ref, kc_ref, kp_ref, vc_ref, vp_ref, bias_ref, o_ref):
    i = pl.program_id(1)
    blk = WINDOW
    row = lax.broadcasted_iota(jnp.int32, (blk, 2 * blk), 0)
    col = lax.broadcasted_iota(jnp.int32, (blk, 2 * blk), 1)
    dist = row + blk - col
    ok1 = (dist >= 0) & (dist <= WINDOW) & ((col >= blk) | (i > 0))
    ok = jnp.concatenate([ok1, ok1], axis=0)
    first = lax.broadcasted_iota(jnp.int32, (2 * blk, 1), 0) < blk
    lo = lax.broadcasted_iota(jnp.int32, (2 * blk, LANES), 1) < ATT_HEAD_DIM
    scale = ATT_HEAD_DIM ** -0.5
    for pair in range(ATT_KV_HEADS // 2):
        sl = slice(pair * LANES, (pair + 1) * LANES)
        kpair = jnp.concatenate([kp_ref[:, sl], kc_ref[:, sl]], axis=0)
        vpair = jnp.concatenate([vp_ref[:, sl], vc_ref[:, sl]], axis=0)
        kroll = pltpu.roll(kpair, ATT_HEAD_DIM, axis=1)
        vroll = pltpu.roll(vpair, ATT_HEAD_DIM, axis=1)
        for gi in range(2):
            g = 2 * pair + gi
            k_lo, k_hi = _half_pair(kpair, kroll, gi, lo)
            v_lo, v_hi = _half_pair(vpair, vroll, gi, lo)
            k_halves = (k_lo.astype(BF16), k_hi.astype(BF16))
            v_halves = (v_lo.astype(BF16), v_hi.astype(BF16))
            qsl = slice(2 * g * LANES, (2 * g + 2) * LANES)
            q2 = jnp.concatenate([q_ref[:, 2 * g * LANES:(2 * g + 1) * LANES],
                                  q_ref[:, (2 * g + 1) * LANES:(2 * g + 2) * LANES]],
                                 axis=0).astype(BF16)
            acc = jnp.zeros((2 * blk, LANES), F32)
            for par in range(2):
                h0, h1 = 4 * g + par, 4 * g + 2 + par
                bias = jnp.concatenate([bias_ref[h0], bias_ref[h1]], axis=0)
                s = jnp.where(ok, _dot_nt(q2, k_halves[par]) * scale + bias, -jnp.inf)
                sink = jnp.where(first, sink_ref[h0], sink_ref[h1])
                m = jnp.maximum(jnp.max(s, axis=-1, keepdims=True), sink)
                p = jnp.exp(s - m)
                den = jnp.sum(p, axis=-1, keepdims=True) + jnp.exp(sink - m)
                acc = acc + _dot((p / den).astype(BF16), v_halves[par])
            o_ref[:, qsl] = jnp.concatenate([acc[:blk], acc[blk:]], axis=1)


def swa_prompt(q, k, v, bias, sinks, batch, seq):
    nb = seq // WINDOW
    cur = lambda b, i: (b * nb + i, 0)
    prev = lambda b, i: (b * nb + jnp.maximum(i - 1, 0), 0)
    kvw = ATT_KV_HEADS * ATT_HEAD_DIM
    return pl.pallas_call(
        _swa_prompt_kernel,
        grid=(batch, nb),
        in_specs=[pl.BlockSpec(memory_space=pltpu.SMEM),
                  pl.BlockSpec((WINDOW, D_MODEL), cur),
                  pl.BlockSpec((WINDOW, kvw), cur),
                  pl.BlockSpec((WINDOW, kvw), prev),
                  pl.BlockSpec((WINDOW, kvw), cur),
                  pl.BlockSpec((WINDOW, kvw), prev),
                  pl.BlockSpec((ATT_HEADS, WINDOW, 2 * WINDOW), lambda b, i: (0, 0, 0))],
        out_specs=pl.BlockSpec((WINDOW, D_MODEL), cur),
        out_shape=jax.ShapeDtypeStruct((batch * seq, D_MODEL), F32),
        compiler_params=_params(("parallel", "parallel")),
        name="swa_prompt",
    )(sinks, q, k, k, v, v, bias)


def _ssd_prompt_kernel(xbc_ref, dt_ref, cw_ref, cb_ref, dtb_ref, alog_ref, dskip_ref, e_ref,
                       y_ref, st_ref, state_scr, tail_scr):
    c = pl.program_id(1)
    last = pl.num_programs(1) - 1

    @pl.when(c == 0)
    def _():
        state_scr[...] = jnp.zeros(state_scr.shape, F32)
        tail_scr[...] = jnp.zeros(tail_scr.shape, F32)

    x = xbc_ref[...]
    tail = tail_scr[...]
    row8 = lax.broadcasted_iota(jnp.int32, tail.shape, 0)
    acc = x * cw_ref[CONV_WIDTH - 1:CONV_WIDTH, :]
    for s in range(1, CONV_WIDTH):
        xr = pltpu.roll(x, s, axis=0)
        tr = pltpu.roll(tail, s, axis=0)
        head = jnp.where(row8 < s, tr, xr[0:8])
        shifted = jnp.concatenate([head, xr[8:]], axis=0)
        acc = acc + shifted * cw_ref[CONV_WIDTH - 1 - s:CONV_WIDTH - s, :]
    tail_scr[...] = x[CHUNK - 8:CHUNK]
    u = acc + cb_ref[...]
    u = u * _sigmoid(u)
    xs = u[:, :SSD_WIDTH]
    bm = u[:, SSD_WIDTH:SSD_WIDTH + SSD_GROUPS * D_STATE]
    cm = u[:, SSD_WIDTH + SSD_GROUPS * D_STATE:]

    dtv = _softplus(dt_ref[...] + dtb_ref[...])
    a = dtv * (-jnp.exp(alog_ref[...]))
    r = lax.broadcasted_iota(jnp.int32, (CHUNK, CHUNK), 0)
    cc = lax.broadcasted_iota(jnp.int32, (CHUNK, CHUNK), 1)
    causal = r >= cc
    tril = jnp.where(causal, 1.0, 0.0).astype(BF16)
    cs = _exact_dot_left(tril, a)
    cs_t = cs.T
    dte = jnp.exp(cs[CHUNK - 1:CHUNK, :] - cs)
    ecs = jnp.exp(cs)
    e01 = e_ref[...]
    dt_x = _exact_dot(dtv, e01)
    dte_x = _exact_dot(dte, e01)
    ecs_x = _exact_dot(ecs, e01)
    xdt = xs * dt_x
    xdec = (xdt * dte_x).astype(BF16)
    xdt_b = xdt.astype(BF16)
    lo = lax.broadcasted_iota(jnp.int32, (CHUNK, LANES), 1) < SSD_HEAD_DIM
    dsk = dskip_ref[...]
    for g in range(SSD_GROUPS):
        gs = slice(g * GROUP_W, (g + 1) * GROUP_W)
        bg = bm[:, g * D_STATE:(g + 1) * D_STATE]
        cg = cm[:, g * D_STATE:(g + 1) * D_STATE].astype(BF16)
        cb = _dot_nt(cg, bg.astype(BF16))
        bgt = bg.T.astype(BF16)
        prev = state_scr[g]
        y_off = _dot(cg, prev.astype(BF16)) * ecs_x[:, gs]
        state_scr[g] = ecs_x[CHUNK - 1:CHUNK, gs] * prev + _dot(bgt, xdec[:, gs])
        for m in range(GROUP_W // LANES):
            ps = slice(g * GROUP_W + m * LANES, g * GROUP_W + (m + 1) * LANES)
            xp = xdt_b[:, ps]
            ydiag = jnp.zeros((CHUNK, LANES), F32)
            for par in range(2):
                j = g * (SSD_HEADS // SSD_GROUPS) + 2 * m + par
                diff = cs[:, j:j + 1] - cs_t[j:j + 1, :]
                ldec = jnp.exp(jnp.where(causal, diff, -jnp.inf))
                mm = (cb * ldec).astype(BF16)
                xpm = jnp.where(lo, xp, 0.0) if par == 0 else jnp.where(lo, 0.0, xp)
                ydiag = ydiag + _dot(mm, xpm.astype(BF16))
            y_ref[:, ps] = ydiag + y_off[:, m * LANES:(m + 1) * LANES] + dsk[:, ps] * xs[:, ps]

    @pl.when(c == last)
    def _():
        for g in range(SSD_GROUPS):
            st_ref[0, g * GROUP_W:(g + 1) * GROUP_W, :] = state_scr[g].T


def ssd_prompt(xbc, dt_raw, cw, cb, dtb, alog, dskip_x, e01, batch, seq):
    nc = seq // CHUNK
    cur = lambda b, c: (b * nc + c, 0)
    const = lambda b, c: (0, 0)
    return pl.pallas_call(
        _ssd_prompt_kernel,
        grid=(batch, nc),
        in_specs=[pl.BlockSpec((CHUNK, CONV_DIM), cur),
                  pl.BlockSpec((CHUNK, HEAD_PAD), cur),
                  pl.BlockSpec((CONV_WIDTH, CONV_DIM), const),
                  pl.BlockSpec((1, CONV_DIM), const),
                  pl.BlockSpec((1, HEAD_PAD), const),
                  pl.BlockSpec((1, HEAD_PAD), const),
                  pl.BlockSpec((1, SSD_WIDTH), const),
                  pl.BlockSpec((HEAD_PAD, SSD_WIDTH), const)],
        out_specs=[pl.BlockSpec((CHUNK, SSD_WIDTH), cur),
                   pl.BlockSpec((1, SSD_WIDTH, D_STATE), lambda b, c: (b, 0, 0))],
        out_shape=[jax.ShapeDtypeStruct((batch * seq, SSD_WIDTH), F32),
                   jax.ShapeDtypeStruct((batch, SSD_WIDTH, D_STATE), F32)],
        scratch_shapes=[pltpu.VMEM((SSD_GROUPS, D_STATE, GROUP_W), F32),
                        pltpu.VMEM((8, CONV_DIM), F32)],
        compiler_params=_params(("parallel", "arbitrary")),
        name="ssd_prompt",
    )(xbc, dt_raw, cw, cb, dtb, alog, dskip_x, e01)


def _swa_sample_kernel(qx_ref, ck_ref, cv_ref, kn_ref, vn_ref, bias_ref, sink_ref,
                       nk_ref, nv_ref, o_ref, *, bt):
    scale = ATT_HEAD_DIM ** -0.5
    kvw = ATT_KV_HEADS * ATT_HEAD_DIM
    rowi = lax.broadcasted_iota(jnp.int32, (WINDOW, kvw), 0)
    bias_c = bias_ref[:, 0:WINDOW]
    bias_n = bias_ref[:, WINDOW:WINDOW + 1]
    sink = sink_ref[:, 0:1]
    for bb in range(bt):
        kc = ck_ref[bb]
        vc = cv_ref[bb]
        kn = kn_ref[bb]
        vn = vn_ref[bb]
        qx = qx_ref[bb]
        s_c = _dot_nt(qx.astype(BF16), kc.astype(BF16)) * scale + bias_c
        s_n = jnp.sum(qx * kn, axis=1, keepdims=True) * scale + bias_n
        m = jnp.maximum(jnp.maximum(jnp.max(s_c, axis=1, keepdims=True), s_n), sink)
        p_c = jnp.exp(s_c - m)
        p_n = jnp.exp(s_n - m)
        den = jnp.sum(p_c, axis=1, keepdims=True) + p_n + jnp.exp(sink - m)
        o_ref[bb] = _dot((p_c / den).astype(BF16), vc.astype(BF16)) + (p_n / den) * vn
        nk_ref[bb] = jnp.where(rowi == WINDOW - 1, kn, pltpu.roll(kc, WINDOW - 1, axis=0))
        nv_ref[bb] = jnp.where(rowi == WINDOW - 1, vn, pltpu.roll(vc, WINDOW - 1, axis=0))


def swa_sample(qx, ck, cv, kn, vn, bias_s, sink_b, bt=8):
    nb = qx.shape[0]
    kvw = ATT_KV_HEADS * ATT_HEAD_DIM
    blk3 = lambda i: (i, 0, 0)
    return pl.pallas_call(
        functools.partial(_swa_sample_kernel, bt=bt),
        grid=(nb // bt,),
        in_specs=[pl.BlockSpec((bt, ATT_HEADS, kvw), blk3),
                  pl.BlockSpec((bt, WINDOW, kvw), blk3),
                  pl.BlockSpec((bt, WINDOW, kvw), blk3),
                  pl.BlockSpec((bt, 1, kvw), blk3),
                  pl.BlockSpec((bt, 1, kvw), blk3),
                  pl.BlockSpec((ATT_HEADS, 2 * WINDOW), lambda i: (0, 0)),
                  pl.BlockSpec((ATT_HEADS, LANES), lambda i: (0, 0))],
        out_specs=[pl.BlockSpec((bt, WINDOW, kvw), blk3),
                   pl.BlockSpec((bt, WINDOW, kvw), blk3),
                   pl.BlockSpec((bt, ATT_HEADS, kvw), blk3)],
        out_shape=[jax.ShapeDtypeStruct((nb, WINDOW, kvw), F32),
                   jax.ShapeDtypeStruct((nb, WINDOW, kvw), F32),
                   jax.ShapeDtypeStruct((nb, ATT_HEADS, kvw), F32)],
        compiler_params=_params(("parallel",)),
        name="swa_sample",
    )(qx, ck, cv, kn, vn, bias_s, sink_b)


def _ssd_sample_prep_kernel(xbc_ref, c0_ref, c1_ref, c2_ref, dt_ref, cw_ref, cb_ref, dtb_ref,
                            alog_ref, e_ref, dec_t_ref, xdt_t_ref, b_ref, c_ref, xs_ref):
    u = (c0_ref[...] * cw_ref[0:1, :] + c1_ref[...] * cw_ref[1:2, :]
         + c2_ref[...] * cw_ref[2:3, :] + xbc_ref[...] * cw_ref[3:4, :]) + cb_ref[...]
    u = u * _sigmoid(u)
    xs = u[:, :SSD_WIDTH]
    b_ref[...] = u[:, SSD_WIDTH:SSD_WIDTH + SSD_GROUPS * D_STATE]
    c_ref[...] = u[:, SSD_WIDTH + SSD_GROUPS * D_STATE:]
    xs_ref[...] = xs
    dtv = _softplus(dt_ref[...] + dtb_ref[...])
    dec = jnp.exp(dtv * (-jnp.exp(alog_ref[...])))
    e01 = e_ref[...]
    dec_t_ref[...] = _exact_dot(dec, e01).T
    xdt_t_ref[...] = (xs * _exact_dot(dtv, e01)).T


def ssd_sample_prep(xbc, c0, c1, c2, dt_raw, cw, cb, dtb, alog, e01):
    nb = xbc.shape[0]
    args = (xbc, c0, c1, c2, dt_raw, cw, cb, dtb, alog, e01)
    full = lambda a: pl.BlockSpec(a.shape, lambda i: (0,) * a.ndim)
    out_shape = [jax.ShapeDtypeStruct((SSD_WIDTH, nb), F32),
                 jax.ShapeDtypeStruct((SSD_WIDTH, nb), F32),
                 jax.ShapeDtypeStruct((nb, SSD_GROUPS * D_STATE), F32),
                 jax.ShapeDtypeStruct((nb, SSD_GROUPS * D_STATE), F32),
                 jax.ShapeDtypeStruct((nb, SSD_WIDTH), F32)]
    return pl.pallas_call(
        _ssd_sample_prep_kernel,
        grid=(1,),
        in_specs=[full(a) for a in args],
        out_specs=[full(s) for s in out_shape],
        out_shape=out_shape,
        compiler_params=_params(("arbitrary",)),
        name="ssd_sample_prep",
    )(*args)


def _ssd_sample_step_kernel(st_ref, dec_t_ref, xdt_t_ref, b_ref, c_ref, xs_ref, dskip_ref,
                            ns_ref, y_ref, yt_scr, *, bt):
    i = pl.program_id(0)
    nb = dec_t_ref.shape[1]

    @pl.when(i == 0)
    def _():
        yt_scr[...] = jnp.zeros(yt_scr.shape, F32)

    dparts = _split3(dec_t_ref[...])
    xparts = _split3(xdt_t_ref[...])
    lane = lax.broadcasted_iota(jnp.int32, (GROUP_W, nb), 1)
    for bb in range(bt):
        b = i * bt + bb
        pick = jnp.where(lax.broadcasted_iota(jnp.int32, (nb, D_STATE), 0) == b,
                         1.0, 0.0).astype(BF16)
        dcol = sum(_dot(p, pick) for p in dparts)
        xcol = sum(_dot(p, pick) for p in xparts)
        brow = b_ref[pl.ds(b, 1), :]
        crow = c_ref[pl.ds(b, 1), :]
        for g in range(SSD_GROUPS):
            gs = slice(g * GROUP_W, (g + 1) * GROUP_W)
            ns = slice(g * D_STATE, (g + 1) * D_STATE)
            hn = dcol[gs] * st_ref[bb, gs, :] + xcol[gs] * brow[:, ns]
            ns_ref[bb, gs, :] = hn
            ycol = jnp.sum(hn * crow[:, ns], axis=1, keepdims=True)
            yt_scr[gs, :] = jnp.where(lane == b, ycol, yt_scr[gs, :])

    @pl.when(i == pl.num_programs(0) - 1)
    def _():
        y_ref[...] = yt_scr[...].T + dskip_ref[...] * xs_ref[...]


def ssd_sample_step(state, dec_t, xdt_t, bmat, cmat, xs, dskip_x, bt=8):
    nb = state.shape[0]
    assert nb == D_STATE
    const2 = lambda i: (0, 0)
    return pl.pallas_call(
        functools.partial(_ssd_sample_step_kernel, bt=bt),
        grid=(nb // bt,),
        in_specs=[pl.BlockSpec((bt, SSD_WIDTH, D_STATE), lambda i: (i, 0, 0)),
                  pl.BlockSpec((SSD_WIDTH, nb), const2),
                  pl.BlockSpec((SSD_WIDTH, nb), const2),
                  pl.BlockSpec((nb, SSD_GROUPS * D_STATE), const2),
                  pl.BlockSpec((nb, SSD_GROUPS * D_STATE), const2),
                  pl.BlockSpec((nb, SSD_WIDTH), const2),
                  pl.BlockSpec((1, SSD_WIDTH), const2)],
        out_specs=[pl.BlockSpec((bt, SSD_WIDTH, D_STATE), lambda i: (i, 0, 0)),
                   pl.BlockSpec((nb, SSD_WIDTH), const2)],
        out_shape=[jax.ShapeDtypeStruct((nb, SSD_WIDTH, D_STATE), F32),
                   jax.ShapeDtypeStruct((nb, SSD_WIDTH), F32)],
        scratch_shapes=[pltpu.VMEM((SSD_WIDTH, nb), F32)],
        compiler_params=_params(("arbitrary",)),
        name="ssd_sample_step",
    )(state, dec_t, xdt_t, bmat, cmat, xs, dskip_x)


def _merge_kernel(x_ref, o_ref, y_ref, z_ref, anw_ref, snw_ref, w1_ref, w2_ref, out_ref):
    ya = _rms(o_ref[...], anw_ref[...]).astype(BF16)
    z = z_ref[...]
    ys = _rms(y_ref[...] * (z * _sigmoid(z)), snw_ref[...]).astype(BF16)
    out_ref[...] = x_ref[...] + _dot(ya, w1_ref[...]) + _dot(ys, w2_ref[...])


def merge(x, o_att, y_ssd, z, anw, snw, w1, w2, tm):
    t, d = x.shape
    row = pl.BlockSpec((tm, d), lambda i: (i, 0))
    vec = pl.BlockSpec((1, d), lambda i: (0, 0))
    mat = pl.BlockSpec((d, d), lambda i: (0, 0))
    return pl.pallas_call(
        _merge_kernel,
        grid=(t // tm,),
        in_specs=[row, row, row, row, vec, vec, mat, mat],
        out_specs=row,
        out_shape=jax.ShapeDtypeStruct((t, d), F32),
        compiler_params=_params(("parallel",)),
        name="merge",
    )(x, o_att, y_ssd, z, anw.reshape(1, d), snw.reshape(1, d), w1, w2)


def _mem_prompt_kernel(x_ref, nw_ref, wq_ref, mk_ref, mv_ref, wo_ref, out_ref):
    x = x_ref[...]
    q = _dot(_rms(x, nw_ref[...]).astype(BF16), wq_ref[...])
    scale = MEM_HEAD_DIM ** -0.5
    outs = []
    for h in range(MEM_HEADS):
        hs = slice(h * MEM_HEAD_DIM, (h + 1) * MEM_HEAD_DIM)
        s = _dot_nt(q[:, hs].astype(BF16), mk_ref[:, hs]) * scale
        p = jnp.exp(s - jnp.max(s, axis=-1, keepdims=True))
        p = p / jnp.sum(p, axis=-1, keepdims=True)
        outs.append(_dot(p.astype(BF16), mv_ref[:, hs]).astype(BF16))
    out_ref[...] = x + _dot(jnp.concatenate(outs, axis=1), wo_ref[...])


def mem_attn_prompt(x, nw, wq, mk, mv, wo, seq, tm):
    t, d = x.shape
    per = seq // tm
    row = pl.BlockSpec((tm, d), lambda i: (i, 0))
    mat = pl.BlockSpec((d, d), lambda i: (0, 0))
    mem = pl.BlockSpec((N_MEM, d), lambda i: (i // per, 0))
    return pl.pallas_call(
        _mem_prompt_kernel,
        grid=(t // tm,),
        in_specs=[row, pl.BlockSpec((1, d), lambda i: (0, 0)), mat, mem, mem, mat],
        out_specs=row,
        out_shape=jax.ShapeDtypeStruct((t, d), F32),
        compiler_params=_params(("parallel",)),
        name="mem_prompt",
    )(x, nw.reshape(1, d), wq, mk, mv, wo)


def _mem_sample_kernel(q_ref, k_ref, v_ref, o_ref, *, bt):
    rows = N_MEM * MEM_HEADS
    qh = lax.broadcasted_iota(jnp.int32, (8, rows), 0)
    kh = lax.broadcasted_iota(jnp.int32, (8, rows), 1) % MEM_HEADS
    live = qh < MEM_HEADS
    for bb in range(bt):
        q4 = q_ref[bb]
        q8 = jnp.concatenate([q4, jnp.zeros_like(q4)], axis=0).astype(BF16)
        k2 = k_ref[bb].reshape(rows, MEM_HEAD_DIM).astype(BF16)
        v2 = v_ref[bb].reshape(rows, MEM_HEAD_DIM).astype(BF16)
        s = _dot_nt(q8, k2) * (MEM_HEAD_DIM ** -0.5)
        s = jnp.where(kh == qh, s, -jnp.inf)
        p = jnp.exp(s - jnp.where(live, jnp.max(s, axis=1, keepdims=True), 0.0))
        p = p / jnp.where(live, jnp.sum(p, axis=1, keepdims=True), 1.0)
        o_ref[bb] = _dot(p.astype(BF16), v2)[0:MEM_HEADS]


def mem_sample(q, ck, cv, bt=4):
    nb, d = q.shape
    blk = pl.BlockSpec((bt, N_MEM, MEM_HEADS, MEM_HEAD_DIM), lambda i: (i, 0, 0, 0))
    vec = pl.BlockSpec((bt, MEM_HEADS, MEM_HEAD_DIM), lambda i: (i, 0, 0))
    return pl.pallas_call(
        functools.partial(_mem_sample_kernel, bt=bt),
        grid=(nb // bt,),
        in_specs=[vec, blk, blk],
        out_specs=vec,
        out_shape=jax.ShapeDtypeStruct((nb, MEM_HEADS, MEM_HEAD_DIM), F32),
        compiler_params=_params(("parallel",)),
        name="mem_sample",
    )(q.reshape(nb, MEM_HEADS, MEM_HEAD_DIM), ck, cv).reshape(nb, d)


def _matmul_res_kernel(a_ref, w_ref, r_ref, out_ref):
    out_ref[...] = r_ref[...] + _dot(a_ref[...].astype(BF16), w_ref[...])


def matmul_res(a, w, res, tm):
    t, d = res.shape
    kdim = a.shape[1]
    return pl.pallas_call(
        _matmul_res_kernel,
        grid=(t // tm,),
        in_specs=[pl.BlockSpec((tm, kdim), lambda i: (i, 0)),
                  pl.BlockSpec((kdim, d), lambda i: (0, 0)),
                  pl.BlockSpec((tm, d), lambda i: (i, 0))],
        out_specs=pl.BlockSpec((tm, d), lambda i: (i, 0)),
        out_shape=jax.ShapeDtypeStruct((t, d), F32),
        compiler_params=_params(("parallel",)),
        name="matmul_res",
    )(a, w, res)


def _topk_rows(work, k):
    rows = work.shape[0]
    iota = lax.broadcasted_iota(jnp.int32, work.shape, 0)
    rank = jnp.full(work.shape, k, jnp.int32)
    vals = []
    for r in range(k):
        m = jnp.max(work, axis=0, keepdims=True)
        idx = jnp.min(jnp.where(work == m, iota, rows), axis=0, keepdims=True)
        hit = iota == idx
        rank = jnp.where(hit, r, rank)
        work = jnp.where(hit, -jnp.inf, work)
        vals.append(m)
    return jnp.concatenate(vals, axis=0), rank


def _topk_rows_distinct(work, k, want_rank):
    rank = jnp.full(work.shape, k, jnp.int32) if want_rank else None
    vals = []
    for r in range(k):
        m = jnp.max(work, axis=0, keepdims=True)
        hit = work == m
        if want_rank:
            rank = jnp.where(hit, r, rank)
        work = jnp.where(hit, -jnp.inf, work)
        vals.append(m)
    taken = work == -jnp.inf
    count = jnp.sum(jnp.where(taken, 1.0, 0.0), axis=0, keepdims=True)
    return jnp.concatenate(vals, axis=0), rank, taken, count


def _peer_candidates(a, b):
    lo4 = lax.broadcasted_iota(jnp.int32, (8, a.shape[1]), 0) < 4
    b8 = b[0:8]
    b44 = jnp.where(lo4, b8, pltpu.roll(b8, 4, axis=0))
    return jnp.concatenate(
        [a[0:1] + b8, a[0:1] + b[8:16], a[1:2] + b8, a[2:3] + b8, a[3:4] + b8,
         jnp.where(lo4, a[4:5], a[5:6]) + b44, jnp.where(lo4, a[6:7], a[7:8]) + b44,
         a[8:16] + b[0:1]], axis=0)


def _peer_emit(subs, a, b, rank0, rank1, cand, chosen, seli_ref, selj_ref):
    k = PEER_TOPK
    cmax = a[0:1, :] + b[0:1, :]
    z = jnp.sum(jnp.where(chosen, jnp.exp(cand - cmax), 0.0), axis=0, keepdims=True)
    lo4 = lax.broadcasted_iota(jnp.int32, (8, a.shape[1]), 0) < 4
    ch = [jnp.where(chosen[8 * i:8 * i + 8], 1.0, 0.0) for i in range(8)]
    cnt = lambda v: jnp.sum(v, axis=0, keepdims=True)
    n = [cnt(ch[0]) + cnt(ch[1]), cnt(ch[2]), cnt(ch[3]), cnt(ch[4]),
         cnt(jnp.where(lo4, ch[5], 0.0)), cnt(jnp.where(lo4, 0.0, ch[5])),
         cnt(jnp.where(lo4, ch[6], 0.0)), cnt(jnp.where(lo4, 0.0, ch[6]))]
    n += [ch[7][i:i + 1] for i in range(8)]
    lim0 = jnp.zeros(subs[0].shape, F32)
    if rank0 is None:
        for r in reversed(range(k)):
            lim0 = jnp.where(subs[0] >= a[r:r + 1, :], n[r], lim0)
    else:
        for r in range(k):
            lim0 = jnp.where(rank0 == r, n[r], lim0)

    def twice(v):
        bits = pltpu.bitcast(v.astype(BF16).astype(F32), jnp.uint32)
        return bits | (bits >> 16)

    seli_ref[0, 0] = twice(0.5 * jnp.exp(subs[0] - a[0:1, :]) / z)
    seli_ref[0, 1] = twice(lim0)
    selj_ref[0, 0] = pltpu.bitcast(jnp.exp(subs[1] - b[0:1, :]).astype(BF16), jnp.uint32)
    selj_ref[0, 1] = pltpu.bitcast(rank1.astype(F32).astype(BF16), jnp.uint32)


def _peer_select_kernel(pq_ref, keys_ref, seli_ref, selj_ref):
    k = PEER_TOPK
    subs = [_dot_nt(keys_ref[p], pq_ref[:, p * PEER_HALF:(p + 1) * PEER_HALF].astype(BF16))
            for p in range(2)]
    a, _, _, took0 = _topk_rows_distinct(subs[0], k, want_rank=False)
    b, rank1, _, took1 = _topk_rows_distinct(subs[1], k, want_rank=True)
    cand = _peer_candidates(a, b)
    _, _, chosen, tookc = _topk_rows_distinct(cand, k, want_rank=False)
    _peer_emit(subs, a, b, None, rank1, cand, chosen, seli_ref, selj_ref)
    tied = jnp.where((took0 != k) | (took1 != k) | (tookc != k), 1.0, 0.0)

    @pl.when(jnp.max(tied) > 0.0)
    def _():
        a, rank0 = _topk_rows(subs[0], k)
        b, rank1 = _topk_rows(subs[1], k)
        cand = _peer_candidates(a, b)
        _, crank = _topk_rows(cand, k)
        _peer_emit(subs, a, b, rank0, rank1, cand, crank < k, seli_ref, selj_ref)


def peer_select(pq, keys_bf16, tm):
    t = pq.shape[0]
    spec = lambda rows: pl.BlockSpec((1, 2, rows, tm), lambda i, h: (h, 0, 0, i))
    return pl.pallas_call(
        _peer_select_kernel,
        grid=(t // tm, PEER_HEADS),
        in_specs=[pl.BlockSpec((tm, 2 * PEER_HALF), lambda i, h: (i, h)),
                  pl.BlockSpec((2, N_KEYS, PEER_HALF), lambda i, h: (h, 0, 0))],
        out_specs=[spec(N_KEYS), spec(N_KEYS // 2)],
        out_shape=[jax.ShapeDtypeStruct((PEER_HEADS, 2, N_KEYS, t), jnp.uint32),
                   jax.ShapeDtypeStruct((PEER_HEADS, 2, N_KEYS // 2, t), jnp.uint32)],
        compiler_params=_params(("parallel", "parallel")),
        name="peer_select",
    )(pq, keys_bf16)


def _gelu_x2(x):
    return x * (1.0 + lax.erf(x * np.float32(math.sqrt(0.5))))


def _peer_dense_kernel(xnt_ref, x_ref, u_ref, vt_ref, seli_ref, selj_ref, fnw_ref, y_ref,
                       acc_scr, act_scr, w_scr, *, tm, ib, sub):
    e = pl.program_id(1)
    ne = pl.num_programs(1)

    @pl.when(e == 0)
    def _():
        acc_scr[...] = jnp.zeros(acc_scr.shape, F32)

    i8 = pl.ds(pl.multiple_of(e * ib, 8), ib)
    nsub = ib // sub
    d = acc_scr.shape[0]
    halves = 2
    sub_rows = lambda k: slice(k * sub * N_KEYS, (k + 1) * sub * N_KEYS)

    def activations(k, half):
        cs = slice(half * tm // halves, (half + 1) * tm // halves)
        u = pltpu.bitcast(u_ref[k * sub * N_KEYS // 2:(k + 1) * sub * N_KEYS // 2, :], BF16)
        act_scr[k % act_scr.shape[0], :, cs] = _dot(u, pltpu.bitcast(xnt_ref[:, cs], BF16))

    def accumulate(k, half):
        rs = slice(half * d // halves, (half + 1) * d // halves)
        ws = slice(half * d // (2 * halves), (half + 1) * d // (2 * halves))
        acc_scr[rs, :] += _dot(pltpu.bitcast(vt_ref[0, ws, sub_rows(k)], BF16),
                               w_scr[k % w_scr.shape[0]])

    def gates(k, lt):
        ls = slice(lt * LANES, (lt + 1) * LANES)
        for j0 in range(0, sub, 2):
            gs = [jnp.zeros((N_KEYS, LANES), BF16) for _ in range(2)]
            for h in range(PEER_HEADS):
                e1 = pltpu.bitcast(selj_ref[h, 0, :, ls], BF16)
                rank1 = pltpu.bitcast(selj_ref[h, 1, :, ls], BF16)
                for t in range(2):
                    ii = k * sub + j0 + t
                    row = lambda c: pltpu.bitcast(jnp.broadcast_to(
                        seli_ref[h, c, i8, ls][ii:ii + 1, :], (N_KEYS // 2, LANES)), BF16)
                    gs[t] = gs[t] + jnp.where(rank1 < row(1), e1, jnp.zeros_like(e1)) * row(0)
            for t in range(2):
                rs = slice((j0 + t) * N_KEYS, (j0 + t + 1) * N_KEYS)
                act = act_scr[k % act_scr.shape[0], rs, ls]
                w_scr[k % w_scr.shape[0], rs, ls] = gs[t] * _gelu_x2(act).astype(BF16)

    for half in range(halves):
        activations(0, half)
    for k in range(nsub):
        mxu_jobs = []
        if k + 1 < nsub:
            mxu_jobs += [functools.partial(activations, k + 1, half) for half in range(halves)]
        if k >= 1:
            mxu_jobs += [functools.partial(accumulate, k - 1, half) for half in range(halves)]
        for job in mxu_jobs:
            job()
        for lt in range(tm // LANES):
            gates(k, lt)
    for half in range(halves):
        accumulate(nsub - 1, half)

    @pl.when(e == ne - 1)
    def _():
        y_ref[...] = _rms(x_ref[...] + acc_scr[...].T, fnw_ref[...])


PEER_BLOCK_ROWS = 16
PEER_SUB_ROWS = 4


def _pack_peer_kernel(u_ref, v_ref, up_ref, vtp_ref):
    up_ref[...] = pltpu.bitcast(u_ref[...].astype(BF16), jnp.uint32)
    vtp_ref[0] = pltpu.bitcast(v_ref[...].T.astype(BF16), jnp.uint32)


def pack_peer_weights(u, v, ib=PEER_BLOCK_ROWS):
    n, d = u.shape
    eb = ib * N_KEYS
    return pl.pallas_call(
        _pack_peer_kernel,
        grid=(n // eb,),
        in_specs=[pl.BlockSpec((eb, d), lambda e: (e, 0)), pl.BlockSpec((eb, d), lambda e: (e, 0))],
        out_specs=[pl.BlockSpec((eb // 2, d), lambda e: (e, 0)),
                   pl.BlockSpec((1, d // 2, eb), lambda e: (e, 0, 0))],
        out_shape=[jax.ShapeDtypeStruct((n // 2, d), jnp.uint32),
                   jax.ShapeDtypeStruct((n // eb, d // 2, eb), jnp.uint32)],
        compiler_params=_params(("parallel",)),
        name="pack_peer_weights",
    )(u, v)


def peer_dense(xnt, x, u_packed, vt_packed, seli, selj, fnw, tm, ib=PEER_BLOCK_ROWS,
               sub=PEER_SUB_ROWS):
    t, d = x.shape
    eb = ib * N_KEYS
    nblk = N_EXPERTS // eb
    assert vt_packed.shape == (nblk, d // 2, eb) and xnt.shape == (d // 2, t)
    sel_spec = lambda rows: pl.BlockSpec((PEER_HEADS, 2, rows, tm), lambda i, e: (0, 0, 0, i))
    return pl.pallas_call(
        functools.partial(_peer_dense_kernel, tm=tm, ib=ib, sub=sub),
        grid=(t // tm, nblk),
        in_specs=[pl.BlockSpec((d // 2, tm), lambda i, e: (0, i)),
                  pl.BlockSpec((tm, d), lambda i, e: (i, 0)),
                  pl.BlockSpec((eb // 2, d), lambda i, e: (e, 0)),
                  pl.BlockSpec((1, d // 2, eb), lambda i, e: (e, 0, 0)),
                  sel_spec(N_KEYS), sel_spec(N_KEYS // 2),
                  pl.BlockSpec((1, d), lambda i, e: (0, 0))],
        out_specs=pl.BlockSpec((tm, d), lambda i, e: (i, 0)),
        out_shape=jax.ShapeDtypeStruct((t, d), F32),
        scratch_shapes=[pltpu.VMEM((d, tm), F32),
                        pltpu.VMEM((min(2, ib // sub), sub * N_KEYS, tm), F32),
                        pltpu.VMEM((min(3, ib // sub), sub * N_KEYS, tm), BF16)],
        compiler_params=_params(("parallel", "arbitrary")),
        name="peer_dense",
    )(xnt, x, u_packed, vt_packed, seli, selj, fnw.reshape(1, d))


def _tail(x, mk, mv, prm, seq, tm, mem_is_shared, tm_peer):
    if mem_is_shared:
        x = mem_attn_prompt(x, prm["norm_mem_w"], prm["w_mq"], mk, mv, prm["w_mo"], seq, tm)
    else:
        (q,) = norm_matmul(x, prm["norm_mem_w"], prm["w_mq"], (D_MODEL,), tm)
        x = matmul_res(mem_sample(q, mk, mv), prm["w_mo"], x, tm)
    xnt, pq = norm_matmul(x, prm["norm_ffn_w"], prm["peer_wq"], (prm["peer_wq"].shape[1],), tm,
                          emit_ht=True)
    seli, selj = peer_select(pq, prm["peer_keys"], tm)
    return peer_dense(xnt, x, prm["peer_u"], prm["peer_vt"], seli, selj, prm["final_norm_w"],
                      tm_peer)


def kernel(x_prompt, x_sample, mem_prompt, cache_swa_k, cache_swa_v, state_ssm, state_conv, cache_mem_k, cache_mem_v, norm_mix_w, w_in, attn_sinks, rel_bias, attn_norm_w, conv_w, conv_b, dt_bias, a_log, d_skip, ssd_norm_w, w_out, norm_mem_w, mem_norm_w, w_mq, w_mk, w_mv, w_mo, norm_ffn_w, peer_wq, peer_keys, peer_u, peer_v, final_norm_w):
    assert w_in.shape[0] == 1, "single-layer step"
    batch, seq, d = x_prompt.shape
    nb = x_sample.shape[0]
    kvw = ATT_KV_HEADS * ATT_HEAD_DIM
    tm = 512

    n_main = D_MODEL + 2 * kvw + SSD_WIDTH + CONV_DIM
    w_in_p = jnp.concatenate(
        [w_in[0], jnp.zeros((d, HEAD_PAD - SSD_HEADS), F32)], axis=1).astype(BF16)
    in_splits = (D_MODEL, kvw, kvw, SSD_WIDTH, CONV_DIM, HEAD_PAD)
    assert n_main + SSD_HEADS == w_in.shape[2]
    pad_h = lambda v: jnp.pad(v.reshape(1, SSD_HEADS), ((0, 0), (0, HEAD_PAD - SSD_HEADS)))
    dtb, alog = pad_h(dt_bias[0]), pad_h(a_log[0])
    dskip_x = jnp.repeat(d_skip[0], SSD_HEAD_DIM).reshape(1, SSD_WIDTH)
    e01 = (jnp.arange(HEAD_PAD)[:, None] == jnp.arange(SSD_WIDTH)[None, :] // SSD_HEAD_DIM
           ).astype(BF16)
    cw, cb = conv_w[0], conv_b[0].reshape(1, CONV_DIM)
    w_out1 = w_out[0, :D_MODEL].astype(BF16)
    w_out2 = w_out[0, D_MODEL:].astype(BF16)
    peer_up, peer_vtp = pack_peer_weights(peer_u[0], peer_v[0])
    prm = dict(norm_mem_w=norm_mem_w[0], w_mq=w_mq[0].astype(BF16), w_mo=w_mo[0].astype(BF16),
               norm_ffn_w=norm_ffn_w[0], peer_wq=peer_wq[0].astype(BF16),
               peer_keys=peer_keys[0].reshape(2 * PEER_HEADS, N_KEYS, PEER_HALF).astype(BF16),
               peer_u=peer_up, peer_vt=peer_vtp, final_norm_w=final_norm_w)

    qi = jnp.arange(WINDOW)[:, None] + WINDOW
    bias_p = rel_bias_rows(rel_bias, _t5_bucket(qi - jnp.arange(2 * WINDOW)[None, :]))
    dist_s = jnp.broadcast_to(jnp.maximum(WINDOW - jnp.arange(2 * WINDOW), 0)[None, :],
                              (8, 2 * WINDOW))
    bias_s = rel_bias_rows(rel_bias, _t5_bucket(dist_s))[:, 0, :]
    sinks = attn_sinks[0]

    xp = x_prompt.reshape(batch * seq, d)
    q, k, v, z, xbc, dt_raw = norm_matmul(xp, norm_mix_w[0], w_in_p, in_splits, tm)
    o_att = swa_prompt(q, k, v, bias_p, sinks, batch, seq)
    y_ssd, p_state = ssd_prompt(xbc, dt_raw, cw, cb, dtb, alog, dskip_x, e01, batch, seq)
    x1 = merge(xp, o_att, y_ssd, z, attn_norm_w[0], ssd_norm_w[0], w_out1, w_out2, tm)
    mkv_w = jnp.concatenate([w_mk[0], w_mv[0]], axis=1).astype(BF16)
    mk, mv = norm_matmul(mem_prompt.reshape(batch * N_MEM, d), mem_norm_w[0], mkv_w, (d, d), tm)
    y_p = _tail(x1, mk.astype(BF16), mv.astype(BF16), prm, seq, tm, True, tm)

    xs_in = x_sample.reshape(nb, d)
    tms = nb
    qs, ks, vs, zs, xbcs, dts = norm_matmul(xs_in, norm_mix_w[0], w_in_p, in_splits, tms)
    own = (jnp.arange(kvw)[None, :] // ATT_HEAD_DIM) == (jnp.arange(ATT_HEADS)[:, None] // (ATT_HEADS // ATT_KV_HEADS))
    qx = jnp.where(own[None], jnp.tile(qs.reshape(nb, ATT_HEADS, ATT_HEAD_DIM), (1, 1, ATT_KV_HEADS)), 0.0)
    ck = cache_swa_k[0].reshape(nb, WINDOW, kvw)
    cv = cache_swa_v[0].reshape(nb, WINDOW, kvw)
    sink_b = jnp.broadcast_to(sinks[:, None], (ATT_HEADS, LANES))
    nk, nv, ox = swa_sample(qx, ck, cv, ks.reshape(nb, 1, kvw), vs.reshape(nb, 1, kvw), bias_s, sink_b)
    ox5 = ox.reshape(nb, ATT_KV_HEADS, ATT_HEADS // ATT_KV_HEADS, ATT_KV_HEADS, ATT_HEAD_DIM)
    o_att_s = jnp.stack([ox5[:, g, :, g, :] for g in range(ATT_KV_HEADS)], axis=1).reshape(nb, d)
    sc = state_conv[0]
    dec_t, xdt_t, bmat, cmat, xs_s = ssd_sample_prep(xbcs, sc[:, 0], sc[:, 1], sc[:, 2], dts, cw, cb, dtb, alog, e01)
    new_state, y_s = ssd_sample_step(state_ssm[0].reshape(nb, SSD_WIDTH, D_STATE), dec_t, xdt_t, bmat, cmat, xs_s, dskip_x)
    x1s = merge(xs_in, o_att_s, y_s, zs, attn_norm_w[0], ssd_norm_w[0], w_out1, w_out2, tms)
    y_s_out = _tail(x1s, cache_mem_k[0], cache_mem_v[0], prm, seq, tms, False, tms)

    k4 = k.reshape(batch, seq, ATT_KV_HEADS, ATT_HEAD_DIM)
    v4 = v.reshape(batch, seq, ATT_KV_HEADS, ATT_HEAD_DIM)
    xbc3 = xbc.reshape(batch, seq, CONV_DIM)
    return (y_p.reshape(batch, seq, d),
            y_s_out.reshape(nb, 1, d),
            k4[None, :, seq - WINDOW:],
            v4[None, :, seq - WINDOW:],
            p_state.reshape(1, batch, SSD_HEADS, SSD_HEAD_DIM, D_STATE),
            xbc3[None, :, seq - (CONV_WIDTH - 1):],
            mk.reshape(1, batch, N_MEM, MEM_HEADS, MEM_HEAD_DIM),
            mv.reshape(1, batch, N_MEM, MEM_HEADS, MEM_HEAD_DIM),
            nk.reshape(1, nb, WINDOW, ATT_KV_HEADS, ATT_HEAD_DIM),
            nv.reshape(1, nb, WINDOW, ATT_KV_HEADS, ATT_HEAD_DIM),
            new_state.reshape(1, nb, SSD_HEADS, SSD_HEAD_DIM, D_STATE),
            jnp.concatenate([sc[:, 1:], xbcs[:, None, :]], axis=1)[None])
```

```python
import functools
import math

import numpy as np
import jax
import jax.numpy as jnp
from jax import lax
from jax.experimental import pallas as pl
from jax.experimental.pallas import tpu as pltpu

F32 = jnp.float32
BF16 = jnp.bfloat16
EPS = 1e-6

D_MODEL = 1024
ATT_HEADS = 16
ATT_KV_HEADS = 4
ATT_HEAD_DIM = 64
WINDOW = 128
N_BUCKETS = 32
MAX_DISTANCE = 128
SSD_HEADS = 16
SSD_HEAD_DIM = 64
SSD_WIDTH = SSD_HEADS * SSD_HEAD_DIM
SSD_GROUPS = 2
D_STATE = 128
CONV_WIDTH = 4
CONV_DIM = SSD_WIDTH + 2 * SSD_GROUPS * D_STATE
CHUNK = 128
N_MEM = 256
MEM_HEADS = 4
MEM_HEAD_DIM = D_MODEL // MEM_HEADS
PEER_HEADS = 8
N_KEYS = 128
N_EXPERTS = N_KEYS * N_KEYS
PEER_TOPK = 16
PEER_HALF = 128

LANES = 128
HEAD_PAD = 128
GROUP_W = SSD_WIDTH // SSD_GROUPS
VMEM_LIMIT = 56 * 1024 * 1024


def _params(sem):
    return pltpu.CompilerParams(dimension_semantics=sem, vmem_limit_bytes=VMEM_LIMIT)


def _dot(a, b):
    return jnp.dot(a, b, preferred_element_type=F32)


def _dot_nt(a, b):
    return lax.dot_general(a, b, (((1,), (1,)), ((), ())), preferred_element_type=F32)


def _rms(x, w):
    var = jnp.mean(x * x, axis=-1, keepdims=True)
    return x * lax.rsqrt(var + EPS) * w


def _sigmoid(x):
    return 1.0 / (1.0 + jnp.exp(-x))


def _softplus(x):
    return jnp.maximum(x, 0.0) + jnp.log1p(jnp.exp(-jnp.abs(x)))


def _split3(v):
    v1 = v.astype(BF16)
    r1 = v - v1.astype(F32)
    v2 = r1.astype(BF16)
    r2 = r1 - v2.astype(F32)
    return v1, v2, r2.astype(BF16)


def _exact_dot(v, m01):
    v1, v2, v3 = _split3(v)
    return _dot(v1, m01) + _dot(v2, m01) + _dot(v3, m01)


def _exact_dot_left(m01, v):
    v1, v2, v3 = _split3(v)
    return _dot(m01, v1) + _dot(m01, v2) + _dot(m01, v3)


def _norm_matmul_kernel(x_ref, nw_ref, w_ref, *out_refs, splits, emit_ht):
    hf = _rms(x_ref[...], nw_ref[...])
    h = hf.astype(BF16)
    refs = list(out_refs)
    if emit_ht:
        refs.pop(0)[...] = pltpu.bitcast(hf.T.astype(BF16), jnp.uint32)
    off = 0
    for o_ref, n in zip(refs, splits):
        o_ref[...] = _dot(h, w_ref[:, off:off + n])
        off += n


def norm_matmul(x, nw, w_bf16, splits, tm, emit_ht=False):
    t, d = x.shape
    n = w_bf16.shape[1]
    assert sum(splits) == n and t % tm == 0
    out_shape = [jax.ShapeDtypeStruct((t, s), F32) for s in splits]
    out_specs = [pl.BlockSpec((tm, s), lambda i: (i, 0)) for s in splits]
    if emit_ht:
        out_shape.insert(0, jax.ShapeDtypeStruct((d // 2, t), jnp.uint32))
        out_specs.insert(0, pl.BlockSpec((d // 2, tm), lambda i: (0, i)))
    return pl.pallas_call(
        functools.partial(_norm_matmul_kernel, splits=tuple(splits), emit_ht=emit_ht),
        grid=(t // tm,),
        in_specs=[pl.BlockSpec((tm, d), lambda i: (i, 0)),
                  pl.BlockSpec((1, d), lambda i: (0, 0)),
                  pl.BlockSpec((d, n), lambda i: (0, 0))],
        out_specs=out_specs,
        out_shape=out_shape,
        compiler_params=_params(("parallel",)),
        name="norm_matmul",
    )(x, nw.reshape(1, d), w_bf16)


def _bias_kernel(table_ref, bucket_ref, out_ref):
    h = pl.program_id(0)
    bk = bucket_ref[...]
    acc = jnp.zeros(bk.shape, F32)
    for b in range(N_BUCKETS):
        acc = jnp.where(bk == b, table_ref[b, h], acc)
    out_ref[0] = acc


def rel_bias_rows(table, bucket):
    r, c = bucket.shape
    return pl.pallas_call(
        _bias_kernel,
        grid=(ATT_HEADS,),
        in_specs=[pl.BlockSpec(memory_space=pltpu.SMEM),
                  pl.BlockSpec((r, c), lambda h: (0, 0))],
        out_specs=pl.BlockSpec((1, r, c), lambda h: (h, 0, 0)),
        out_shape=jax.ShapeDtypeStruct((ATT_HEADS, r, c), F32),
        compiler_params=_params(("parallel",)),
        name="rel_bias",
    )(table, bucket)


def _t5_bucket(dist):
    n = jnp.maximum(dist, 0)
    max_exact = N_BUCKETS // 2
    nf = jnp.maximum(n, 1).astype(F32)
    large = max_exact + (jnp.log(nf / max_exact) / math.log(MAX_DISTANCE / max_exact)
                         * (N_BUCKETS - max_exact)).astype(jnp.int32)
    large = jnp.minimum(large, N_BUCKETS - 1)
    return jnp.where(n < max_exact, n, large)


def _half_pair(pair, rolled, which, lo):
    if which == 0:
        return jnp.where(lo, pair, 0.0), jnp.where(lo, 0.0, rolled)
    return jnp.where(lo, rolled, 0.0), jnp.where(lo, 0.0, pair)


SWA_QUERY_BLOCKS = 2


def _swa_prompt_kernel(sink_ref, q_ref, kc_ref, kp_ref, vc_ref, vp_ref, bias_ref, o_ref):
    i = pl.program_id(1)
    blk = WINDOW
    row = lax.broadcasted_iota(jnp.int32, (blk, 2 * blk), 0)
    col = lax.broadcasted_iota(jnp.int32, (blk, 2 * blk), 1)
    dist = row + blk - col
    in_window = (dist >= 0) & (dist <= WINDOW)
    first = lax.broadcasted_iota(jnp.int32, (2 * blk, 1), 0) < blk
    lo = lax.broadcasted_iota(jnp.int32, (2 * blk, LANES), 1) < ATT_HEAD_DIM
    scale = ATT_HEAD_DIM ** -0.5
    for qb in range(SWA_QUERY_BLOCKS):
        rq = slice(qb * blk, (qb + 1) * blk)
        rp = slice((qb - 1) * blk, qb * blk)
        ok1 = in_window & ((col >= blk) | (i > 0)) if qb == 0 else in_window
        ok = jnp.concatenate([ok1, ok1], axis=0)
        for pair in range(ATT_KV_HEADS // 2):
            sl = slice(pair * LANES, (pair + 1) * LANES)
            k_prev = kp_ref[:, sl] if qb == 0 else kc_ref[rp, sl]
            v_prev = vp_ref[:, sl] if qb == 0 else vc_ref[rp, sl]
            kpair = jnp.concatenate([k_prev, kc_ref[rq, sl]], axis=0)
            vpair = jnp.concatenate([v_prev, vc_ref[rq, sl]], axis=0)
            kroll = pltpu.roll(kpair, ATT_HEAD_DIM, axis=1)
            vroll = pltpu.roll(vpair, ATT_HEAD_DIM, axis=1)
            for gi in range(2):
                g = 2 * pair + gi
                k_lo, k_hi = _half_pair(kpair, kroll, gi, lo)
                v_lo, v_hi = _half_pair(vpair, vroll, gi, lo)
                k_halves = (k_lo.astype(BF16), k_hi.astype(BF16))
                v_halves = (v_lo.astype(BF16), v_hi.astype(BF16))
                qsl = slice(2 * g * LANES, (2 * g + 2) * LANES)
                q2 = jnp.concatenate([q_ref[rq, 2 * g * LANES:(2 * g + 1) * LANES],
                                      q_ref[rq, (2 * g + 1) * LANES:(2 * g + 2) * LANES]],
                                     axis=0).astype(BF16)
                acc = jnp.zeros((2 * blk, LANES), F32)
                for par in range(2):
                    h0, h1 = 4 * g + par, 4 * g + 2 + par
                    bias = jnp.concatenate([bias_ref[h0], bias_ref[h1]], axis=0)
                    s = jnp.where(ok, _dot_nt(q2, k_halves[par]) * scale + bias, -jnp.inf)
                    sink = jnp.where(first, sink_ref[h0], sink_ref[h1])
                    m = jnp.maximum(jnp.max(s, axis=-1, keepdims=True), sink)
                    p = jnp.exp(s - m)
                    den = jnp.sum(p, axis=-1, keepdims=True) + jnp.exp(sink - m)
                    acc = acc + _dot((p / den).astype(BF16), v_halves[par])
                o_ref[rq, qsl] = jnp.concatenate([acc[:blk], acc[blk:]], axis=1)


def swa_prompt(q, k, v, bias, sinks, batch, seq):
    nb = seq // WINDOW
    qb = SWA_QUERY_BLOCKS
    assert nb % qb == 0
    cur = lambda b, i: (b * (nb // qb) + i, 0)
    prev = lambda b, i: (b * nb + jnp.maximum(qb * i - 1, 0), 0)
    kvw = ATT_KV_HEADS * ATT_HEAD_DIM
    return pl.pallas_call(
        _swa_prompt_kernel,
        grid=(batch, nb // qb),
        in_specs=[pl.BlockSpec(memory_space=pltpu.SMEM),
                  pl.BlockSpec((qb * WINDOW, D_MODEL), cur),
                  pl.BlockSpec((qb * WINDOW, kvw), cur),
                  pl.BlockSpec((WINDOW, kvw), prev),
                  pl.BlockSpec((qb * WINDOW, kvw), cur),
                  pl.BlockSpec((WINDOW, kvw), prev),
                  pl.BlockSpec((ATT_HEADS, WINDOW, 2 * WINDOW), lambda b, i: (0, 0, 0))],
        out_specs=pl.BlockSpec((qb * WINDOW, D_MODEL), cur),
        out_shape=jax.ShapeDtypeStruct((batch * seq, D_MODEL), F32),
        compiler_params=_params(("parallel", "parallel")),
        name="swa_prompt",
    )(sinks, q, k, k, v, v, bias)


def _ssd_prompt_kernel(xbc_ref, dt_ref, cw_ref, cb_ref, dtb_ref, alog_ref, dskip_ref, e_ref,
                       y_ref, st_ref, state_scr, tail_scr):
    c = pl.program_id(1)
    last = pl.num_programs(1) - 1

    @pl.when(c == 0)
    def _():
        state_scr[...] = jnp.zeros(state_scr.shape, F32)
        tail_scr[...] = jnp.zeros(tail_scr.shape, F32)

    x = xbc_ref[...]
    tail = tail_scr[...]
    row8 = lax.broadcasted_iota(jnp.int32, tail.shape, 0)
    acc = x * cw_ref[CONV_WIDTH - 1:CONV_WIDTH, :]
    for s in range(1, CONV_WIDTH):
        xr = pltpu.roll(x, s, axis=0)
        tr = pltpu.roll(tail, s, axis=0)
        head = jnp.where(row8 < s, tr, xr[0:8])
        shifted = jnp.concatenate([head, xr[8:]], axis=0)
        acc = acc + shifted * cw_ref[CONV_WIDTH - 1 - s:CONV_WIDTH - s, :]
    tail_scr[...] = x[CHUNK - 8:CHUNK]
    u = acc + cb_ref[...]
    u = u * _sigmoid(u)
    xs = u[:, :SSD_WIDTH]
    bm = u[:, SSD_WIDTH:SSD_WIDTH + SSD_GROUPS * D_STATE]
    cm = u[:, SSD_WIDTH + SSD_GROUPS * D_STATE:]

    dtv = _softplus(dt_ref[...] + dtb_ref[...])
    a = dtv * (-jnp.exp(alog_ref[...]))
    r = lax.broadcasted_iota(jnp.int32, (CHUNK, CHUNK), 0)
    cc = lax.broadcasted_iota(jnp.int32, (CHUNK, CHUNK), 1)
    causal = r >= cc
    tril = jnp.where(causal, 1.0, 0.0).astype(BF16)
    cs = _exact_dot_left(tril, a)
    cs_t = cs.T
    dte = jnp.exp(cs[CHUNK - 1:CHUNK, :] - cs)
    ecs = jnp.exp(cs)
    e01 = e_ref[...]
    dt_x = _exact_dot(dtv, e01)
    dte_x = _exact_dot(dte, e01)
    ecs_x = _exact_dot(ecs, e01)
    xdt = xs * dt_x
    xdec = (xdt * dte_x).astype(BF16)
    xdt_b = xdt.astype(BF16)
    lo = lax.broadcasted_iota(jnp.int32, (CHUNK, LANES), 1) < SSD_HEAD_DIM
    dsk = dskip_ref[...]
    for g in range(SSD_GROUPS):
        gs = slice(g * GROUP_W, (g + 1) * GROUP_W)
        bg = bm[:, g * D_STATE:(g + 1) * D_STATE]
        cg = cm[:, g * D_STATE:(g + 1) * D_STATE].astype(BF16)
        cb = _dot_nt(cg, bg.astype(BF16))
        bgt = bg.T.astype(BF16)
        prev = state_scr[g]
        y_off = _dot(cg, prev.astype(BF16)) * ecs_x[:, gs]
        state_scr[g] = ecs_x[CHUNK - 1:CHUNK, gs] * prev + _dot(bgt, xdec[:, gs])
        for m in range(GROUP_W // LANES):
            ps = slice(g * GROUP_W + m * LANES, g * GROUP_W + (m + 1) * LANES)
            xp = xdt_b[:, ps]
            ydiag = jnp.zeros((CHUNK, LANES), F32)
            for par in range(2):
                j = g * (SSD_HEADS // SSD_GROUPS) + 2 * m + par
                diff = cs[:, j:j + 1] - cs_t[j:j + 1, :]
                ldec = jnp.exp(jnp.where(causal, diff, -jnp.inf))
                mm = (cb * ldec).astype(BF16)
                xpm = jnp.where(lo, xp, 0.0) if par == 0 else jnp.where(lo, 0.0, xp)
                ydiag = ydiag + _dot(mm, xpm.astype(BF16))
            y_ref[:, ps] = ydiag + y_off[:, m * LANES:(m + 1) * LANES] + dsk[:, ps] * xs[:, ps]

    @pl.when(c == last)
    def _():
        for g in range(SSD_GROUPS):
            st_ref[0, g * GROUP_W:(g + 1) * GROUP_W, :] = state_scr[g].T


def ssd_prompt(xbc, dt_raw, cw, cb, dtb, alog, dskip_x, e01, batch, seq):
    nc = seq // CHUNK
    cur = lambda b, c: (b * nc + c, 0)
    const = lambda b, c: (0, 0)
    return pl.pallas_call(
        _ssd_prompt_kernel,
        grid=(batch, nc),
        in_specs=[pl.BlockSpec((CHUNK, CONV_DIM), cur),
                  pl.BlockSpec((CHUNK, HEAD_PAD), cur),
                  pl.BlockSpec((CONV_WIDTH, CONV_DIM), const),
                  pl.BlockSpec((1, CONV_DIM), const),
                  pl.BlockSpec((1, HEAD_PAD), const),
                  pl.BlockSpec((1, HEAD_PAD), const),
                  pl.BlockSpec((1, SSD_WIDTH), const),
                  pl.BlockSpec((HEAD_PAD, SSD_WIDTH), const)],
        out_specs=[pl.BlockSpec((CHUNK, SSD_WIDTH), cur),
                   pl.BlockSpec((1, SSD_WIDTH, D_STATE), lambda b, c: (b, 0, 0))],
        out_shape=[jax.ShapeDtypeStruct((batch * seq, SSD_WIDTH), F32),
                   jax.ShapeDtypeStruct((batch, SSD_WIDTH, D_STATE), F32)],
        scratch_shapes=[pltpu.VMEM((SSD_GROUPS, D_STATE, GROUP_W), F32),
                        pltpu.VMEM((8, CONV_DIM), F32)],
        compiler_params=_params(("parallel", "arbitrary")),
        name="ssd_prompt",
    )(xbc, dt_raw, cw, cb, dtb, alog, dskip_x, e01)


def _swa_sample_kernel(qx_ref, ck_ref, cv_ref, kn_ref, vn_ref, bias_ref, sink_ref,
                       nk_ref, nv_ref, o_ref, *, bt):
    scale = ATT_HEAD_DIM ** -0.5
    kvw = ATT_KV_HEADS * ATT_HEAD_DIM
    rowi = lax.broadcasted_iota(jnp.int32, (WINDOW, kvw), 0)
    bias_c = bias_ref[:, 0:WINDOW]
    bias_n = bias_ref[:, WINDOW:WINDOW + 1]
    sink = sink_ref[:, 0:1]
    for bb in range(bt):
        kc = ck_ref[bb]
        vc = cv_ref[bb]
        kn = kn_ref[bb]
        vn = vn_ref[bb]
        qx = qx_ref[bb]
        s_c = _dot_nt(qx.astype(BF16), kc.astype(BF16)) * scale + bias_c
        s_n = jnp.sum(qx * kn, axis=1, keepdims=True) * scale + bias_n
        m = jnp.maximum(jnp.maximum(jnp.max(s_c, axis=1, keepdims=True), s_n), sink)
        p_c = jnp.exp(s_c - m)
        p_n = jnp.exp(s_n - m)
        den = jnp.sum(p_c, axis=1, keepdims=True) + p_n + jnp.exp(sink - m)
        o_ref[bb] = _dot((p_c / den).astype(BF16), vc.astype(BF16)) + (p_n / den) * vn
        nk_ref[bb] = jnp.where(rowi == WINDOW - 1, kn, pltpu.roll(kc, WINDOW - 1, axis=0))
        nv_ref[bb] = jnp.where(rowi == WINDOW - 1, vn, pltpu.roll(vc, WINDOW - 1, axis=0))


def swa_sample(qx, ck, cv, kn, vn, bias_s, sink_b, bt=8):
    nb = qx.shape[0]
    kvw = ATT_KV_HEADS * ATT_HEAD_DIM
    blk3 = lambda i: (i, 0, 0)
    return pl.pallas_call(
        functools.partial(_swa_sample_kernel, bt=bt),
        grid=(nb // bt,),
        in_specs=[pl.BlockSpec((bt, ATT_HEADS, kvw), blk3),
                  pl.BlockSpec((bt, WINDOW, kvw), blk3),
                  pl.BlockSpec((bt, WINDOW, kvw), blk3),
                  pl.BlockSpec((bt, 1, kvw), blk3),
                  pl.BlockSpec((bt, 1, kvw), blk3),
                  pl.BlockSpec((ATT_HEADS, 2 * WINDOW), lambda i: (0, 0)),
                  pl.BlockSpec((ATT_HEADS, LANES), lambda i: (0, 0))],
        out_specs=[pl.BlockSpec((bt, WINDOW, kvw), blk3),
                   pl.BlockSpec((bt, WINDOW, kvw), blk3),
                   pl.BlockSpec((bt, ATT_HEADS, kvw), blk3)],
        out_shape=[jax.ShapeDtypeStruct((nb, WINDOW, kvw), F32),
                   jax.ShapeDtypeStruct((nb, WINDOW, kvw), F32),
                   jax.ShapeDtypeStruct((nb, ATT_HEADS, kvw), F32)],
        compiler_params=_params(("parallel",)),
        name="swa_sample",
    )(qx, ck, cv, kn, vn, bias_s, sink_b)


def _ssd_sample_prep_kernel(xbc_ref, c0_ref, c1_ref, c2_ref, dt_ref, cw_ref, cb_ref, dtb_ref,
                            alog_ref, e_ref, dec_t_ref, xdt_t_ref, b_ref, c_ref, xs_ref):
    u = (c0_ref[...] * cw_ref[0:1, :] + c1_ref[...] * cw_ref[1:2, :]
         + c2_ref[...] * cw_ref[2:3, :] + xbc_ref[...] * cw_ref[3:4, :]) + cb_ref[...]
    u = u * _sigmoid(u)
    xs = u[:, :SSD_WIDTH]
    b_ref[...] = u[:, SSD_WIDTH:SSD_WIDTH + SSD_GROUPS * D_STATE]
    c_ref[...] = u[:, SSD_WIDTH + SSD_GROUPS * D_STATE:]
    xs_ref[...] = xs
    dtv = _softplus(dt_ref[...] + dtb_ref[...])
    dec = jnp.exp(dtv * (-jnp.exp(alog_ref[...])))
    e01 = e_ref[...]
    dec_t_ref[...] = _exact_dot(dec, e01).T
    xdt_t_ref[...] = (xs * _exact_dot(dtv, e01)).T


def ssd_sample_prep(xbc, c0, c1, c2, dt_raw, cw, cb, dtb, alog, e01):
    nb = xbc.shape[0]
    args = (xbc, c0, c1, c2, dt_raw, cw, cb, dtb, alog, e01)
    full = lambda a: pl.BlockSpec(a.shape, lambda i: (0,) * a.ndim)
    out_shape = [jax.ShapeDtypeStruct((SSD_WIDTH, nb), F32),
                 jax.ShapeDtypeStruct((SSD_WIDTH, nb), F32),
                 jax.ShapeDtypeStruct((nb, SSD_GROUPS * D_STATE), F32),
                 jax.ShapeDtypeStruct((nb, SSD_GROUPS * D_STATE), F32),
                 jax.ShapeDtypeStruct((nb, SSD_WIDTH), F32)]
    return pl.pallas_call(
        _ssd_sample_prep_kernel,
        grid=(1,),
        in_specs=[full(a) for a in args],
        out_specs=[full(s) for s in out_shape],
        out_shape=out_shape,
        compiler_params=_params(("arbitrary",)),
        name="ssd_sample_prep",
    )(*args)


def _ssd_sample_step_kernel(st_ref, dec_t_ref, xdt_t_ref, b_ref, c_ref, xs_ref, dskip_ref,
                            ns_ref, y_ref, yt_scr, *, bt):
    i = pl.program_id(0)
    nb = dec_t_ref.shape[1]

    @pl.when(i == 0)
    def _():
        yt_scr[...] = jnp.zeros(yt_scr.shape, F32)

    dparts = _split3(dec_t_ref[...])
    xparts = _split3(xdt_t_ref[...])
    lane = lax.broadcasted_iota(jnp.int32, (GROUP_W, nb), 1)
    for bb in range(bt):
        b = i * bt + bb
        pick = jnp.where(lax.broadcasted_iota(jnp.int32, (nb, D_STATE), 0) == b,
                         1.0, 0.0).astype(BF16)
        dcol = sum(_dot(p, pick) for p in dparts)
        xcol = sum(_dot(p, pick) for p in xparts)
        brow = b_ref[pl.ds(b, 1), :]
        crow = c_ref[pl.ds(b, 1), :]
        for g in range(SSD_GROUPS):
            gs = slice(g * GROUP_W, (g + 1) * GROUP_W)
            ns = slice(g * D_STATE, (g + 1) * D_STATE)
            hn = dcol[gs] * st_ref[bb, gs, :] + xcol[gs] * brow[:, ns]
            ns_ref[bb, gs, :] = hn
            ycol = jnp.sum(hn * crow[:, ns], axis=1, keepdims=True)
            yt_scr[gs, :] = jnp.where(lane == b, ycol, yt_scr[gs, :])

    @pl.when(i == pl.num_programs(0) - 1)
    def _():
        y_ref[...] = yt_scr[...].T + dskip_ref[...] * xs_ref[...]


def ssd_sample_step(state, dec_t, xdt_t, bmat, cmat, xs, dskip_x, bt=8):
    nb = state.shape[0]
    assert nb == D_STATE
    const2 = lambda i: (0, 0)
    return pl.pallas_call(
        functools.partial(_ssd_sample_step_kernel, bt=bt),
        grid=(nb // bt,),
        in_specs=[pl.BlockSpec((bt, SSD_WIDTH, D_STATE), lambda i: (i, 0, 0)),
                  pl.BlockSpec((SSD_WIDTH, nb), const2),
                  pl.BlockSpec((SSD_WIDTH, nb), const2),
                  pl.BlockSpec((nb, SSD_GROUPS * D_STATE), const2),
                  pl.BlockSpec((nb, SSD_GROUPS * D_STATE), const2),
                  pl.BlockSpec((nb, SSD_WIDTH), const2),
                  pl.BlockSpec((1, SSD_WIDTH), const2)],
        out_specs=[pl.BlockSpec((bt, SSD_WIDTH, D_STATE), lambda i: (i, 0, 0)),
                   pl.BlockSpec((nb, SSD_WIDTH), const2)],
        out_shape=[jax.ShapeDtypeStruct((nb, SSD_WIDTH, D_STATE), F32),
                   jax.ShapeDtypeStruct((nb, SSD_WIDTH), F32)],
        scratch_shapes=[pltpu.VMEM((SSD_WIDTH, nb), F32)],
        compiler_params=_params(("arbitrary",)),
        name="ssd_sample_step",
    )(state, dec_t, xdt_t, bmat, cmat, xs, dskip_x)


def _merge_kernel(x_ref, o_ref, y_ref, z_ref, anw_ref, snw_ref, w1_ref, w2_ref, out_ref):
    ya = _rms(o_ref[...], anw_ref[...]).astype(BF16)
    z = z_ref[...]
    ys = _rms(y_ref[...] * (z * _sigmoid(z)), snw_ref[...]).astype(BF16)
    out_ref[...] = x_ref[...] + _dot(ya, w1_ref[...]) + _dot(ys, w2_ref[...])


def merge(x, o_att, y_ssd, z, anw, snw, w1, w2, tm):
    t, d = x.shape
    row = pl.BlockSpec((tm, d), lambda i: (i, 0))
    vec = pl.BlockSpec((1, d), lambda i: (0, 0))
    mat = pl.BlockSpec((d, d), lambda i: (0, 0))
    return pl.pallas_call(
        _merge_kernel,
        grid=(t // tm,),
        in_specs=[row, row, row, row, vec, vec, mat, mat],
        out_specs=row,
        out_shape=jax.ShapeDtypeStruct((t, d), F32),
        compiler_params=_params(("parallel",)),
        name="merge",
    )(x, o_att, y_ssd, z, anw.reshape(1, d), snw.reshape(1, d), w1, w2)


def _mem_prompt_kernel(x_ref, nw_ref, wq_ref, mk_ref, mv_ref, wo_ref, out_ref):
    x = x_ref[...]
    q = _dot(_rms(x, nw_ref[...]).astype(BF16), wq_ref[...])
    scale = MEM_HEAD_DIM ** -0.5
    outs = []
    for h in range(MEM_HEADS):
        hs = slice(h * MEM_HEAD_DIM, (h + 1) * MEM_HEAD_DIM)
        s = _dot_nt(q[:, hs].astype(BF16), mk_ref[:, hs]) * scale
        p = jnp.exp(s - jnp.max(s, axis=-1, keepdims=True))
        p = p / jnp.sum(p, axis=-1, keepdims=True)
        outs.append(_dot(p.astype(BF16), mv_ref[:, hs]).astype(BF16))
    out_ref[...] = x + _dot(jnp.concatenate(outs, axis=1), wo_ref[...])


def mem_attn_prompt(x, nw, wq, mk, mv, wo, seq, tm):
    t, d = x.shape
    per = seq // tm
    row = pl.BlockSpec((tm, d), lambda i: (i, 0))
    mat = pl.BlockSpec((d, d), lambda i: (0, 0))
    mem = pl.BlockSpec((N_MEM, d), lambda i: (i // per, 0))
    return pl.pallas_call(
        _mem_prompt_kernel,
        grid=(t // tm,),
        in_specs=[row, pl.BlockSpec((1, d), lambda i: (0, 0)), mat, mem, mem, mat],
        out_specs=row,
        out_shape=jax.ShapeDtypeStruct((t, d), F32),
        compiler_params=_params(("parallel",)),
        name="mem_prompt",
    )(x, nw.reshape(1, d), wq, mk, mv, wo)


def _mem_sample_kernel(q_ref, k_ref, v_ref, o_ref, *, bt):
    rows = N_MEM * MEM_HEADS
    qh = lax.broadcasted_iota(jnp.int32, (8, rows), 0)
    kh = lax.broadcasted_iota(jnp.int32, (8, rows), 1) % MEM_HEADS
    live = qh < MEM_HEADS
    for bb in range(bt):
        q4 = q_ref[bb]
        q8 = jnp.concatenate([q4, jnp.zeros_like(q4)], axis=0).astype(BF16)
        k2 = k_ref[bb].reshape(rows, MEM_HEAD_DIM).astype(BF16)
        v2 = v_ref[bb].reshape(rows, MEM_HEAD_DIM).astype(BF16)
        s = _dot_nt(q8, k2) * (MEM_HEAD_DIM ** -0.5)
        s = jnp.where(kh == qh, s, -jnp.inf)
        p = jnp.exp(s - jnp.where(live, jnp.max(s, axis=1, keepdims=True), 0.0))
        p = p / jnp.where(live, jnp.sum(p, axis=1, keepdims=True), 1.0)
        o_ref[bb] = _dot(p.astype(BF16), v2)[0:MEM_HEADS]


def mem_sample(q, ck, cv, bt=4):
    nb, d = q.shape
    blk = pl.BlockSpec((bt, N_MEM, MEM_HEADS, MEM_HEAD_DIM), lambda i: (i, 0, 0, 0))
    vec = pl.BlockSpec((bt, MEM_HEADS, MEM_HEAD_DIM), lambda i: (i, 0, 0))
    return pl.pallas_call(
        functools.partial(_mem_sample_kernel, bt=bt),
        grid=(nb // bt,),
        in_specs=[vec, blk, blk],
        out_specs=vec,
        out_shape=jax.ShapeDtypeStruct((nb, MEM_HEADS, MEM_HEAD_DIM), F32),
        compiler_params=_params(("parallel",)),
        name="mem_sample",
    )(q.reshape(nb, MEM_HEADS, MEM_HEAD_DIM), ck, cv).reshape(nb, d)


def _matmul_res_kernel(a_ref, w_ref, r_ref, out_ref):
    out_ref[...] = r_ref[...] + _dot(a_ref[...].astype(BF16), w_ref[...])


def matmul_res(a, w, res, tm):
    t, d = res.shape
    kdim = a.shape[1]
    return pl.pallas_call(
        _matmul_res_kernel,
        grid=(t // tm,),
        in_specs=[pl.BlockSpec((tm, kdim), lambda i: (i, 0)),
                  pl.BlockSpec((kdim, d), lambda i: (0, 0)),
                  pl.BlockSpec((tm, d), lambda i: (i, 0))],
        out_specs=pl.BlockSpec((tm, d), lambda i: (i, 0)),
        out_shape=jax.ShapeDtypeStruct((t, d), F32),
        compiler_params=_params(("parallel",)),
        name="matmul_res",
    )(a, w, res)


def _topk_rows(work, k):
    rows = work.shape[0]
    iota = lax.broadcasted_iota(jnp.int32, work.shape, 0)
    rank = jnp.full(work.shape, k, jnp.int32)
    vals = []
    for r in range(k):
        m = jnp.max(work, axis=0, keepdims=True)
        idx = jnp.min(jnp.where(work == m, iota, rows), axis=0, keepdims=True)
        hit = iota == idx
        rank = jnp.where(hit, r, rank)
        work = jnp.where(hit, -jnp.inf, work)
        vals.append(m)
    return jnp.concatenate(vals, axis=0), rank


def _topk_rows_distinct(work, k, want_rank):
    rank = jnp.full(work.shape, k, jnp.int32) if want_rank else None
    vals = []
    for r in range(k):
        m = jnp.max(work, axis=0, keepdims=True)
        hit = work == m
        if want_rank:
            rank = jnp.where(hit, r, rank)
        work = jnp.where(hit, -jnp.inf, work)
        vals.append(m)
    taken = work == -jnp.inf
    count = jnp.sum(jnp.where(taken, 1.0, 0.0), axis=0, keepdims=True)
    return jnp.concatenate(vals, axis=0), rank, taken, count


def _peer_candidates(a, b):
    lo4 = lax.broadcasted_iota(jnp.int32, (8, a.shape[1]), 0) < 4
    b8 = b[0:8]
    b44 = jnp.where(lo4, b8, pltpu.roll(b8, 4, axis=0))
    return jnp.concatenate(
        [a[0:1] + b8, a[0:1] + b[8:16], a[1:2] + b8, a[2:3] + b8, a[3:4] + b8,
         jnp.where(lo4, a[4:5], a[5:6]) + b44, jnp.where(lo4, a[6:7], a[7:8]) + b44,
         a[8:16] + b[0:1]], axis=0)


def _peer_emit(subs, a, b, rank0, rank1, cand, chosen, seli_ref, selj_ref):
    k = PEER_TOPK
    cmax = a[0:1, :] + b[0:1, :]
    z = jnp.sum(jnp.where(chosen, jnp.exp(cand - cmax), 0.0), axis=0, keepdims=True)
    lo4 = lax.broadcasted_iota(jnp.int32, (8, a.shape[1]), 0) < 4
    ch = [jnp.where(chosen[8 * i:8 * i + 8], 1.0, 0.0) for i in range(8)]
    cnt = lambda v: jnp.sum(v, axis=0, keepdims=True)
    n = [cnt(ch[0]) + cnt(ch[1]), cnt(ch[2]), cnt(ch[3]), cnt(ch[4]),
         cnt(jnp.where(lo4, ch[5], 0.0)), cnt(jnp.where(lo4, 0.0, ch[5])),
         cnt(jnp.where(lo4, ch[6], 0.0)), cnt(jnp.where(lo4, 0.0, ch[6]))]
    n += [ch[7][i:i + 1] for i in range(8)]
    lim0 = jnp.zeros(subs[0].shape, F32)
    if rank0 is None:
        for r in reversed(range(k)):
            lim0 = jnp.where(subs[0] >= a[r:r + 1, :], n[r], lim0)
    else:
        for r in range(k):
            lim0 = jnp.where(rank0 == r, n[r], lim0)

    def twice(v):
        bits = pltpu.bitcast(v.astype(BF16).astype(F32), jnp.uint32)
        return bits | (bits >> 16)

    seli_ref[0, 0] = twice(0.5 * jnp.exp(subs[0] - a[0:1, :]) / z)
    seli_ref[0, 1] = twice(lim0)
    selj_ref[0, 0] = pltpu.bitcast(jnp.exp(subs[1] - b[0:1, :]).astype(BF16), jnp.uint32)
    selj_ref[0, 1] = pltpu.bitcast(rank1.astype(F32).astype(BF16), jnp.uint32)


def _peer_select_kernel(pq_ref, keys_ref, seli_ref, selj_ref):
    k = PEER_TOPK
    subs = [_dot_nt(keys_ref[p], pq_ref[:, p * PEER_HALF:(p + 1) * PEER_HALF].astype(BF16))
            for p in range(2)]
    a, _, _, took0 = _topk_rows_distinct(subs[0], k, want_rank=False)
    b, rank1, _, took1 = _topk_rows_distinct(subs[1], k, want_rank=True)
    cand = _peer_candidates(a, b)
    _, _, chosen, tookc = _topk_rows_distinct(cand, k, want_rank=False)
    _peer_emit(subs, a, b, None, rank1, cand, chosen, seli_ref, selj_ref)
    tied = jnp.where((took0 != k) | (took1 != k) | (tookc != k), 1.0, 0.0)

    @pl.when(jnp.max(tied) > 0.0)
    def _():
        a, rank0 = _topk_rows(subs[0], k)
        b, rank1 = _topk_rows(subs[1], k)
        cand = _peer_candidates(a, b)
        _, crank = _topk_rows(cand, k)
        _peer_emit(subs, a, b, rank0, rank1, cand, crank < k, seli_ref, selj_ref)


def peer_select(pq, keys_bf16, tm):
    t = pq.shape[0]
    spec = lambda rows: pl.BlockSpec((1, 2, rows, tm), lambda i, h: (h, 0, 0, i))
    return pl.pallas_call(
        _peer_select_kernel,
        grid=(t // tm, PEER_HEADS),
        in_specs=[pl.BlockSpec((tm, 2 * PEER_HALF), lambda i, h: (i, h)),
                  pl.BlockSpec((2, N_KEYS, PEER_HALF), lambda i, h: (h, 0, 0))],
        out_specs=[spec(N_KEYS), spec(N_KEYS // 2)],
        out_shape=[jax.ShapeDtypeStruct((PEER_HEADS, 2, N_KEYS, t), jnp.uint32),
                   jax.ShapeDtypeStruct((PEER_HEADS, 2, N_KEYS // 2, t), jnp.uint32)],
        compiler_params=_params(("parallel", "parallel")),
        name="peer_select",
    )(pq, keys_bf16)


def _gelu_x2(x):
    return x * (1.0 + lax.erf(x * np.float32(math.sqrt(0.5))))


def _peer_dense_kernel(xnt_ref, x_ref, u_ref, vt_ref, seli_ref, selj_ref, fnw_ref, y_ref,
                       acc_scr, act_scr, w_scr, *, tm, ib, sub):
    e = pl.program_id(1)
    ne = pl.num_programs(1)

    @pl.when(e == 0)
    def _():
        acc_scr[...] = jnp.zeros(acc_scr.shape, F32)

    i8 = pl.ds(pl.multiple_of(e * ib, 8), ib)
    nsub = ib // sub
    d = acc_scr.shape[0]
    halves = 2
    sub_rows = lambda k: slice(k * sub * N_KEYS, (k + 1) * sub * N_KEYS)

    def activations(k, half):
        cs = slice(half * tm // halves, (half + 1) * tm // halves)
        u = pltpu.bitcast(u_ref[k * sub * N_KEYS // 2:(k + 1) * sub * N_KEYS // 2, :], BF16)
        act_scr[k % act_scr.shape[0], :, cs] = _dot(u, pltpu.bitcast(xnt_ref[:, cs], BF16))

    def accumulate(k, half):
        rs = slice(half * d // halves, (half + 1) * d // halves)
        ws = slice(half * d // (2 * halves), (half + 1) * d // (2 * halves))
        acc_scr[rs, :] += _dot(pltpu.bitcast(vt_ref[0, ws, sub_rows(k)], BF16),
                               w_scr[k % w_scr.shape[0]])

    def gates(k, lt):
        ls = slice(lt * LANES, (lt + 1) * LANES)
        for j0 in range(0, sub, 2):
            gs = [jnp.zeros((N_KEYS, LANES), BF16) for _ in range(2)]
            for h in range(PEER_HEADS):
                e1 = pltpu.bitcast(selj_ref[h, 0, :, ls], BF16)
                rank1 = pltpu.bitcast(selj_ref[h, 1, :, ls], BF16)
                for t in range(2):
                    ii = k * sub + j0 + t
                    row = lambda c: pltpu.bitcast(jnp.broadcast_to(
                        seli_ref[h, c, i8, ls][ii:ii + 1, :], (N_KEYS // 2, LANES)), BF16)
                    gs[t] = gs[t] + jnp.where(rank1 < row(1), e1, jnp.zeros_like(e1)) * row(0)
            for t in range(2):
                rs = slice((j0 + t) * N_KEYS, (j0 + t + 1) * N_KEYS)
                act = act_scr[k % act_scr.shape[0], rs, ls]
                w_scr[k % w_scr.shape[0], rs, ls] = gs[t] * _gelu_x2(act).astype(BF16)

    for half in range(halves):
        activations(0, half)
    for k in range(nsub):
        mxu_jobs = []
        if k + 1 < nsub:
            mxu_jobs += [functools.partial(activations, k + 1, half) for half in range(halves)]
        if k >= 1:
            mxu_jobs += [functools.partial(accumulate, k - 1, half) for half in range(halves)]
        for job in mxu_jobs:
            job()
        for lt in range(tm // LANES):
            gates(k, lt)
    for half in range(halves):
        accumulate(nsub - 1, half)

    @pl.when(e == ne - 1)
    def _():
        y_ref[...] = _rms(x_ref[...] + acc_scr[...].T, fnw_ref[...])


PEER_BLOCK_ROWS = 16
PEER_SUB_ROWS = 4


def _pack_peer_kernel(u_ref, v_ref, up_ref, vtp_ref):
    up_ref[...] = pltpu.bitcast(u_ref[...].astype(BF16), jnp.uint32)
    vtp_ref[0] = pltpu.bitcast(v_ref[...].T.astype(BF16), jnp.uint32)


def pack_peer_weights(u, v, ib=PEER_BLOCK_ROWS):
    n, d = u.shape
    eb = ib * N_KEYS
    return pl.pallas_call(
        _pack_peer_kernel,
        grid=(n // eb,),
        in_specs=[pl.BlockSpec((eb, d), lambda e: (e, 0)), pl.BlockSpec((eb, d), lambda e: (e, 0))],
        out_specs=[pl.BlockSpec((eb // 2, d), lambda e: (e, 0)),
                   pl.BlockSpec((1, d // 2, eb), lambda e: (e, 0, 0))],
        out_shape=[jax.ShapeDtypeStruct((n // 2, d), jnp.uint32),
                   jax.ShapeDtypeStruct((n // eb, d // 2, eb), jnp.uint32)],
        compiler_params=_params(("parallel",)),
        name="pack_peer_weights",
    )(u, v)


def peer_dense(xnt, x, u_packed, vt_packed, seli, selj, fnw, tm, ib=PEER_BLOCK_ROWS,
               sub=PEER_SUB_ROWS):
    t, d = x.shape
    eb = ib * N_KEYS
    nblk = N_EXPERTS // eb
    assert vt_packed.shape == (nblk, d // 2, eb) and xnt.shape == (d // 2, t)
    sel_spec = lambda rows: pl.BlockSpec((PEER_HEADS, 2, rows, tm), lambda i, e: (0, 0, 0, i))
    return pl.pallas_call(
        functools.partial(_peer_dense_kernel, tm=tm, ib=ib, sub=sub),
        grid=(t // tm, nblk),
        in_specs=[pl.BlockSpec((d // 2, tm), lambda i, e: (0, i)),
                  pl.BlockSpec((tm, d), lambda i, e: (i, 0)),
                  pl.BlockSpec((eb // 2, d), lambda i, e: (e, 0)),
                  pl.BlockSpec((1, d // 2, eb), lambda i, e: (e, 0, 0)),
                  sel_spec(N_KEYS), sel_spec(N_KEYS // 2),
                  pl.BlockSpec((1, d), lambda i, e: (0, 0))],
        out_specs=pl.BlockSpec((tm, d), lambda i, e: (i, 0)),
        out_shape=jax.ShapeDtypeStruct((t, d), F32),
        scratch_shapes=[pltpu.VMEM((d, tm), F32),
                        pltpu.VMEM((min(2, ib // sub), sub * N_KEYS, tm), F32),
                        pltpu.VMEM((min(3, ib // sub), sub * N_KEYS, tm), BF16)],
        compiler_params=_params(("parallel", "arbitrary")),
        name="peer_dense",
    )(xnt, x, u_packed, vt_packed, seli, selj, fnw.reshape(1, d))


def _tail(x, mk, mv, prm, seq, tm, mem_is_shared, tm_peer):
    if mem_is_shared:
        x = mem_attn_prompt(x, prm["norm_mem_w"], prm["w_mq"], mk, mv, prm["w_mo"], seq, tm)
    else:
        (q,) = norm_matmul(x, prm["norm_mem_w"], prm["w_mq"], (D_MODEL,), tm)
        x = matmul_res(mem_sample(q, mk, mv), prm["w_mo"], x, tm)
    xnt, pq = norm_matmul(x, prm["norm_ffn_w"], prm["peer_wq"], (prm["peer_wq"].shape[1],), tm,
                          emit_ht=True)
    seli, selj = peer_select(pq, prm["peer_keys"], tm)
    return peer_dense(xnt, x, prm["peer_u"], prm["peer_vt"], seli, selj, prm["final_norm_w"],
                      tm_peer)


def kernel(x_prompt, x_sample, mem_prompt, cache_swa_k, cache_swa_v, state_ssm, state_conv, cache_mem_k, cache_mem_v, norm_mix_w, w_in, attn_sinks, rel_bias, attn_norm_w, conv_w, conv_b, dt_bias, a_log, d_skip, ssd_norm_w, w_out, norm_mem_w, mem_norm_w, w_mq, w_mk, w_mv, w_mo, norm_ffn_w, peer_wq, peer_keys, peer_u, peer_v, final_norm_w):
    assert w_in.shape[0] == 1, "single-layer step"
    batch, seq, d = x_prompt.shape
    nb = x_sample.shape[0]
    kvw = ATT_KV_HEADS * ATT_HEAD_DIM
    tm = 512

    n_main = D_MODEL + 2 * kvw + SSD_WIDTH + CONV_DIM
    w_in_p = jnp.concatenate(
        [w_in[0], jnp.zeros((d, HEAD_PAD - SSD_HEADS), F32)], axis=1).astype(BF16)
    in_splits = (D_MODEL, kvw, kvw, SSD_WIDTH, CONV_DIM, HEAD_PAD)
    assert n_main + SSD_HEADS == w_in.shape[2]
    pad_h = lambda v: jnp.pad(v.reshape(1, SSD_HEADS), ((0, 0), (0, HEAD_PAD - SSD_HEADS)))
    dtb, alog = pad_h(dt_bias[0]), pad_h(a_log[0])
    dskip_x = jnp.repeat(d_skip[0], SSD_HEAD_DIM).reshape(1, SSD_WIDTH)
    e01 = (jnp.arange(HEAD_PAD)[:, None] == jnp.arange(SSD_WIDTH)[None, :] // SSD_HEAD_DIM
           ).astype(BF16)
    cw, cb = conv_w[0], conv_b[0].reshape(1, CONV_DIM)
    w_out1 = w_out[0, :D_MODEL].astype(BF16)
    w_out2 = w_out[0, D_MODEL:].astype(BF16)
    peer_up, peer_vtp = pack_peer_weights(peer_u[0], peer_v[0])
    prm = dict(norm_mem_w=norm_mem_w[0], w_mq=w_mq[0].astype(BF16), w_mo=w_mo[0].astype(BF16),
               norm_ffn_w=norm_ffn_w[0], peer_wq=peer_wq[0].astype(BF16),
               peer_keys=peer_keys[0].reshape(2 * PEER_HEADS, N_KEYS, PEER_HALF).astype(BF16),
               peer_u=peer_up, peer_vt=peer_vtp, final_norm_w=final_norm_w)

    qi = jnp.arange(WINDOW)[:, None] + WINDOW
    bias_p = rel_bias_rows(rel_bias, _t5_bucket(qi - jnp.arange(2 * WINDOW)[None, :]))
    dist_s = jnp.broadcast_to(jnp.maximum(WINDOW - jnp.arange(2 * WINDOW), 0)[None, :],
                              (8, 2 * WINDOW))
    bias_s = rel_bias_rows(rel_bias, _t5_bucket(dist_s))[:, 0, :]
    sinks = attn_sinks[0]

    xp = x_prompt.reshape(batch * seq, d)
    q, k, v, z, xbc, dt_raw = norm_matmul(xp, norm_mix_w[0], w_in_p, in_splits, tm)
    o_att = swa_prompt(q, k, v, bias_p, sinks, batch, seq)
    y_ssd, p_state = ssd_prompt(xbc, dt_raw, cw, cb, dtb, alog, dskip_x, e01, batch, seq)
    x1 = merge(xp, o_att, y_ssd, z, attn_norm_w[0], ssd_norm_w[0], w_out1, w_out2, tm)
    mkv_w = jnp.concatenate([w_mk[0], w_mv[0]], axis=1).astype(BF16)
    mk, mv = norm_matmul(mem_prompt.reshape(batch * N_MEM, d), mem_norm_w[0], mkv_w, (d, d), tm)
    y_p = _tail(x1, mk.astype(BF16), mv.astype(BF16), prm, seq, tm, True, tm)

    xs_in = x_sample.reshape(nb, d)
    tms = nb
    qs, ks, vs, zs, xbcs, dts = norm_matmul(xs_in, norm_mix_w[0], w_in_p, in_splits, tms)
    own = (jnp.arange(kvw)[None, :] // ATT_HEAD_DIM) == (jnp.arange(ATT_HEADS)[:, None] // (ATT_HEADS // ATT_KV_HEADS))
    qx = jnp.where(own[None], jnp.tile(qs.reshape(nb, ATT_HEADS, ATT_HEAD_DIM), (1, 1, ATT_KV_HEADS)), 0.0)
    ck = cache_swa_k[0].reshape(nb, WINDOW, kvw)
    cv = cache_swa_v[0].reshape(nb, WINDOW, kvw)
    sink_b = jnp.broadcast_to(sinks[:, None], (ATT_HEADS, LANES))
    nk, nv, ox = swa_sample(qx, ck, cv, ks.reshape(nb, 1, kvw), vs.reshape(nb, 1, kvw), bias_s, sink_b)
    ox5 = ox.reshape(nb, ATT_KV_HEADS, ATT_HEADS // ATT_KV_HEADS, ATT_KV_HEADS, ATT_HEAD_DIM)
    o_att_s = jnp.stack([ox5[:, g, :, g, :] for g in range(ATT_KV_HEADS)], axis=1).reshape(nb, d)
    sc = state_conv[0]
    dec_t, xdt_t, bmat, cmat, xs_s = ssd_sample_prep(xbcs, sc[:, 0], sc[:, 1], sc[:, 2], dts, cw, cb, dtb, alog, e01)
    new_state, y_s = ssd_sample_step(state_ssm[0].reshape(nb, SSD_WIDTH, D_STATE), dec_t, xdt_t, bmat, cmat, xs_s, dskip_x)
    x1s = merge(xs_in, o_att_s, y_s, zs, attn_norm_w[0], ssd_norm_w[0], w_out1, w_out2, tms)
    y_s_out = _tail(x1s, cache_mem_k[0], cache_mem_v[0], prm, seq, tms, False, tms)

    k4 = k.reshape(batch, seq, ATT_KV_HEADS, ATT_HEAD_DIM)
    v4 = v.reshape(batch, seq, ATT_KV_HEADS, ATT_HEAD_DIM)
    xbc3 = xbc.reshape(batch, seq, CONV_DIM)
    return (y_p.reshape(batch, seq, d),
            y_s_out.reshape(nb, 1, d),
            k4[None, :, seq - WINDOW:],
            v4[None, :, seq - WINDOW:],
            p_state.reshape(1, batch, SSD_HEADS, SSD_HEAD_DIM, D_STATE),
            xbc3[None, :, seq - (CONV_WIDTH - 1):],
            mk.reshape(1, batch, N_MEM, MEM_HEADS, MEM_HEAD_DIM),
            mv.reshape(1, batch, N_MEM, MEM_HEADS, MEM_HEAD_DIM),
            nk.reshape(1, nb, WINDOW, ATT_KV_HEADS, ATT_HEAD_DIM),
            nv.reshape(1, nb, WINDOW, ATT_KV_HEADS, ATT_HEAD_DIM),
            new_state.reshape(1, nb, SSD_HEADS, SSD_HEAD_DIM, D_STATE),
            jnp.concatenate([sc[:, 1:], xbcs[:, None, :]], axis=1)[None])
```
